```python
import math
import jax
import jax.numpy as jnp
from jax import lax
import numpy as np

D_MODEL = 1024
BATCH = 4
SEQ = 8192
DEPTH = 2

GRID_W = 64
CTX_LEN = 256
ROPE_BASE = 10000.0
NORM_EPS = 1e-6
LB_FLOOR = 1e-30
Q_BLOCK = 128
CHUNK = 64

GLA_HEADS = 4
GLA_DK = 32
GLA_DV = 64
GLA_GATE_RANK = 16
GLA_GATE_NORM = 16.0
HG_HEADS = 4
HG_DIM = 64
MLA_HEADS = 4
MLA_Q_RANK = 192
MLA_KV_RANK = 128
MLA_NOPE = 64
MLA_ROPE = 32
MLA_DV = 64
DIFF_HEADS = 4
DIFF_DQK = 32
DIFF_DV = 64
PEER_HEADS = 8
PEER_NKEYS = 128
PEER_N_EXPERTS = PEER_NKEYS * PEER_NKEYS
PEER_DQ = 256
PEER_TOPK = 16
TOKEN_BLOCK = 128

GLA_COLS = (GLA_HEADS * GLA_DK, GLA_HEADS * GLA_DK, GLA_HEADS * GLA_DV, GLA_HEADS * GLA_DV, GLA_GATE_RANK, GLA_GATE_RANK)
HG_COLS = (HG_HEADS * HG_DIM,) * 5
MLA_COLS = (MLA_Q_RANK, MLA_KV_RANK, MLA_ROPE)
DIFF_COLS = (DIFF_HEADS * 2 * DIFF_DQK, DIFF_HEADS * 2 * DIFF_DQK, DIFF_HEADS * DIFF_DV)
MIXER_COLS = (sum(GLA_COLS), sum(HG_COLS), sum(MLA_COLS), sum(DIFF_COLS))
IN_COLS = sum(MIXER_COLS)
MIX_WIDTH = GLA_HEADS * GLA_DV + HG_HEADS * HG_DIM + MLA_HEADS * MLA_DV + DIFF_HEADS * DIFF_DV

kernel_name = 'hybrid_prefix_dit_gla_hgrn2_mla_diff_peer'


def rms_norm(x, g):
    xf = x.astype(jnp.float32)
    y = xf * lax.rsqrt(jnp.mean(xf * xf, axis=-1, keepdims=True) + NORM_EPS)
    return (y * g.astype(jnp.float32)).astype(x.dtype)


def modulate(h, shift, scale):
    return h * (1 + scale) + shift


def split_cols(a, sizes):
    return jnp.split(a, np.cumsum(sizes)[:-1].tolist(), axis=-1)


def to_heads(a, n):
    B, L, _ = a.shape
    return a.reshape(B, L, n, -1).transpose(0, 2, 1, 3)


def bhld(a):
    return a.transpose(0, 2, 1, 3)


def merge_heads(a):
    B, n, L, d = a.shape
    return a.transpose(0, 2, 1, 3).reshape(B, L, n * d)


def rope_1d(x, pos):
    half = x.shape[-1] // 2
    freqs = ROPE_BASE ** (-jnp.arange(half, dtype=jnp.float32) / half)
    ang = pos.astype(jnp.float32)[:, None, None] * freqs
    cos, sin = jnp.cos(ang), jnp.sin(ang)
    x1 = x[..., :half].astype(jnp.float32)
    x2 = x[..., half:].astype(jnp.float32)
    return jnp.concatenate([x1 * cos - x2 * sin, x1 * sin + x2 * cos], axis=-1).astype(x.dtype)


def rope_2d(x, rows, cols):
    h = x.shape[-1] // 2
    return jnp.concatenate([rope_1d(x[..., :h], rows), rope_1d(x[..., h:], cols)], axis=-1)


def blocked_attention(q, k, v, scale):
    B, H, S, dq = q.shape
    dv = v.shape[-1]
    nb = S // Q_BLOCK
    qb = q.reshape(B, H, nb, Q_BLOCK, dq).transpose(2, 0, 1, 3, 4)

    def one(qi):
        s = jnp.einsum('bhqd,bhkd->bhqk', qi, k).astype(jnp.float32) * scale
        p = jax.nn.softmax(s, axis=-1)
        return jnp.einsum('bhqk,bhkv->bhqv', p.astype(v.dtype), v)

    o = lax.map(one, qb)
    return o.transpose(1, 2, 0, 3, 4).reshape(B, H, S, dv)


def gated_chunk_scan(q, k, v, log_g, s0):
    B, H, L, dk = q.shape
    dv = v.shape[-1]
    n = L // CHUNK

    def chunks(a):
        return a.reshape(B, H, n, CHUNK, a.shape[-1]).transpose(2, 0, 1, 3, 4)

    lower = jnp.tril(jnp.ones((CHUNK, CHUNK), dtype=bool))[:, :, None]

    def step(state, inp):
        qi, ki, vi, gi = [a.astype(jnp.float32) for a in inp]
        b = jnp.cumsum(gi, axis=2)
        inter = jnp.einsum('bhtk,bhkv->bhtv', qi * jnp.exp(b), state)
        rel = b[:, :, :, None, :] - b[:, :, None, :, :]
        decay = jnp.where(lower, jnp.exp(jnp.where(lower, rel, 0.0)), 0.0)
        scores = jnp.einsum('bhtk,bhsk,bhtsk->bhts', qi, ki, decay)
        intra = jnp.einsum('bhts,bhsv->bhtv', scores, vi)
        b_end = b[:, :, -1:, :]
        new_state = (jnp.exp(b_end[:, :, 0, :])[..., None] * state
                     + jnp.einsum('bhsk,bhsv->bhkv', ki * jnp.exp(b_end - b), vi))
        return new_state, inter + intra

    s_final, o = lax.scan(step, s0.astype(jnp.float32), (chunks(q), chunks(k), chunks(v), chunks(log_g)))
    o = o.transpose(1, 2, 0, 3, 4).reshape(B, H, L, dv)
    return o.astype(v.dtype), s_final


def bidirectional_prefix_scan(ctx_in, lat_in):
    q_c, kf_c, kb_c, v_c, gf_c, gb_c = ctx_in
    q_l, kf_l, kb_l, v_l, gf_l, gb_l = lat_in
    B, H, _, dk = q_c.shape
    dv = v_c.shape[-1]
    zero = jnp.zeros((B, H, dk, dv), jnp.float32)

    def flip(a):
        return jnp.flip(a, axis=2)

    o_cf, s_f = gated_chunk_scan(q_c, kf_c, v_c, gf_c, zero)
    o_cb, s_b = gated_chunk_scan(flip(q_c), flip(kb_c), flip(v_c), flip(gb_c), zero)
    o_lf, _ = gated_chunk_scan(q_l, kf_l, v_l, gf_l, s_f)
    o_lb, _ = gated_chunk_scan(flip(q_l), flip(kb_l), flip(v_l), flip(gb_l), s_b)
    return o_cf + flip(o_cb), o_lf + flip(o_lb)


def gla_mixer(p_c, p_l, w_gate, b_gate, g_norm, need_ctx):
    def prep(p):
        q, k, v, r, zf, zb = split_cols(p, GLA_COLS)
        gf = jax.nn.log_sigmoid((zf @ w_gate[0] + b_gate[0]).astype(jnp.float32)) / GLA_GATE_NORM
        gb = jax.nn.log_sigmoid((zb @ w_gate[1] + b_gate[1]).astype(jnp.float32)) / GLA_GATE_NORM
        q = to_heads(q * GLA_DK ** -0.5, GLA_HEADS)
        k = to_heads(k, GLA_HEADS)
        return (q, k, k, to_heads(v, GLA_HEADS), to_heads(gf, GLA_HEADS), to_heads(gb, GLA_HEADS)), r

    c_in, r_c = prep(p_c)
    l_in, r_l = prep(p_l)
    o_c, o_l = bidirectional_prefix_scan(c_in, l_in)

    def finish(o, r):
        return merge_heads(rms_norm(o, g_norm)) * jax.nn.silu(r)

    return (finish(o_c, r_c) if need_ctx else None), finish(o_l, r_l)


def hgrn2_mixer(p_c, p_l, lb, g_norm, need_ctx):
    lb = lb.astype(jnp.float32)
    log_lb = jnp.log(jnp.maximum(lb, LB_FLOOR))
    log_1mlb = jnp.log1p(-lb)

    def gates(z):
        z = z.astype(jnp.float32)
        log_f = jnp.logaddexp(log_lb, log_1mlb + jax.nn.log_sigmoid(z))
        k = (1.0 - lb) * jax.nn.sigmoid(-z)
        return to_heads(k, HG_HEADS), to_heads(log_f, HG_HEADS)

    def prep(p):
        q, i, og, zf, zb = split_cols(p, HG_COLS)
        kf, gf = gates(zf)
        kb, gb = gates(zb)
        return (to_heads(jax.nn.silu(q), HG_HEADS), kf, kb, to_heads(i, HG_HEADS), gf, gb), og

    c_in, og_c = prep(p_c)
    l_in, og_l = prep(p_l)
    o_c, o_l = bidirectional_prefix_scan(c_in, l_in)

    def finish(o, og):
        return merge_heads(rms_norm(o, g_norm)) * jax.nn.silu(og)

    return (finish(o_c, og_c) if need_ctx else None), finish(o_l, og_l)


def mla_mixer(p_c, p_l, g_q, g_kv, w_uq, w_ukv, pos, need_ctx):
    def prep(p, grid_pos):
        B, L, _ = p.shape
        c_q, c_kv, k_rope = split_cols(p, MLA_COLS)
        q = (rms_norm(c_q, g_q) @ w_uq).reshape(B, L, MLA_HEADS, MLA_NOPE + MLA_ROPE)
        kv = (rms_norm(c_kv, g_kv) @ w_ukv).reshape(B, L, MLA_HEADS, MLA_NOPE + MLA_DV)
        q_nope, q_rope = q[..., :MLA_NOPE], q[..., MLA_NOPE:]
        k_nope, v = kv[..., :MLA_NOPE], kv[..., MLA_NOPE:]
        k_rope = k_rope[:, :, None, :]
        if grid_pos is not None:
            q_rope = rope_2d(q_rope, *grid_pos)
            k_rope = rope_2d(k_rope, *grid_pos)
        q = jnp.concatenate([q_nope, q_rope], axis=-1)
        k = jnp.concatenate([k_nope, jnp.broadcast_to(k_rope, (B, L, MLA_HEADS, MLA_ROPE))], axis=-1)
        return bhld(q), bhld(k), bhld(v)

    scale = (MLA_NOPE + MLA_ROPE) ** -0.5
    q_c, k_c, v_c = prep(p_c, None)
    q_l, k_l, v_l = prep(p_l, pos)
    o_l = blocked_attention(q_l, jnp.concatenate([k_c, k_l], axis=2), jnp.concatenate([v_c, v_l], axis=2), scale)
    o_c = merge_heads(blocked_attention(q_c, k_c, v_c, scale)) if need_ctx else None
    return o_c, merge_heads(o_l)


def diff_mixer(p_c, p_l, lam_params, g_norm, lambda_init, pos, need_ctx):
    lq1, lk1, lq2, lk2 = lam_params[0], lam_params[1], lam_params[2], lam_params[3]
    lam = (jnp.exp(jnp.sum(lq1 * lk1).astype(jnp.float32)) - jnp.exp(jnp.sum(lq2 * lk2).astype(jnp.float32))
           + lambda_init)

    def prep(p, grid_pos):
        B, L, _ = p.shape
        q, k, v = split_cols(p, DIFF_COLS)
        q = q.reshape(B, L, 2 * DIFF_HEADS, DIFF_DQK)
        k = k.reshape(B, L, 2 * DIFF_HEADS, DIFF_DQK)
        v = jnp.repeat(v.reshape(B, L, DIFF_HEADS, DIFF_DV), 2, axis=2)
        if grid_pos is not None:
            q = rope_2d(q, *grid_pos)
            k = rope_2d(k, *grid_pos)
        return bhld(q), bhld(k), bhld(v)

    def finish(o):
        B, _, L, dv = o.shape
        o = o.reshape(B, DIFF_HEADS, 2, L, dv)
        o = o[:, :, 0] - lam.astype(o.dtype) * o[:, :, 1]
        return merge_heads(rms_norm(o, g_norm) * (1.0 - lambda_init))

    scale = DIFF_DQK ** -0.5
    q_c, k_c, v_c = prep(p_c, None)
    q_l, k_l, v_l = prep(p_l, pos)
    o_l = blocked_attention(q_l, jnp.concatenate([k_c, k_l], axis=2), jnp.concatenate([v_c, v_l], axis=2), scale)
    o_c = finish(blocked_attention(q_c, k_c, v_c, scale)) if need_ctx else None
    return o_c, finish(o_l)


def peer_ffn(h, w_q, keys, u, v):
    B, L, D = h.shape
    tok = h.reshape(-1, TOKEN_BLOCK, D)

    def one(xb):
        T = xb.shape[0]
        q = (xb @ w_q).reshape(T, PEER_HEADS, 2, PEER_DQ // 2)
        s = jnp.einsum('thpd,hpnd->thpn', q, keys).astype(jnp.float32)
        s_top, i_top = lax.top_k(s, PEER_TOPK)
        cand = (s_top[:, :, 0, :, None] + s_top[:, :, 1, None, :]).reshape(T, PEER_HEADS, -1)
        cidx = (i_top[:, :, 0, :, None] * PEER_NKEYS + i_top[:, :, 1, None, :]).reshape(T, PEER_HEADS, -1)
        best, sel = lax.top_k(cand, PEER_TOPK)
        eidx = jnp.take_along_axis(cidx, sel, axis=-1)
        g = jax.nn.softmax(best, axis=-1)
        act = jax.nn.gelu(jnp.einsum('td,thkd->thk', xb, u[eidx]).astype(jnp.float32))
        return jnp.einsum('thk,thkd->td', (g * act).astype(v.dtype), v[eidx]).astype(h.dtype)

    return lax.map(one, tok).reshape(B, L, D)


def setup_inputs(seed: int = 0) -> dict:
    key = jax.random.key(seed)
    ks = jax.random.split(key, 26)
    D = D_MODEL

    def nrm(k, shape, scale):
        return jax.random.normal(k, shape, jnp.float32) * scale

    def gain(k, shape):
        return 1.0 + nrm(k, shape, 0.02)

    return {
        'x': nrm(ks[0], (BATCH, SEQ, D), 1.0),
        'c': nrm(ks[1], (BATCH, D), 1.0),
        'ctx': nrm(ks[2], (BATCH, CTX_LEN, D), 1.0),
        'c_ctx': nrm(ks[3], (D,), 1.0),
        'ada_w': nrm(ks[4], (DEPTH, D, 6 * D), 0.5 * D ** -0.5),
        'ada_b': nrm(ks[5], (DEPTH, 6 * D), 0.02),
        'norm_mix_g': gain(ks[6], (DEPTH, D)),
        'norm_ffn_g': gain(ks[7], (DEPTH, D)),
        'w_in': nrm(ks[8], (DEPTH, D, IN_COLS), D ** -0.5),
        'w_out': nrm(ks[9], (DEPTH, MIX_WIDTH, D), MIX_WIDTH ** -0.5),
        'gla_gate_w': nrm(ks[10], (DEPTH, 2, GLA_GATE_RANK, GLA_HEADS * GLA_DK), GLA_GATE_RANK ** -0.5),
        'gla_gate_b': nrm(ks[11], (DEPTH, 2, GLA_HEADS * GLA_DK), 0.1),
        'gla_norm_g': gain(ks[12], (DEPTH, GLA_DV)),
        'hgrn_lb_raw': nrm(ks[13], (DEPTH, HG_HEADS * HG_DIM), 0.1),
        'hgrn_norm_g': gain(ks[14], (DEPTH, HG_DIM)),
        'mla_q_norm_g': gain(ks[15], (DEPTH, MLA_Q_RANK)),
        'mla_kv_norm_g': gain(ks[16], (DEPTH, MLA_KV_RANK)),
        'mla_w_uq': nrm(ks[17], (DEPTH, MLA_Q_RANK, MLA_HEADS * (MLA_NOPE + MLA_ROPE)), MLA_Q_RANK ** -0.5),
        'mla_w_ukv': nrm(ks[18], (DEPTH, MLA_KV_RANK, MLA_HEADS * (MLA_NOPE + MLA_DV)), MLA_KV_RANK ** -0.5),
        'diff_lambda': nrm(ks[19], (DEPTH, 4, DIFF_DQK), 0.1),
        'diff_norm_g': gain(ks[20], (DEPTH, DIFF_DV)),
        'peer_wq': nrm(ks[21], (DEPTH, D, PEER_HEADS * PEER_DQ), D ** -0.5),
        'peer_keys': nrm(ks[22], (DEPTH, PEER_HEADS, 2, PEER_NKEYS, PEER_DQ // 2), (PEER_DQ // 2) ** -0.5),
        'peer_u': nrm(ks[23], (DEPTH, PEER_N_EXPERTS, D), D ** -0.5),
        'peer_v': nrm(ks[24], (DEPTH, PEER_N_EXPERTS, D), 0.3),
        'final_norm_g': gain(ks[25], (D,)),
    }


def reference(x, c, ctx, c_ctx, ada_w, ada_b, norm_mix_g, norm_ffn_g, w_in, w_out,
              gla_gate_w, gla_gate_b, gla_norm_g, hgrn_lb_raw, hgrn_norm_g,
              mla_q_norm_g, mla_kv_norm_g, mla_w_uq, mla_w_ukv, diff_lambda, diff_norm_g,
              peer_wq, peer_keys, peer_u, peer_v, final_norm_g):
    L = x.shape[1]
    ROWS = L // GRID_W
    rows = jnp.repeat(jnp.arange(ROWS, dtype=jnp.int32), GRID_W)
    cols = jnp.tile(jnp.arange(GRID_W, dtype=jnp.int32), ROWS)
    pos = (rows, cols)

    lb_sm = jax.nn.softmax(hgrn_lb_raw.astype(jnp.float32), axis=0)
    lb_all = jnp.cumsum(lb_sm, axis=0) - lb_sm[0]

    silu_c = jax.nn.silu(c)
    silu_cc = jax.nn.silu(c_ctx)
    xc = ctx
    for l in range(DEPTH):
        need_ctx = l < DEPTH - 1
        lambda_init = 0.8 - 0.6 * math.exp(-0.3 * l)
        mod_l = jnp.split((silu_c @ ada_w[l] + ada_b[l])[:, None, :], 6, axis=-1)
        mod_c = jnp.split(silu_cc @ ada_w[l] + ada_b[l], 6, axis=-1)

        h_l = modulate(rms_norm(x, norm_mix_g[l]), mod_l[0], mod_l[1])
        h_c = modulate(rms_norm(xc, norm_mix_g[l]), mod_c[0], mod_c[1])
        gla_l, hg_l, mla_l, dif_l = split_cols(h_l @ w_in[l], MIXER_COLS)
        gla_c, hg_c, mla_c, dif_c = split_cols(h_c @ w_in[l], MIXER_COLS)

        a_c, a_l = gla_mixer(gla_c, gla_l, gla_gate_w[l], gla_gate_b[l], gla_norm_g[l], need_ctx)
        b_c, b_l = hgrn2_mixer(hg_c, hg_l, lb_all[l], hgrn_norm_g[l], need_ctx)
        m_c, m_l = mla_mixer(mla_c, mla_l, mla_q_norm_g[l], mla_kv_norm_g[l], mla_w_uq[l], mla_w_ukv[l], pos, need_ctx)
        d_c, d_l = diff_mixer(dif_c, dif_l, diff_lambda[l], diff_norm_g[l], lambda_init, pos, need_ctx)

        x = x + mod_l[2] * (jnp.concatenate([a_l, b_l, m_l, d_l], axis=-1) @ w_out[l])
        x = x + mod_l[5] * peer_ffn(modulate(rms_norm(x, norm_ffn_g[l]), mod_l[3], mod_l[4]),
                                    peer_wq[l], peer_keys[l], peer_u[l], peer_v[l])
        if need_ctx:
            xc = xc + mod_c[2] * (jnp.concatenate([a_c, b_c, m_c, d_c], axis=-1) @ w_out[l])
            xc = xc + mod_c[5] * peer_ffn(modulate(rms_norm(xc, norm_ffn_g[l]), mod_c[3], mod_c[4]),
                                          peer_wq[l], peer_keys[l], peer_u[l], peer_v[l])
    return rms_norm(x, final_norm_g)
```

```python
import functools
import math

import jax
import jax.numpy as jnp
import numpy as np
from jax import lax
from jax.experimental import pallas as pl
from jax.experimental.pallas import tpu as pltpu

F32 = jnp.float32
BF16 = jnp.bfloat16

NORM_EPS = 1e-6
ROPE_BASE = 10000.0
GRID_W = 64
LB_FLOOR = 1e-30

GLA_HEADS, GLA_DK, GLA_DV, GLA_GATE_RANK, GLA_GATE_NORM = 4, 32, 64, 16, 16.0
HG_HEADS, HG_DIM = 4, 64
MLA_HEADS, MLA_Q_RANK, MLA_KV_RANK, MLA_NOPE, MLA_ROPE, MLA_DV = 4, 192, 128, 64, 32, 64
DIFF_HEADS, DIFF_DQK, DIFF_DV = 4, 32, 64
PEER_HEADS, PEER_NKEYS, PEER_DQ, PEER_TOPK = 8, 128, 256, 16

LANES = 128
VMEM_LIMIT = 48 * 1024 * 1024
EXP_CAP = 80.0
NEG_BIG = -1e30

COL_GLA, COL_HG, COL_MLA, COL_DQ, COL_DK, COL_DV, MAIN_COLS = 0, 768, 1536, 2048, 2560, 3072, 3328
GATE_COLS = 768

STAIR = [(a, b) for a in range(PEER_TOPK) for b in range(PEER_TOPK) if (a + 1) * (b + 1) <= PEER_TOPK]


def _cparams(sem):
    return pltpu.CompilerParams(dimension_semantics=sem, vmem_limit_bytes=VMEM_LIMIT)


def _sigmoid(x):
    return 1.0 / (1.0 + jnp.exp(-x))


def _logsig(x):
    return jnp.minimum(x, 0.0) - jnp.log(1.0 + jnp.exp(-jnp.abs(x)))


def _nt(a, b):
    return lax.dot_general(a, b, (((1,), (1,)), ((), ())), preferred_element_type=F32)


def _tn(a, b):
    return lax.dot_general(a, b, (((0,), (0,)), ((), ())), preferred_element_type=F32)


def _adaln_kernel(c_ref, w_ref, b_ref, o_ref):
    c = c_ref[...]
    s = c * _sigmoid(c)
    o_ref[...] = jnp.dot(s.astype(BF16), w_ref[...].astype(BF16), preferred_element_type=F32) + b_ref[...]


def _adaln(cpad, ada_w, ada_b):
    depth, d, n6 = ada_w.shape
    rows = cpad.shape[0]
    tn = 512
    return pl.pallas_call(
        _adaln_kernel,
        grid=(depth, n6 // tn),
        in_specs=[
            pl.BlockSpec((rows, d), lambda l, j: (0, 0)),
            pl.BlockSpec((None, d, tn), lambda l, j: (l, 0, j)),
            pl.BlockSpec((None, 1, tn), lambda l, j: (l, 0, j)),
        ],
        out_specs=pl.BlockSpec((None, rows, tn), lambda l, j: (l, 0, j)),
        out_shape=jax.ShapeDtypeStruct((depth, rows, n6), F32),
        compiler_params=_cparams(("parallel", "parallel")),
        name="adaln",
    )(cpad, ada_w, ada_b.reshape(depth, 1, n6))


def _modnorm(x, g, mod_ref, shift_row, scale_row):
    ms = jnp.mean(x * x, axis=-1, keepdims=True)
    h = x * lax.rsqrt(ms + NORM_EPS) * g
    return h * (1.0 + mod_ref[scale_row:scale_row + 1, :]) + mod_ref[shift_row:shift_row + 1, :]


def _inproj_kernel(x_ref, g_ref, mod_ref, w_ref, main_ref, gate_ref):
    h = _modnorm(x_ref[...], g_ref[...], mod_ref, 0, 1)
    y = jnp.dot(h.astype(BF16), w_ref[...], preferred_element_type=F32)
    main_ref[...] = y[:, :MAIN_COLS].astype(BF16)
    gate_ref[...] = y[:, MAIN_COLS:]


def _inproj(x, g, mod, w_perm):
    b, l, d = x.shape
    tm = min(256, l)
    nc = w_perm.shape[1]
    return pl.pallas_call(
        _inproj_kernel,
        grid=(b, l // tm),
        in_specs=[
            pl.BlockSpec((None, tm, d), lambda bi, i: (bi, i, 0)),
            pl.BlockSpec((1, d), lambda bi, i: (0, 0)),
            pl.BlockSpec((None, 6, d), lambda bi, i: (bi, 0, 0)),
            pl.BlockSpec((d, nc), lambda bi, i: (0, 0)),
        ],
        out_specs=[
            pl.BlockSpec((None, tm, MAIN_COLS), lambda bi, i: (bi, i, 0)),
            pl.BlockSpec((None, tm, GATE_COLS), lambda bi, i: (bi, i, 0)),
        ],
        out_shape=[
            jax.ShapeDtypeStruct((b, l, MAIN_COLS), BF16),
            jax.ShapeDtypeStruct((b, l, GATE_COLS), F32),
        ],
        compiler_params=_cparams(("parallel", "parallel")),
        name="inproj",
    )(x, g.reshape(1, d), mod, w_perm)


def _scan_kernel(*refs, mixer, reverse, finish, chunk):
    if finish:
        main_ref, gate_ref, pa_ref, pb_ref, gn_ref, s0_ref, oprev_ref, o_ref, sfin_ref, st_scr = refs
    else:
        main_ref, gate_ref, pa_ref, pb_ref, gn_ref, s0_ref, o_ref, sfin_ref, st_scr = refs
        oprev_ref = None
    i = pl.program_id(1)

    @pl.when(i == 0)
    def _():
        st_scr[...] = s0_ref[...]

    main = main_ref[...]
    t = main.shape[0]
    c = chunk
    nc = t // c
    heads = 4
    dv = 256
    v = main[:, 256:512]
    gcol = main[:, 512:768].astype(F32)
    if mixer == "gla":
        dk = 128
        q = main[:, 0:128].astype(F32) * (GLA_DK ** -0.5)
        k = main[:, 128:256].astype(F32)
        z = jnp.dot(gate_ref[...].astype(BF16), pa_ref[...], preferred_element_type=F32) + pb_ref[...]
        g = _logsig(z) * (1.0 / GLA_GATE_NORM)
    else:
        dk = 256
        qq = main[:, 0:256].astype(F32)
        q = qq * _sigmoid(qq)
        z = gate_ref[...]
        la = pa_ref[0:1, :]
        lb = pa_ref[1:2, :] + _logsig(z)
        g = jnp.maximum(la, lb) + jnp.log(1.0 + jnp.exp(-jnp.abs(la - lb)))
        k = pa_ref[2:3, :] * _sigmoid(-z)
    hk = dk // heads
    hv = dv // heads

    rt = lax.broadcasted_iota(jnp.int32, (t, t), 0)
    ct = lax.broadcasted_iota(jnp.int32, (t, t), 1)
    same = (rt // c) == (ct // c)
    tri = jnp.where(same & ((ct >= rt) if reverse else (ct <= rt)), 1.0, 0.0).astype(BF16)
    g1 = g.astype(BF16)
    r1 = g - g1.astype(F32)
    g2 = r1.astype(BF16)
    g3 = (r1 - g2.astype(F32)).astype(BF16)
    bcum = (jnp.dot(tri, g1, preferred_element_type=F32) + jnp.dot(tri, g2, preferred_element_type=F32)
            + jnp.dot(tri, g3, preferred_element_type=F32))

    b3 = bcum.reshape(nc, c, dk)
    q3 = q.reshape(nc, c, dk)
    k3 = k.reshape(nc, c, dk)
    if reverse:
        bmid = b3[:, c // 2:c // 2 + 1, :]
        bend = b3[:, 0:1, :]
    else:
        bmid = b3[:, c // 2 - 1:c // 2, :]
        bend = b3[:, c - 1:c, :]
    eb = b3 - bmid
    qa = (q3 * jnp.exp(jnp.minimum(eb, EXP_CAP))).reshape(t, dk).astype(BF16)
    ka = (k3 * jnp.exp(jnp.minimum(-eb, EXP_CAP))).reshape(t, dk).astype(BF16)
    qi = (q3 * jnp.exp(b3)).reshape(t, dk).astype(BF16)
    kb = (k3 * jnp.exp(bend - b3)).reshape(t, dk).astype(BF16)
    dec = jnp.exp(bend)

    lane_k = lax.broadcasted_iota(jnp.int32, (1, dk), 1) // hk
    lane_v = lax.broadcasted_iota(jnp.int32, (1, dv), 1) // hv
    rr = lax.broadcasted_iota(jnp.int32, (c, heads * c), 0)
    cc = lax.broadcasted_iota(jnp.int32, (c, heads * c), 1) % c
    causal = (cc >= rr) if reverse else (cc <= rr)
    bd = (lax.broadcasted_iota(jnp.int32, (dv, dk), 0) // hv) == (lax.broadcasted_iota(jnp.int32, (dv, dk), 1) // hk)

    o_intra = []
    upd = []
    for ci in range(nc):
        sl = slice(ci * c, (ci + 1) * c)
        ka_c = ka[sl]
        v_c = v[sl]
        kst = jnp.concatenate([jnp.where(lane_k == h, ka_c, 0) for h in range(heads)], axis=0)
        vst = jnp.concatenate([jnp.where(lane_v == h, v_c, 0) for h in range(heads)], axis=0)
        sw = _nt(qa[sl], kst)
        p = jnp.where(causal, sw, 0.0).astype(BF16)
        o_intra.append(jnp.dot(p, vst, preferred_element_type=F32))
        upd.append(jnp.where(bd, _tn(v_c, kb[sl]), 0.0))

    outs = [None] * nc
    order = range(nc - 1, -1, -1) if reverse else range(nc)
    for ci in order:
        sl = slice(ci * c, (ci + 1) * c)
        st = st_scr[...]
        outs[ci] = o_intra[ci] + _nt(qi[sl], st.astype(BF16))
        st_scr[...] = st * dec[ci] + upd[ci]
    o = jnp.concatenate(outs, axis=0)

    @pl.when(i == pl.num_programs(1) - 1)
    def _():
        sfin_ref[...] = st_scr[...]

    if not finish:
        o_ref[...] = o
    else:
        ot = o + oprev_ref[...]
        sq = ot * ot
        jv = ((lax.broadcasted_iota(jnp.int32, (dv, dv), 0) // hv)
              == (lax.broadcasted_iota(jnp.int32, (dv, dv), 1) // hv))
        jm = jnp.where(jv, 1.0, 0.0).astype(BF16)
        s1 = sq.astype(BF16)
        s2 = (sq - s1.astype(F32)).astype(BF16)
        ms = (jnp.dot(s1, jm, preferred_element_type=F32) + jnp.dot(s2, jm, preferred_element_type=F32)) * (1.0 / hv)
        y = ot * lax.rsqrt(ms + NORM_EPS) * gn_ref[...]
        o_ref[...] = (y * (gcol * _sigmoid(gcol))).astype(BF16)


def _scan(main, gate, pa, pb, gn, s0, oprev, *, mixer, reverse, chunk):
    b, l, _ = main.shape
    t = min(256, l)
    nb = l // t
    finish = oprev is not None
    dk = 128 if mixer == "gla" else 256
    dv = 256
    main_blk = 0 if mixer == "gla" else 1
    gate_blk = 0 if mixer == "gla" else (2 if reverse else 1)

    def tok(bi, i):
        return (nb - 1 - i) if reverse else i

    in_specs = [
        pl.BlockSpec((None, t, 768), lambda bi, i: (bi, tok(bi, i), main_blk)),
        pl.BlockSpec((None, t, 256), lambda bi, i: (bi, tok(bi, i), gate_blk)),
        pl.BlockSpec(pa.shape, lambda bi, i: (0, 0)),
        pl.BlockSpec(pb.shape, lambda bi, i: (0, 0)),
        pl.BlockSpec((1, dv), lambda bi, i: (0, 0)),
        pl.BlockSpec((None, dv, dk), lambda bi, i: (bi, 0, 0)),
    ]
    args = [main, gate, pa, pb, gn, s0]
    if finish:
        in_specs.append(pl.BlockSpec((None, t, dv), lambda bi, i: (bi, tok(bi, i), 0)))
        args.append(oprev)
    return pl.pallas_call(
        functools.partial(_scan_kernel, mixer=mixer, reverse=reverse, finish=finish, chunk=chunk),
        grid=(b, nb),
        in_specs=in_specs,
        out_specs=[
            pl.BlockSpec((None, t, dv), lambda bi, i: (bi, tok(bi, i), 0)),
            pl.BlockSpec((None, dv, dk), lambda bi, i: (bi, 0, 0)),
        ],
        out_shape=[
            jax.ShapeDtypeStruct((b, l, dv), BF16 if finish else F32),
            jax.ShapeDtypeStruct((b, dv, dk), F32),
        ],
        scratch_shapes=[pltpu.VMEM((dv, dk), F32)],
        compiler_params=_cparams(("parallel", "arbitrary")),
        name=f"scan_{mixer}_{'bwd' if reverse else 'fwd'}",
    )(*args)


def _rope(x, cos, sin_lo, sin_hi):
    return x * cos + pltpu.roll(x, LANES - 8, 1) * sin_lo + pltpu.roll(x, 8, 1) * sin_hi


def _rope_tables(l, first_lane, n_lanes, rotate):
    lane = np.arange(LANES)
    d = (lane - first_lane) % 32
    active = (lane >= first_lane) & (lane < first_lane + n_lanes)
    freqs = ROPE_BASE ** (-(d % 8).astype(np.float32) / 8.0)
    tt = jnp.arange(l, dtype=jnp.int32)
    rows = (tt // GRID_W).astype(F32)[:, None]
    cols = (tt % GRID_W).astype(F32)[:, None]
    pos = jnp.where(jnp.asarray(d < 16)[None, :], rows, cols)
    ang = pos * jnp.asarray(freqs, F32)[None, :]
    act = jnp.asarray(active)[None, :] & rotate
    lo = jnp.asarray((d % 16) < 8)[None, :]
    cos = jnp.where(act, jnp.cos(ang), 1.0)
    sin = jnp.where(act, jnp.sin(ang), 0.0)
    return cos, jnp.where(lo, -sin, 0.0), jnp.where(lo, 0.0, sin)


def _mla_prep_kernel(slab_ref, gq_ref, gkv_ref, wq_ref, wkv_ref, cos_ref, slo_ref, shi_ref, q_ref, k_ref, v_ref):
    slab = slab_ref[...]
    cos, slo, shi = cos_ref[...], slo_ref[...], shi_ref[...]
    cq = slab[:, 0:256].astype(F32)
    qn = cq * lax.rsqrt(jnp.sum(cq * cq, axis=-1, keepdims=True) * (1.0 / MLA_Q_RANK) + NORM_EPS) * gq_ref[...]
    qall = jnp.dot(qn.astype(BF16), wq_ref[...], preferred_element_type=F32)
    ckv = slab[:, 256:384].astype(F32)
    kvn = ckv * lax.rsqrt(jnp.mean(ckv * ckv, axis=-1, keepdims=True) + NORM_EPS) * gkv_ref[...]
    kvall = jnp.dot(kvn.astype(BF16), wkv_ref[...], preferred_element_type=F32)
    kr = _rope(slab[:, 384:512].astype(F32), cos, slo, shi)
    scale = (MLA_NOPE + MLA_ROPE) ** -0.5
    for h in range(MLA_HEADS):
        qh = _rope(qall[:, h * LANES:(h + 1) * LANES], cos, slo, shi)
        q_ref[h] = (qh * scale).astype(BF16)
        k_ref[h] = (kvall[:, h * LANES:(h + 1) * LANES] + kr).astype(BF16)
    v_ref[...] = kvall[:, MLA_HEADS * LANES:].astype(BF16)


def _mla_prep(main, gq, gkv, wq, wkv, tables):
    b, l, _ = main.shape
    tm = min(256, l)
    cos, slo, shi = tables
    tab = pl.BlockSpec((tm, LANES), lambda bi, i: (i, 0))
    hd = pl.BlockSpec((None, MLA_HEADS, tm, LANES), lambda bi, i: (bi, 0, i, 0))
    return pl.pallas_call(
        _mla_prep_kernel,
        grid=(b, l // tm),
        in_specs=[
            pl.BlockSpec((None, tm, 512), lambda bi, i: (bi, i, COL_MLA // 512)),
            pl.BlockSpec(gq.shape, lambda bi, i: (0, 0)),
            pl.BlockSpec(gkv.shape, lambda bi, i: (0, 0)),
            pl.BlockSpec(wq.shape, lambda bi, i: (0, 0)),
            pl.BlockSpec(wkv.shape, lambda bi, i: (0, 0)),
            tab, tab, tab,
        ],
        out_specs=[hd, hd, pl.BlockSpec((None, tm, 256), lambda bi, i: (bi, i, 0))],
        out_shape=[
            jax.ShapeDtypeStruct((b, MLA_HEADS, l, LANES), BF16),
            jax.ShapeDtypeStruct((b, MLA_HEADS, l, LANES), BF16),
            jax.ShapeDtypeStruct((b, l, 256), BF16),
        ],
        compiler_params=_cparams(("parallel", "parallel")),
        name="mla_prep",
    )(main, gq, gkv, wq, wkv, cos, slo, shi)


def _diff_prep_kernel(q_in, k_in, cos_ref, slo_ref, shi_ref, q_ref, k_ref):
    cos, slo, shi = cos_ref[...], slo_ref[...], shi_ref[...]
    qs = q_in[...]
    ks = k_in[...]
    scale = DIFF_DQK ** -0.5
    for h in range(DIFF_HEADS):
        sl = slice(h * LANES, (h + 1) * LANES)
        q_ref[h] = (_rope(qs[:, sl].astype(F32), cos, slo, shi) * scale).astype(BF16)
        k_ref[h] = _rope(ks[:, sl].astype(F32), cos, slo, shi).astype(BF16)


def _diff_prep(main, tables):
    b, l, _ = main.shape
    tm = min(256, l)
    cos, slo, shi = tables
    tab = pl.BlockSpec((tm, LANES), lambda bi, i: (i, 0))
    hd = pl.BlockSpec((None, DIFF_HEADS, tm, LANES), lambda bi, i: (bi, 0, i, 0))
    return pl.pallas_call(
        _diff_prep_kernel,
        grid=(b, l // tm),
        in_specs=[
            pl.BlockSpec((None, tm, 512), lambda bi, i: (bi, i, COL_DQ // 512)),
            pl.BlockSpec((None, tm, 512), lambda bi, i: (bi, i, COL_DK // 512)),
            tab, tab, tab,
        ],
        out_specs=[hd, hd],
        out_shape=[jax.ShapeDtypeStruct((b, DIFF_HEADS, l, LANES), BF16)] * 2,
        compiler_params=_cparams(("parallel", "parallel")),
        name="diff_prep",
    )(main, main, cos, slo, shi)


def _attn_kernel(*refs, n_maps, has_lat, tk, lam_init):
    refs = list(refs)
    q_ref, kc_ref, vc_ref = refs[:3]
    pos = 3
    if has_lat:
        kl_ref, vl_ref = refs[pos:pos + 2]
        pos += 2
    if n_maps == 2:
        lam_ref, gn_ref = refs[pos:pos + 2]
        pos += 2
    o_ref = refs[pos]
    tq = q_ref.shape[1]
    lane = lax.broadcasted_iota(jnp.int32, (1, LANES), 1)

    def step(qm, kc, vc, carry):
        m, l, acc = carry
        s = _nt(qm, kc)
        m_new = jnp.maximum(m, jnp.max(s, axis=-1, keepdims=True))
        alpha = jnp.exp(m - m_new)
        p = jnp.exp(s - m_new)
        l = alpha * l + jnp.sum(p, axis=-1, keepdims=True)
        acc = alpha * acc + jnp.dot(p.astype(BF16), vc, preferred_element_type=F32)
        return m_new, l, acc

    heads_out = []
    for hh in range(2):
        q = q_ref[hh]
        maps = []
        for mi in range(n_maps):
            if n_maps == 1:
                qm = q
            else:
                qm = jnp.where((lane >= DIFF_DQK * mi) & (lane < DIFF_DQK * (mi + 1)), q, jnp.zeros_like(q))
            carry = (jnp.full((tq, 1), NEG_BIG, F32), jnp.zeros((tq, 1), F32), jnp.zeros((tq, LANES), F32))
            carry = step(qm, kc_ref[hh], vc_ref[...], carry)
            if has_lat:
                def body(j, cr, qm=qm, hh=hh):
                    off = pl.multiple_of(j * tk, tk)
                    return step(qm, kl_ref[hh, pl.ds(off, tk), :], vl_ref[pl.ds(off, tk), :], cr)

                carry = lax.fori_loop(0, kl_ref.shape[1] // tk, body, carry)
            m, l, acc = carry
            maps.append(acc / l)
        if n_maps == 1:
            heads_out.append(maps[0])
        else:
            oh = maps[0] - lam_ref[...] * maps[1]
            mine = (lane >= DIFF_DV * hh) & (lane < DIFF_DV * (hh + 1))
            ms = jnp.sum(jnp.where(mine, oh * oh, 0.0), axis=-1, keepdims=True) * (1.0 / DIFF_DV)
            heads_out.append(oh * lax.rsqrt(ms + NORM_EPS) * gn_ref[...] * (1.0 - lam_init))
    o_ref[...] = jnp.where(lane < 64, heads_out[0], heads_out[1]).astype(BF16)


def _attention(q, kc, vc, vc_blk0, kl, vl, vl_blk0, extra, *, n_maps, lam_init):
    b, nh, l, _ = q.shape
    lc = kc.shape[2]
    tq = min(256, l)
    has_lat = kl is not None
    in_specs = [
        pl.BlockSpec((None, 2, tq, LANES), lambda bi, hp, i: (bi, hp, i, 0)),
        pl.BlockSpec((None, 2, lc, LANES), lambda bi, hp, i: (bi, hp, 0, 0)),
        pl.BlockSpec((None, lc, LANES), lambda bi, hp, i: (bi, 0, vc_blk0 + hp)),
    ]
    args = [q, kc, vc]
    tk = 0
    if has_lat:
        ll = kl.shape[2]
        tk = min(512, ll)
        in_specs += [
            pl.BlockSpec((None, 2, ll, LANES), lambda bi, hp, i: (bi, hp, 0, 0)),
            pl.BlockSpec((None, ll, LANES), lambda bi, hp, i: (bi, 0, vl_blk0 + hp)),
        ]
        args += [kl, vl]
    if n_maps == 2:
        in_specs += [pl.BlockSpec((1, LANES), lambda bi, hp, i: (0, 0))] * 2
        args += list(extra)
    return pl.pallas_call(
        functools.partial(_attn_kernel, n_maps=n_maps, has_lat=has_lat, tk=tk, lam_init=lam_init),
        grid=(b, nh // 2, l // tq),
        in_specs=in_specs,
        out_specs=pl.BlockSpec((None, tq, LANES), lambda bi, hp, i: (bi, i, hp)),
        out_shape=jax.ShapeDtypeStruct((b, l, nh * 64), BF16),
        compiler_params=_cparams(("parallel", "parallel", "arbitrary")),
        name="attn_mla" if n_maps == 1 else "attn_diff",
    )(*args)


def _outproj_kernel(x_ref, a_ref, b_ref, m_ref, d_ref, w_ref, mod_ref, o_ref):
    mix = jnp.concatenate([a_ref[...], b_ref[...], m_ref[...], d_ref[...]], axis=-1)
    y = jnp.dot(mix, w_ref[...], preferred_element_type=F32)
    o_ref[...] = x_ref[...] + mod_ref[2:3, :] * y


def _outproj(x, a, bb, m, dd, w, mod):
    b, l, d = x.shape
    tm = min(512, l)
    slab = pl.BlockSpec((None, tm, 256), lambda bi, i: (bi, i, 0))
    return pl.pallas_call(
        _outproj_kernel,
        grid=(b, l // tm),
        in_specs=[
            pl.BlockSpec((None, tm, d), lambda bi, i: (bi, i, 0)),
            slab, slab, slab, slab,
            pl.BlockSpec(w.shape, lambda bi, i: (0, 0)),
            pl.BlockSpec((None, 6, d), lambda bi, i: (bi, 0, 0)),
        ],
        out_specs=pl.BlockSpec((None, tm, d), lambda bi, i: (bi, i, 0)),
        out_shape=jax.ShapeDtypeStruct((b, l, d), F32),
        compiler_params=_cparams(("parallel", "parallel")),
        name="outproj",
    )(x, a, bb, m, dd, w, mod)


def _top_rows(s, n):
    rows = []
    for r in range(n):
        mx = jnp.max(s, axis=0, keepdims=True)
        rows.append(mx)
        if r + 1 < n:
            s = jnp.where(s == mx, -jnp.inf, s)
    return rows


def _peer_route_kernel(x_ref, g_ref, mod_ref, wq_ref, keys_ref, xnt_ref, e1_ref, th_ref, s2_ref, e2_ref, xn_scr):
    h = pl.program_id(2)

    @pl.when(h == 0)
    def _():
        xn = _modnorm(x_ref[...], g_ref[...], mod_ref, 3, 4)
        xn_scr[...] = xn.astype(BF16)
        xnt_ref[...] = xn.T.astype(BF16)

    q = jnp.dot(xn_scr[...], wq_ref[...], preferred_element_type=F32).astype(BF16)
    half = PEER_DQ // 2
    s1 = _nt(keys_ref[0], q[:, :half])
    s2 = _nt(keys_ref[1], q[:, half:])
    top1 = _top_rows(s1, PEER_TOPK)
    top2 = _top_rows(s2, PEER_TOPK)
    pad_rows = [jnp.full_like(top1[0], -jnp.inf)] * (-len(STAIR) % 8)
    cand = jnp.concatenate([top1[a] + top2[b] for a, b in STAIR] + pad_rows, axis=0)
    thr = _top_rows(cand, PEER_TOPK)[-1]
    m1, m2 = top1[0], top2[0]
    zsum = jnp.sum(jnp.where(cand >= thr, jnp.exp(cand - (m1 + m2)), 0.0), axis=0, keepdims=True)
    e1_ref[...] = jnp.exp(s1 - m1) / zsum
    th_ref[...] = thr - s1
    s2_ref[...] = s2
    e2_ref[...] = jnp.exp(s2 - m2)


def _peer_route(x, g, mod, wq, keys):
    b, l, d = x.shape
    tm = min(256, l)
    nt = l // tm
    n = b * l
    hb = pl.BlockSpec((None, PEER_NKEYS, tm), lambda bi, i, h: (h, 0, bi * nt + i))
    return pl.pallas_call(
        _peer_route_kernel,
        grid=(b, nt, PEER_HEADS),
        in_specs=[
            pl.BlockSpec((None, tm, d), lambda bi, i, h: (bi, i, 0)),
            pl.BlockSpec((1, d), lambda bi, i, h: (0, 0)),
            pl.BlockSpec((None, 6, d), lambda bi, i, h: (bi, 0, 0)),
            pl.BlockSpec((d, PEER_DQ), lambda bi, i, h: (0, h)),
            pl.BlockSpec((None, 2, PEER_NKEYS, PEER_DQ // 2), lambda bi, i, h: (h, 0, 0, 0)),
        ],
        out_specs=[pl.BlockSpec((d, tm), lambda bi, i, h: (0, bi * nt + i)), hb, hb, hb, hb],
        out_shape=[jax.ShapeDtypeStruct((d, n), BF16)] + [jax.ShapeDtypeStruct((PEER_HEADS, PEER_NKEYS, n), F32)] * 4,
        scratch_shapes=[pltpu.VMEM((tm, d), BF16)],
        compiler_params=_cparams(("parallel", "parallel", "arbitrary")),
        name="peer_route",
    )(x, g.reshape(1, d), mod, wq, keys)


def _gelu_tanh(x):
    return 0.5 * x * (1.0 + jnp.tanh(math.sqrt(2.0 / math.pi) * (x + 0.044715 * (x * x * x))))


def _peer_expert_kernel(*refs, rows_per_step, final):
    if final:
        x_ref, mod_ref, xnt_ref, e1_ref, th_ref, s2_ref, e2_ref, u_ref, vt_ref, fg_ref, o_ref, acc_scr, a_scr, wg_scr = refs
    else:
        x_ref, mod_ref, xnt_ref, e1_ref, th_ref, s2_ref, e2_ref, u_ref, vt_ref, o_ref, acc_scr, a_scr, wg_scr = refs
    e = pl.program_id(2)

    @pl.when(e == 0)
    def _():
        acc_scr[...] = jnp.zeros_like(acc_scr)

    a_scr[...] = jnp.dot(u_ref[...], xnt_ref[...], preferred_element_type=F32)

    def per_row(ii, carry):
        off = pl.multiple_of(ii * PEER_NKEYS, PEER_NKEYS)
        w = None
        for h in range(PEER_HEADS):
            e1 = e1_ref[h, pl.ds(ii, 1), :]
            th = th_ref[h, pl.ds(ii, 1), :]
            contrib = jnp.where(s2_ref[h] >= th, e2_ref[h] * e1, 0.0)
            w = contrib if w is None else w + contrib
        act = _gelu_tanh(a_scr[pl.ds(off, PEER_NKEYS), :])
        wg_scr[pl.ds(off, PEER_NKEYS), :] = (w * act).astype(BF16)
        return carry

    lax.fori_loop(0, rows_per_step, per_row, 0)
    acc_scr[...] += jnp.dot(vt_ref[...], wg_scr[...], preferred_element_type=F32)

    @pl.when(e == pl.num_programs(2) - 1)
    def _():
        y = x_ref[...] + mod_ref[5:6, :] * acc_scr[...].T
        if final:
            ms = jnp.mean(y * y, axis=-1, keepdims=True)
            y = y * lax.rsqrt(ms + NORM_EPS) * fg_ref[...]
        o_ref[...] = y


def _peer_experts(x, mod, xnt, e1, th, s2, e2, u, vt, final_g):
    b, l, d = x.shape
    tm = min(512, l)
    nt = l // tm
    rows = 8
    eb = rows * PEER_NKEYS
    n_steps = u.shape[0] // eb
    final = final_g is not None
    tokmap = lambda bi, i, e: (0, 0, bi * nt + i)
    in_specs = [
        pl.BlockSpec((None, tm, d), lambda bi, i, e: (bi, i, 0)),
        pl.BlockSpec((None, 6, d), lambda bi, i, e: (bi, 0, 0)),
        pl.BlockSpec((d, tm), lambda bi, i, e: (0, bi * nt + i)),
        pl.BlockSpec((PEER_HEADS, rows, tm), lambda bi, i, e: (0, e, bi * nt + i)),
        pl.BlockSpec((PEER_HEADS, rows, tm), lambda bi, i, e: (0, e, bi * nt + i)),
        pl.BlockSpec((PEER_HEADS, PEER_NKEYS, tm), tokmap),
        pl.BlockSpec((PEER_HEADS, PEER_NKEYS, tm), tokmap),
        pl.BlockSpec((eb, d), lambda bi, i, e: (e, 0)),
        pl.BlockSpec((d, eb), lambda bi, i, e: (0, e)),
    ]
    args = [x, mod, xnt, e1, th, s2, e2, u, vt]
    if final:
        in_specs.append(pl.BlockSpec((1, d), lambda bi, i, e: (0, 0)))
        args.append(final_g.reshape(1, d))
    return pl.pallas_call(
        functools.partial(_peer_expert_kernel, rows_per_step=rows, final=final),
        grid=(b, nt, n_steps),
        in_specs=in_specs,
        out_specs=pl.BlockSpec((None, tm, d), lambda bi, i, e: (bi, i, 0)),
        out_shape=jax.ShapeDtypeStruct((b, l, d), F32),
        scratch_shapes=[pltpu.VMEM((d, tm), F32), pltpu.VMEM((eb, tm), F32), pltpu.VMEM((eb, tm), BF16)],
        compiler_params=_cparams(("parallel", "parallel", "arbitrary")),
        name="peer_experts",
    )(*args)


def _pad_cols(a, width):
    return jnp.pad(a, ((0, 0), (0, width - a.shape[1])))


def _layout_w_in(w):
    d = w.shape[0]
    z = lambda n: jnp.zeros((d, n), w.dtype)
    gla, hg, mla, dif = 0, 800, 2080, 2432
    parts = [w[:, gla:gla + 768], w[:, hg:hg + 768],
             w[:, mla:mla + 192], z(64), w[:, mla + 192:mla + 320], z(64), w[:, mla + 320:mla + 352], z(32)]
    for base in (dif, dif + 256):
        for h in range(DIFF_HEADS):
            parts += [w[:, base + 64 * h:base + 64 * (h + 1)], z(64)]
    parts += [w[:, dif + 512:dif + 768]]
    parts += [w[:, gla + 768:gla + 800], z(224), w[:, hg + 768:hg + 1280]]
    out = jnp.concatenate(parts, axis=1)
    assert out.shape[1] == MAIN_COLS + GATE_COLS
    return out.astype(BF16)


def _layout_mla(w_uq, w_ukv):
    dq = MLA_NOPE + MLA_ROPE
    wq = jnp.concatenate([_pad_cols(w_uq[:, dq * h:dq * (h + 1)], LANES) for h in range(MLA_HEADS)], axis=1)
    wq = jnp.pad(wq, ((0, 256 - MLA_Q_RANK), (0, 0)))
    per = MLA_NOPE + MLA_DV
    wk = jnp.concatenate([_pad_cols(w_ukv[:, per * h:per * h + MLA_NOPE], LANES) for h in range(MLA_HEADS)], axis=1)
    wv = jnp.concatenate([w_ukv[:, per * h + MLA_NOPE:per * (h + 1)] for h in range(MLA_HEADS)], axis=1)
    return wq.astype(BF16), jnp.concatenate([wk, wv], axis=1).astype(BF16)


def kernel(x, c, ctx, c_ctx, ada_w, ada_b, norm_mix_g, norm_ffn_g, w_in, w_out, gla_gate_w, gla_gate_b, gla_norm_g, hgrn_lb_raw, hgrn_norm_g, mla_q_norm_g, mla_kv_norm_g, mla_w_uq, mla_w_ukv, diff_lambda, diff_norm_g, peer_wq, peer_keys, peer_u, peer_v, final_norm_g):
    b, l, d = x.shape
    lc = ctx.shape[1]
    depth = ada_w.shape[0]

    rows = -(-(b + 1) // 8) * 8
    cpad = jnp.zeros((rows, d), F32).at[:b].set(c).at[b].set(c_ctx)
    mod_all = _adaln(cpad, ada_w, ada_b)

    lb_sm = jax.nn.softmax(hgrn_lb_raw.astype(F32), axis=0)
    lb_all = jnp.cumsum(lb_sm, axis=0) - lb_sm[0]

    lat_mla_tab = _rope_tables(l, 64, 32, True)
    ctx_mla_tab = _rope_tables(lc, 64, 32, False)
    lat_dif_tab = _rope_tables(l, 0, 64, True)
    ctx_dif_tab = _rope_tables(lc, 0, 64, False)

    xc = ctx
    for li in range(depth):
        need_ctx = li < depth - 1
        lam_init = 0.8 - 0.6 * math.exp(-0.3 * li)
        mod_l = mod_all[li, :b].reshape(b, 6, d)
        mod_c = jnp.broadcast_to(mod_all[li, b].reshape(1, 6, d), (b, 6, d))

        w_perm = _layout_w_in(w_in[li])
        w_o = w_out[li].astype(BF16)
        wg = jnp.zeros((2, 256, 128), F32)
        wg = wg.at[0, 0:GLA_GATE_RANK].set(gla_gate_w[li, 0]).at[1, GLA_GATE_RANK:2 * GLA_GATE_RANK].set(gla_gate_w[li, 1])
        wg = wg.astype(BF16)
        bg = gla_gate_b[li].reshape(2, 1, 128)
        gla_gn = jnp.tile(gla_norm_g[li], GLA_HEADS).reshape(1, 256)
        hg_gn = jnp.tile(hgrn_norm_g[li], HG_HEADS).reshape(1, 256)
        lb = lb_all[li]
        hg_p = jnp.zeros((8, 256), F32).at[0].set(jnp.log(jnp.maximum(lb, LB_FLOOR))).at[1].set(jnp.log1p(-lb)).at[2].set(1.0 - lb)
        hg_dummy = jnp.zeros((8, 128), F32)
        gq = jnp.pad(mla_q_norm_g[li], (0, 256 - MLA_Q_RANK)).reshape(1, 256)
        gkv = mla_kv_norm_g[li].reshape(1, MLA_KV_RANK)
        wq_mla, wkv_mla = _layout_mla(mla_w_uq[li], mla_w_ukv[li])
        lq1, lk1, lq2, lk2 = diff_lambda[li, 0], diff_lambda[li, 1], diff_lambda[li, 2], diff_lambda[li, 3]
        lam = (jnp.exp(jnp.sum(lq1 * lk1).astype(F32)) - jnp.exp(jnp.sum(lq2 * lk2).astype(F32)) + lam_init)
        lam_row = jnp.full((1, LANES), lam, F32)
        dif_gn = jnp.tile(diff_norm_g[li], 2).reshape(1, LANES)
        wq_peer = peer_wq[li].astype(BF16)
        keys = peer_keys[li].astype(BF16)
        u_bf = peer_u[li].astype(BF16)
        vt_bf = peer_v[li].astype(BF16).T

        main_c, gate_c = _inproj(xc, norm_mix_g[li], mod_c, w_perm)
        main_l, gate_l = _inproj(x, norm_mix_g[li], mod_l, w_perm)

        mixes_c, mixes_l = [], []
        for mixer, pa_f, pb_f, pa_b, pb_b, gn, dk in (
            ("gla", wg[0], bg[0], wg[1], bg[1], gla_gn, 128),
            ("hg", hg_p, hg_dummy, hg_p, hg_dummy, hg_gn, 256),
        ):
            zero = jnp.zeros((b, 256, dk), F32)
            o_cf, s_f = _scan(main_c, gate_c, pa_f, pb_f, gn, zero, None, mixer=mixer, reverse=False, chunk=64)
            mix_c, s_b = _scan(main_c, gate_c, pa_b, pb_b, gn, zero, o_cf, mixer=mixer, reverse=True, chunk=64)
            o_lf, _ = _scan(main_l, gate_l, pa_f, pb_f, gn, s_f, None, mixer=mixer, reverse=False, chunk=64)
            mix_l, _ = _scan(main_l, gate_l, pa_b, pb_b, gn, s_b, o_lf, mixer=mixer, reverse=True, chunk=64)
            mixes_c.append(mix_c)
            mixes_l.append(mix_l)

        qm_c, km_c, vm_c = _mla_prep(main_c, gq, gkv, wq_mla, wkv_mla, ctx_mla_tab)
        qm_l, km_l, vm_l = _mla_prep(main_l, gq, gkv, wq_mla, wkv_mla, lat_mla_tab)
        mla_l = _attention(qm_l, km_c, vm_c, 0, km_l, vm_l, 0, None, n_maps=1, lam_init=lam_init)
        qd_c, kd_c = _diff_prep(main_c, ctx_dif_tab)
        qd_l, kd_l = _diff_prep(main_l, lat_dif_tab)
        vblk = COL_DV // LANES
        dif_l = _attention(qd_l, kd_c, main_c, vblk, kd_l, main_l, vblk, (lam_row, dif_gn), n_maps=2, lam_init=lam_init)

        x = _outproj(x, mixes_l[0], mixes_l[1], mla_l, dif_l, w_o, mod_l)
        rt = _peer_route(x, norm_ffn_g[li], mod_l, wq_peer, keys)
        x = _peer_experts(x, mod_l, *rt, u_bf, vt_bf, None if need_ctx else final_norm_g)

        if need_ctx:
            mla_c = _attention(qm_c, km_c, vm_c, 0, None, None, 0, None, n_maps=1, lam_init=lam_init)
            dif_c = _attention(qd_c, kd_c, main_c, vblk, None, None, 0, (lam_row, dif_gn), n_maps=2, lam_init=lam_init)
            xc = _outproj(xc, mixes_c[0], mixes_c[1], mla_c, dif_c, w_o, mod_c)
            rtc = _peer_route(xc, norm_ffn_g[li], mod_c, wq_peer, keys)
            xc = _peer_experts(xc, mod_c, *rtc, u_bf, vt_bf, None)
    return x
```

```python
import functools
import math

import jax
import jax.numpy as jnp
import numpy as np
from jax import lax
from jax.experimental import pallas as pl
from jax.experimental.pallas import tpu as pltpu

F32 = jnp.float32
BF16 = jnp.bfloat16

NORM_EPS = 1e-6
ROPE_BASE = 10000.0
GRID_W = 64
LB_FLOOR = 1e-30

GLA_HEADS, GLA_DK, GLA_DV, GLA_GATE_RANK, GLA_GATE_NORM = 4, 32, 64, 16, 16.0
HG_HEADS, HG_DIM = 4, 64
MLA_HEADS, MLA_Q_RANK, MLA_KV_RANK, MLA_NOPE, MLA_ROPE, MLA_DV = 4, 192, 128, 64, 32, 64
DIFF_HEADS, DIFF_DQK, DIFF_DV = 4, 32, 64
PEER_HEADS, PEER_NKEYS, PEER_DQ, PEER_TOPK = 8, 128, 256, 16

LANES = 128
VMEM_LIMIT = 48 * 1024 * 1024
EXP_CAP = 80.0
NEG_BIG = -1e30
ATT_TK = 512
LOG2E = 1.4426950408889634
ATT_R = 512
VT_ROWS = 80

COL_GLA, COL_HG, COL_MLA, COL_DQ, COL_DK, COL_DV, MAIN_COLS = 0, 768, 1536, 2048, 2560, 3072, 3584
GATE_COLS = 768

STAIR = [(a, b) for a in range(PEER_TOPK) for b in range(PEER_TOPK) if (a + 1) * (b + 1) <= PEER_TOPK]


def _cparams(sem):
    return pltpu.CompilerParams(dimension_semantics=sem, vmem_limit_bytes=VMEM_LIMIT)


def _sigmoid(x):
    return 1.0 / (1.0 + jnp.exp(-x))


def _logsig(x):
    return jnp.minimum(x, 0.0) - jnp.log(1.0 + jnp.exp(-jnp.abs(x)))


def _nt(a, b):
    return lax.dot_general(a, b, (((1,), (1,)), ((), ())), preferred_element_type=F32)


def _tn(a, b):
    return lax.dot_general(a, b, (((0,), (0,)), ((), ())), preferred_element_type=F32)


def _adaln_kernel(c_ref, w_ref, b_ref, o_ref):
    c = c_ref[...]
    s = c * _sigmoid(c)
    o_ref[...] = jnp.dot(s.astype(BF16), w_ref[...].astype(BF16), preferred_element_type=F32) + b_ref[...]


def _adaln(cpad, ada_w, ada_b):
    depth, d, n6 = ada_w.shape
    rows = cpad.shape[0]
    tn = 512
    return pl.pallas_call(
        _adaln_kernel,
        grid=(depth, n6 // tn),
        in_specs=[
            pl.BlockSpec((rows, d), lambda l, j: (0, 0)),
            pl.BlockSpec((None, d, tn), lambda l, j: (l, 0, j)),
            pl.BlockSpec((None, 1, tn), lambda l, j: (l, 0, j)),
        ],
        out_specs=pl.BlockSpec((None, rows, tn), lambda l, j: (l, 0, j)),
        out_shape=jax.ShapeDtypeStruct((depth, rows, n6), F32),
        compiler_params=_cparams(("parallel", "parallel")),
        name="adaln",
    )(cpad, ada_w, ada_b.reshape(depth, 1, n6))


def _modnorm(x, g, mod_ref, shift_row, scale_row):
    ms = jnp.mean(x * x, axis=-1, keepdims=True)
    h = x * lax.rsqrt(ms + NORM_EPS) * g
    return h * (1.0 + mod_ref[scale_row:scale_row + 1, :]) + mod_ref[shift_row:shift_row + 1, :]


def _inproj_kernel(x_ref, g_ref, mod_ref, w_ref, main_ref, gate_ref):
    h = _modnorm(x_ref[...], g_ref[...], mod_ref, 0, 1)
    y = jnp.dot(h.astype(BF16), w_ref[...], preferred_element_type=F32)
    main_ref[...] = y[:, :MAIN_COLS].astype(BF16)
    gate_ref[...] = y[:, MAIN_COLS:]


def _inproj(x, g, mod, w_perm):
    b, l, d = x.shape
    tm = min(256, l)
    nc = w_perm.shape[1]
    return pl.pallas_call(
        _inproj_kernel,
        grid=(b, l // tm),
        in_specs=[
            pl.BlockSpec((None, tm, d), lambda bi, i: (bi, i, 0)),
            pl.BlockSpec((1, d), lambda bi, i: (0, 0)),
            pl.BlockSpec((None, 6, d), lambda bi, i: (bi, 0, 0)),
            pl.BlockSpec((d, nc), lambda bi, i: (0, 0)),
        ],
        out_specs=[
            pl.BlockSpec((None, tm, MAIN_COLS), lambda bi, i: (bi, i, 0)),
            pl.BlockSpec((None, tm, GATE_COLS), lambda bi, i: (bi, i, 0)),
        ],
        out_shape=[
            jax.ShapeDtypeStruct((b, l, MAIN_COLS), BF16),
            jax.ShapeDtypeStruct((b, l, GATE_COLS), F32),
        ],
        compiler_params=_cparams(("parallel", "parallel")),
        name="inproj",
    )(x, g.reshape(1, d), mod, w_perm)


def _scan_kernel(*refs, mixer, reverse, finish, chunk):
    if finish:
        main_ref, gate_ref, pa_ref, pb_ref, gn_ref, s0_ref, oprev_ref, o_ref, sfin_ref, st_scr = refs
    else:
        main_ref, gate_ref, pa_ref, pb_ref, gn_ref, s0_ref, o_ref, sfin_ref, st_scr = refs
        oprev_ref = None
    i = pl.program_id(1)

    @pl.when(i == 0)
    def _():
        st_scr[...] = s0_ref[...]

    main = main_ref[...]
    t = main.shape[0]
    c = chunk
    nc = t // c
    heads = 4
    dv = 256
    v = main[:, 256:512]
    gcol = main[:, 512:768].astype(F32)
    if mixer == "gla":
        dk = 128
        q = main[:, 0:128].astype(F32) * (GLA_DK ** -0.5)
        k = main[:, 128:256].astype(F32)
        z = jnp.dot(gate_ref[...].astype(BF16), pa_ref[...], preferred_element_type=F32) + pb_ref[...]
        g = _logsig(z) * (1.0 / GLA_GATE_NORM)
    else:
        dk = 256
        qq = main[:, 0:256].astype(F32)
        q = qq * _sigmoid(qq)
        z = gate_ref[...]
        la = pa_ref[0:1, :]
        lb = pa_ref[1:2, :] + _logsig(z)
        g = jnp.maximum(la, lb) + jnp.log(1.0 + jnp.exp(-jnp.abs(la - lb)))
        k = pa_ref[2:3, :] * _sigmoid(-z)
    hk = dk // heads
    hv = dv // heads

    rt = lax.broadcasted_iota(jnp.int32, (t, t), 0)
    ct = lax.broadcasted_iota(jnp.int32, (t, t), 1)
    same = (rt // c) == (ct // c)
    tri = jnp.where(same & ((ct >= rt) if reverse else (ct <= rt)), 1.0, 0.0).astype(BF16)
    g1 = g.astype(BF16)
    r1 = g - g1.astype(F32)
    g2 = r1.astype(BF16)
    g3 = (r1 - g2.astype(F32)).astype(BF16)
    bcum = (jnp.dot(tri, g1, preferred_element_type=F32) + jnp.dot(tri, g2, preferred_element_type=F32)
            + jnp.dot(tri, g3, preferred_element_type=F32))

    b3 = bcum.reshape(nc, c, dk)
    q3 = q.reshape(nc, c, dk)
    k3 = k.reshape(nc, c, dk)
    if reverse:
        bmid = b3[:, c // 2:c // 2 + 1, :]
        bend = b3[:, 0:1, :]
    else:
        bmid = b3[:, c // 2 - 1:c // 2, :]
        bend = b3[:, c - 1:c, :]
    eb = b3 - bmid
    qa = (q3 * jnp.exp(jnp.minimum(eb, EXP_CAP))).reshape(t, dk).astype(BF16)
    ka = (k3 * jnp.exp(jnp.minimum(-eb, EXP_CAP))).reshape(t, dk).astype(BF16)
    qi = (q3 * jnp.exp(b3)).reshape(t, dk).astype(BF16)
    kb = (k3 * jnp.exp(bend - b3)).reshape(t, dk).astype(BF16)
    dec = jnp.exp(bend)

    lane_k = lax.broadcasted_iota(jnp.int32, (1, dk), 1) // hk
    lane_v = lax.broadcasted_iota(jnp.int32, (1, dv), 1) // hv
    rr = lax.broadcasted_iota(jnp.int32, (c, heads * c), 0)
    cc = lax.broadcasted_iota(jnp.int32, (c, heads * c), 1) % c
    causal = (cc >= rr) if reverse else (cc <= rr)
    bd = (lax.broadcasted_iota(jnp.int32, (dv, dk), 0) // hv) == (lax.broadcasted_iota(jnp.int32, (dv, dk), 1) // hk)

    o_intra = []
    upd = []
    for ci in range(nc):
        sl = slice(ci * c, (ci + 1) * c)
        ka_c = ka[sl]
        v_c = v[sl]
        kst = jnp.concatenate([jnp.where(lane_k == h, ka_c, 0) for h in range(heads)], axis=0)
        vst = jnp.concatenate([jnp.where(lane_v == h, v_c, 0) for h in range(heads)], axis=0)
        sw = _nt(qa[sl], kst)
        p = jnp.where(causal, sw, 0.0).astype(BF16)
        o_intra.append(jnp.dot(p, vst, preferred_element_type=F32))
        upd.append(jnp.where(bd, _tn(v_c, kb[sl]), 0.0))

    outs = [None] * nc
    order = range(nc - 1, -1, -1) if reverse else range(nc)
    for ci in order:
        sl = slice(ci * c, (ci + 1) * c)
        st = st_scr[...]
        outs[ci] = o_intra[ci] + _nt(qi[sl], st.astype(BF16))
        st_scr[...] = st * dec[ci] + upd[ci]
    o = jnp.concatenate(outs, axis=0)

    @pl.when(i == pl.num_programs(1) - 1)
    def _():
        sfin_ref[...] = st_scr[...]

    if not finish:
        o_ref[...] = o
    else:
        ot = o + oprev_ref[...]
        sq = ot * ot
        jv = ((lax.broadcasted_iota(jnp.int32, (dv, dv), 0) // hv)
              == (lax.broadcasted_iota(jnp.int32, (dv, dv), 1) // hv))
        jm = jnp.where(jv, 1.0, 0.0).astype(BF16)
        s1 = sq.astype(BF16)
        s2 = (sq - s1.astype(F32)).astype(BF16)
        ms = (jnp.dot(s1, jm, preferred_element_type=F32) + jnp.dot(s2, jm, preferred_element_type=F32)) * (1.0 / hv)
        y = ot * lax.rsqrt(ms + NORM_EPS) * gn_ref[...]
        o_ref[...] = (y * (gcol * _sigmoid(gcol))).astype(BF16)


def _scan(main, gate, pa, pb, gn, s0, oprev, *, mixer, reverse, chunk):
    b, l, _ = main.shape
    t = min(256, l)
    nb = l // t
    finish = oprev is not None
    dk = 128 if mixer == "gla" else 256
    dv = 256
    main_blk = 0 if mixer == "gla" else 1
    gate_blk = 0 if mixer == "gla" else (2 if reverse else 1)

    def tok(bi, i):
        return (nb - 1 - i) if reverse else i

    in_specs = [
        pl.BlockSpec((None, t, 768), lambda bi, i: (bi, tok(bi, i), main_blk)),
        pl.BlockSpec((None, t, 256), lambda bi, i: (bi, tok(bi, i), gate_blk)),
        pl.BlockSpec(pa.shape, lambda bi, i: (0, 0)),
        pl.BlockSpec(pb.shape, lambda bi, i: (0, 0)),
        pl.BlockSpec((1, dv), lambda bi, i: (0, 0)),
        pl.BlockSpec((None, dv, dk), lambda bi, i: (bi, 0, 0)),
    ]
    args = [main, gate, pa, pb, gn, s0]
    if finish:
        in_specs.append(pl.BlockSpec((None, t, dv), lambda bi, i: (bi, tok(bi, i), 0)))
        args.append(oprev)
    return pl.pallas_call(
        functools.partial(_scan_kernel, mixer=mixer, reverse=reverse, finish=finish, chunk=chunk),
        grid=(b, nb),
        in_specs=in_specs,
        out_specs=[
            pl.BlockSpec((None, t, dv), lambda bi, i: (bi, tok(bi, i), 0)),
            pl.BlockSpec((None, dv, dk), lambda bi, i: (bi, 0, 0)),
        ],
        out_shape=[
            jax.ShapeDtypeStruct((b, l, dv), BF16 if finish else F32),
            jax.ShapeDtypeStruct((b, dv, dk), F32),
        ],
        scratch_shapes=[pltpu.VMEM((dv, dk), F32)],
        compiler_params=_cparams(("parallel", "arbitrary")),
        name=f"scan_{mixer}_{'bwd' if reverse else 'fwd'}",
    )(*args)


def _rope(x, cos, sin_lo, sin_hi):
    return x * cos + pltpu.roll(x, LANES - 8, 1) * sin_lo + pltpu.roll(x, 8, 1) * sin_hi


def _rope_tables(l, first_lane, n_lanes, rotate):
    lane = np.arange(LANES)
    d = (lane - first_lane) % 32
    active = (lane >= first_lane) & (lane < first_lane + n_lanes)
    freqs = ROPE_BASE ** (-(d % 8).astype(np.float32) / 8.0)
    tt = jnp.arange(l, dtype=jnp.int32)
    rows = (tt // GRID_W).astype(F32)[:, None]
    cols = (tt % GRID_W).astype(F32)[:, None]
    pos = jnp.where(jnp.asarray(d < 16)[None, :], rows, cols)
    ang = pos * jnp.asarray(freqs, F32)[None, :]
    act = jnp.asarray(active)[None, :] & rotate
    lo = jnp.asarray((d % 16) < 8)[None, :]
    cos = jnp.where(act, jnp.cos(ang), 1.0)
    sin = jnp.where(act, jnp.sin(ang), 0.0)
    return cos, jnp.where(lo, -sin, 0.0), jnp.where(lo, 0.0, sin)


def _vt_block(vgrp):
    lane = lax.broadcasted_iota(jnp.int32, (1, LANES), 1)
    return jnp.where(lane == 64, 1.0, vgrp).T[:VT_ROWS].astype(BF16)


def _mla_prep_kernel(slab_ref, gq_ref, gkv_ref, wq_ref, wkv_ref, cos_ref, slo_ref, shi_ref, qt_ref, k_ref, vt_ref):
    slab = slab_ref[...]
    cos, slo, shi = cos_ref[...], slo_ref[...], shi_ref[...]
    cq = slab[:, 0:256].astype(F32)
    qn = cq * lax.rsqrt(jnp.sum(cq * cq, axis=-1, keepdims=True) * (1.0 / MLA_Q_RANK) + NORM_EPS) * gq_ref[...]
    qall = jnp.dot(qn.astype(BF16), wq_ref[...], preferred_element_type=F32)
    ckv = slab[:, 256:384].astype(F32)
    kvn = ckv * lax.rsqrt(jnp.mean(ckv * ckv, axis=-1, keepdims=True) + NORM_EPS) * gkv_ref[...]
    kvall = jnp.dot(kvn.astype(BF16), wkv_ref[...], preferred_element_type=F32)
    kr = _rope(slab[:, 384:512].astype(F32), cos, slo, shi)
    scale = (MLA_NOPE + MLA_ROPE) ** -0.5 * LOG2E
    for h in range(MLA_HEADS):
        qh = _rope(qall[:, h * LANES:(h + 1) * LANES], cos, slo, shi)
        qt_ref[h] = (qh * scale).T.astype(BF16)
        k_ref[h] = (kvall[:, h * LANES:(h + 1) * LANES] + kr).astype(BF16)
        vt_ref[h] = _vt_block(kvall[:, (MLA_HEADS + h) * LANES:(MLA_HEADS + h + 1) * LANES])


def _attn_prep_specs(b, l, tm, heads):
    qt = pl.BlockSpec((None, heads, LANES, tm), lambda bi, i: (bi, 0, 0, i))
    kk = pl.BlockSpec((None, heads, tm, LANES), lambda bi, i: (bi, 0, i, 0))
    vt = pl.BlockSpec((None, heads, None, VT_ROWS, tm), lambda bi, i: (bi, 0, i, 0, 0))
    shapes = [
        jax.ShapeDtypeStruct((b, heads, LANES, l), BF16),
        jax.ShapeDtypeStruct((b, heads, l, LANES), BF16),
        jax.ShapeDtypeStruct((b, heads, l // tm, VT_ROWS, tm), BF16),
    ]
    return [qt, kk, vt], shapes


def _mla_prep(main, gq, gkv, wq, wkv, tables):
    b, l, _ = main.shape
    tm = min(ATT_TK, l)
    cos, slo, shi = tables
    tab = pl.BlockSpec((tm, LANES), lambda bi, i: (i, 0))
    out_specs, out_shape = _attn_prep_specs(b, l, tm, MLA_HEADS)
    return pl.pallas_call(
        _mla_prep_kernel,
        grid=(b, l // tm),
        in_specs=[
            pl.BlockSpec((None, tm, 512), lambda bi, i: (bi, i, COL_MLA // 512)),
            pl.BlockSpec(gq.shape, lambda bi, i: (0, 0)),
            pl.BlockSpec(gkv.shape, lambda bi, i: (0, 0)),
            pl.BlockSpec(wq.shape, lambda bi, i: (0, 0)),
            pl.BlockSpec(wkv.shape, lambda bi, i: (0, 0)),
            tab, tab, tab,
        ],
        out_specs=out_specs,
        out_shape=out_shape,
        compiler_params=_cparams(("parallel", "parallel")),
        name="mla_prep",
    )(main, gq, gkv, wq, wkv, cos, slo, shi)


def _diff_prep_kernel(q_in, k_in, v_in, cos_ref, slo_ref, shi_ref, qt_ref, k_ref, vt_ref):
    cos, slo, shi = cos_ref[...], slo_ref[...], shi_ref[...]
    qs = q_in[...]
    ks = k_in[...]
    vs = v_in[...]
    scale = DIFF_DQK ** -0.5 * LOG2E
    for h in range(DIFF_HEADS):
        sl = slice(h * LANES, (h + 1) * LANES)
        qt_ref[h] = (_rope(qs[:, sl].astype(F32), cos, slo, shi) * scale).T.astype(BF16)
        k_ref[h] = _rope(ks[:, sl].astype(F32), cos, slo, shi).astype(BF16)
        vt_ref[h] = _vt_block(vs[:, sl].astype(F32))


def _diff_prep(main, tables):
    b, l, _ = main.shape
    tm = min(ATT_TK, l)
    cos, slo, shi = tables
    tab = pl.BlockSpec((tm, LANES), lambda bi, i: (i, 0))
    out_specs, out_shape = _attn_prep_specs(b, l, tm, DIFF_HEADS)
    return pl.pallas_call(
        _diff_prep_kernel,
        grid=(b, l // tm),
        in_specs=[
            pl.BlockSpec((None, tm, 512), lambda bi, i: (bi, i, COL_DQ // 512)),
            pl.BlockSpec((None, tm, 512), lambda bi, i: (bi, i, COL_DK // 512)),
            pl.BlockSpec((None, tm, 512), lambda bi, i: (bi, i, COL_DV // 512)),
            tab, tab, tab,
        ],
        out_specs=out_specs,
        out_shape=out_shape,
        compiler_params=_cparams(("parallel", "parallel")),
        name="diff_prep",
    )(main, main, main, cos, slo, shi)


def _attn_kernel(*refs, n_maps, has_lat, lam_init):
    refs = list(refs)
    qt_ref, kc_ref, vtc_ref = refs[:3]
    pos = 3
    if has_lat:
        kl_ref, vtl_ref = refs[pos:pos + 2]
        pos += 2
    if n_maps == 2:
        lam_ref, gn_ref = refs[pos:pos + 2]
        pos += 2
    o_ref = refs[pos]
    if has_lat:
        m_scr, acc_scr, s_scr, cm_scr, al_scr, p_scr = refs[pos + 1:pos + 7]
    tq = qt_ref.shape[2]
    feat = lax.broadcasted_iota(jnp.int32, (LANES, 1), 0)

    qts = []
    for hh in range(2):
        qt = qt_ref[hh]
        if n_maps == 2:
            zero = jnp.zeros_like(qt)
            qt = jnp.concatenate(
                [jnp.where((feat >= DIFF_DQK * mi) & (feat < DIFF_DQK * (mi + 1)), qt, zero) for mi in range(2)], axis=1)
        qts.append(qt)

    accs = []
    for hh in range(2):
        st = jnp.dot(kc_ref[hh], qts[hh], preferred_element_type=F32)
        m0 = jnp.max(st, axis=0, keepdims=True)
        acc0 = jnp.dot(vtc_ref[hh, 0], jnp.exp2(st - m0).astype(BF16), preferred_element_type=F32)
        accs.append(acc0)
        if has_lat:
            m_scr[hh] = m0
            acc_scr[hh] = acc0

    if has_lat:
        n = vtl_ref.shape[1]
        tk = vtl_ref.shape[3]

        def scores(j, slot):
            off = pl.multiple_of(j * tk, tk)
            for hh in range(2):
                st = jnp.dot(kl_ref[hh, pl.ds(off, tk), :], qts[hh], preferred_element_type=F32)
                s_scr[hh, slot] = st
                cm_scr[hh, slot] = jnp.max(st, axis=0, keepdims=True)

        def numer(slot):
            for hh in range(2):
                m_old = m_scr[hh]
                m_new = jnp.maximum(m_old, cm_scr[hh, slot])
                al_scr[hh, slot] = jnp.exp2(m_old - m_new)
                p_scr[hh, slot] = jnp.exp2(s_scr[hh, slot] - m_new).astype(BF16)
                m_scr[hh] = m_new

        def values(j, slot):
            for hh in range(2):
                acc_scr[hh] = (al_scr[hh, slot] * acc_scr[hh]
                               + jnp.dot(vtl_ref[hh, j], p_scr[hh, slot], preferred_element_type=F32))

        first = n % 2
        if first:
            scores(0, 0)
            numer(0)
            values(0, 0)
        if n > first:
            for hh in range(2):
                p_scr[hh, 1] = jnp.zeros(p_scr.shape[2:], BF16)
                al_scr[hh, 1] = jnp.ones(al_scr.shape[2:], F32)
            scores(first, 0)

            def body(t, carry):
                j = first + 2 * t
                scores(j + 1, 1)
                numer(0)
                values(jnp.maximum(j - 1, first), 1)
                scores(jnp.minimum(j + 2, n - 1), 0)
                numer(1)
                values(j, 0)
                return carry

            lax.fori_loop(0, (n - first) // 2, body, 0)
            values(n - 1, 1)
        accs = [acc_scr[0], acc_scr[1]]

    outs = []
    for hh in range(2):
        acc = accs[hh]
        ot = acc[0:64, :] / acc[64:65, :]
        if n_maps == 2:
            oh = ot[:, :tq] - lam_ref[:, 0:1] * ot[:, tq:]
            ms = jnp.mean(oh * oh, axis=0, keepdims=True)
            ot = oh * lax.rsqrt(ms + NORM_EPS)
        outs.append(ot)
    o = jnp.concatenate(outs, axis=0).T
    if n_maps == 2:
        o = o * gn_ref[...] * (1.0 - lam_init)
    o_ref[...] = o.astype(BF16)


def _attention(qt, kc, vtc, kl, vtl, extra, *, n_maps, lam_init):
    b, nh, _, l = qt.shape
    lc = kc.shape[2]
    tq = min(ATT_R // n_maps, l)
    r = n_maps * tq
    has_lat = kl is not None
    scratch = []
    in_specs = [
        pl.BlockSpec((None, 2, LANES, tq), lambda bi, hp, i: (bi, hp, 0, i)),
        pl.BlockSpec((None, 2, lc, LANES), lambda bi, hp, i: (bi, hp, 0, 0)),
        pl.BlockSpec((None, 2) + vtc.shape[2:], lambda bi, hp, i: (bi, hp, 0, 0, 0)),
    ]
    args = [qt, kc, vtc]
    if has_lat:
        in_specs += [
            pl.BlockSpec((None, 2, kl.shape[2], LANES), lambda bi, hp, i: (bi, hp, 0, 0)),
            pl.BlockSpec((None, 2) + vtl.shape[2:], lambda bi, hp, i: (bi, hp, 0, 0, 0)),
        ]
        args += [kl, vtl]
        tk = vtl.shape[4]
        scratch = [
            pltpu.VMEM((2, 1, r), F32), pltpu.VMEM((2, VT_ROWS, r), F32),
            pltpu.VMEM((2, 2, tk, r), F32), pltpu.VMEM((2, 2, 1, r), F32), pltpu.VMEM((2, 2, 1, r), F32),
            pltpu.VMEM((2, 2, tk, r), BF16),
        ]
    if n_maps == 2:
        in_specs += [pl.BlockSpec((1, LANES), lambda bi, hp, i: (0, 0))] * 2
        args += list(extra)
    return pl.pallas_call(
        functools.partial(_attn_kernel, n_maps=n_maps, has_lat=has_lat, lam_init=lam_init),
        grid=(b, nh // 2, l // tq),
        in_specs=in_specs,
        out_specs=pl.BlockSpec((None, tq, LANES), lambda bi, hp, i: (bi, i, hp)),
        out_shape=jax.ShapeDtypeStruct((b, l, nh * 64), BF16),
        scratch_shapes=scratch,
        compiler_params=_cparams(("parallel", "parallel", "arbitrary")),
        name="attn_mla" if n_maps == 1 else "attn_diff",
    )(*args)


def _outproj_kernel(x_ref, a_ref, b_ref, m_ref, d_ref, w_ref, mod_ref, o_ref):
    mix = jnp.concatenate([a_ref[...], b_ref[...], m_ref[...], d_ref[...]], axis=-1)
    y = jnp.dot(mix, w_ref[...], preferred_element_type=F32)
    o_ref[...] = x_ref[...] + mod_ref[2:3, :] * y


def _outproj(x, a, bb, m, dd, w, mod):
    b, l, d = x.shape
    tm = min(512, l)
    slab = pl.BlockSpec((None, tm, 256), lambda bi, i: (bi, i, 0))
    return pl.pallas_call(
        _outproj_kernel,
        grid=(b, l // tm),
        in_specs=[
            pl.BlockSpec((None, tm, d), lambda bi, i: (bi, i, 0)),
            slab, slab, slab, slab,
            pl.BlockSpec(w.shape, lambda bi, i: (0, 0)),
            pl.BlockSpec((None, 6, d), lambda bi, i: (bi, 0, 0)),
        ],
        out_specs=pl.BlockSpec((None, tm, d), lambda bi, i: (bi, i, 0)),
        out_shape=jax.ShapeDtypeStruct((b, l, d), F32),
        compiler_params=_cparams(("parallel", "parallel")),
        name="outproj",
    )(x, a, bb, m, dd, w, mod)


def _top_rows(s, n):
    rows = []
    for r in range(n):
        mx = jnp.max(s, axis=0, keepdims=True)
        rows.append(mx)
        if r + 1 < n:
            s = jnp.where(s == mx, -jnp.inf, s)
    return rows


def _peer_route_kernel(x_ref, g_ref, mod_ref, wq_ref, keys_ref, xnt_ref, e1_ref, th_ref, s2_ref, e2_ref, xn_scr):
    h = pl.program_id(2)

    @pl.when(h == 0)
    def _():
        xn = _modnorm(x_ref[...], g_ref[...], mod_ref, 3, 4)
        xn_scr[...] = xn.astype(BF16)
        xnt_ref[...] = xn.T.astype(BF16)

    q = jnp.dot(xn_scr[...], wq_ref[...], preferred_element_type=F32).astype(BF16)
    half = PEER_DQ // 2
    s1 = _nt(keys_ref[0], q[:, :half])
    s2 = _nt(keys_ref[1], q[:, half:])
    top1 = _top_rows(s1, PEER_TOPK)
    top2 = _top_rows(s2, PEER_TOPK)
    pad_rows = [jnp.full_like(top1[0], -jnp.inf)] * (-len(STAIR) % 8)
    cand = jnp.concatenate([top1[a] + top2[b] for a, b in STAIR] + pad_rows, axis=0)
    thr = _top_rows(cand, PEER_TOPK)[-1]
    m1, m2 = top1[0], top2[0]
    zsum = jnp.sum(jnp.where(cand >= thr, jnp.exp(cand - (m1 + m2)), 0.0), axis=0, keepdims=True)
    e1_ref[...] = jnp.exp(s1 - m1) / zsum
    th_ref[...] = thr - s1
    s2_ref[...] = s2
    e2_ref[...] = jnp.exp(s2 - m2)


def _peer_route(x, g, mod, wq, keys):
    b, l, d = x.shape
    tm = min(256, l)
    nt = l // tm
    n = b * l
    hb = pl.BlockSpec((None, PEER_NKEYS, tm), lambda bi, i, h: (h, 0, bi * nt + i))
    return pl.pallas_call(
        _peer_route_kernel,
        grid=(b, nt, PEER_HEADS),
        in_specs=[
            pl.BlockSpec((None, tm, d), lambda bi, i, h: (bi, i, 0)),
            pl.BlockSpec((1, d), lambda bi, i, h: (0, 0)),
            pl.BlockSpec((None, 6, d), lambda bi, i, h: (bi, 0, 0)),
            pl.BlockSpec((d, PEER_DQ), lambda bi, i, h: (0, h)),
            pl.BlockSpec((None, 2, PEER_NKEYS, PEER_DQ // 2), lambda bi, i, h: (h, 0, 0, 0)),
        ],
        out_specs=[pl.BlockSpec((d, tm), lambda bi, i, h: (0, bi * nt + i)), hb, hb, hb, hb],
        out_shape=[jax.ShapeDtypeStruct((d, n), BF16)] + [jax.ShapeDtypeStruct((PEER_HEADS, PEER_NKEYS, n), F32)] * 4,
        scratch_shapes=[pltpu.VMEM((tm, d), BF16)],
        compiler_params=_cparams(("parallel", "parallel", "arbitrary")),
        name="peer_route",
    )(x, g.reshape(1, d), mod, wq, keys)


def _gelu_tanh(x):
    return 0.5 * x * (1.0 + jnp.tanh(math.sqrt(2.0 / math.pi) * (x + 0.044715 * (x * x * x))))


def _peer_expert_kernel(*refs, rows_per_step, final):
    if final:
        x_ref, mod_ref, xnt_ref, e1_ref, th_ref, s2_ref, e2_ref, u_ref, vt_ref, fg_ref, o_ref, acc_scr, a_scr, wg_scr = refs
    else:
        x_ref, mod_ref, xnt_ref, e1_ref, th_ref, s2_ref, e2_ref, u_ref, vt_ref, o_ref, acc_scr, a_scr, wg_scr = refs
    e = pl.program_id(2)

    @pl.when(e == 0)
    def _():
        acc_scr[...] = jnp.zeros_like(acc_scr)

    a_scr[...] = jnp.dot(u_ref[...], xnt_ref[...], preferred_element_type=F32)

    def per_row(ii, carry):
        off = pl.multiple_of(ii * PEER_NKEYS, PEER_NKEYS)
        w = None
        for h in range(PEER_HEADS):
            e1 = e1_ref[h, pl.ds(ii, 1), :]
            th = th_ref[h, pl.ds(ii, 1), :]
            contrib = jnp.where(s2_ref[h] >= th, e2_ref[h] * e1, 0.0)
            w = contrib if w is None else w + contrib
        act = _gelu_tanh(a_scr[pl.ds(off, PEER_NKEYS), :])
        wg_scr[pl.ds(off, PEER_NKEYS), :] = (w * act).astype(BF16)
        return carry

    lax.fori_loop(0, rows_per_step, per_row, 0)
    acc_scr[...] += jnp.dot(vt_ref[...], wg_scr[...], preferred_element_type=F32)

    @pl.when(e == pl.num_programs(2) - 1)
    def _():
        y = x_ref[...] + mod_ref[5:6, :] * acc_scr[...].T
        if final:
            ms = jnp.mean(y * y, axis=-1, keepdims=True)
            y = y * lax.rsqrt(ms + NORM_EPS) * fg_ref[...]
        o_ref[...] = y


def _peer_experts(x, mod, xnt, e1, th, s2, e2, u, vt, final_g):
    b, l, d = x.shape
    tm = min(512, l)
    nt = l // tm
    rows = 8
    eb = rows * PEER_NKEYS
    n_steps = u.shape[0] // eb
    final = final_g is not None
    tokmap = lambda bi, i, e: (0, 0, bi * nt + i)
    in_specs = [
        pl.BlockSpec((None, tm, d), lambda bi, i, e: (bi, i, 0)),
        pl.BlockSpec((None, 6, d), lambda bi, i, e: (bi, 0, 0)),
        pl.BlockSpec((d, tm), lambda bi, i, e: (0, bi * nt + i)),
        pl.BlockSpec((PEER_HEADS, rows, tm), lambda bi, i, e: (0, e, bi * nt + i)),
        pl.BlockSpec((PEER_HEADS, rows, tm), lambda bi, i, e: (0, e, bi * nt + i)),
        pl.BlockSpec((PEER_HEADS, PEER_NKEYS, tm), tokmap),
        pl.BlockSpec((PEER_HEADS, PEER_NKEYS, tm), tokmap),
        pl.BlockSpec((eb, d), lambda bi, i, e: (e, 0)),
        pl.BlockSpec((d, eb), lambda bi, i, e: (0, e)),
    ]
    args = [x, mod, xnt, e1, th, s2, e2, u, vt]
    if final:
        in_specs.append(pl.BlockSpec((1, d), lambda bi, i, e: (0, 0)))
        args.append(final_g.reshape(1, d))
    return pl.pallas_call(
        functools.partial(_peer_expert_kernel, rows_per_step=rows, final=final),
        grid=(b, nt, n_steps),
        in_specs=in_specs,
        out_specs=pl.BlockSpec((None, tm, d), lambda bi, i, e: (bi, i, 0)),
        out_shape=jax.ShapeDtypeStruct((b, l, d), F32),
        scratch_shapes=[pltpu.VMEM((d, tm), F32), pltpu.VMEM((eb, tm), F32), pltpu.VMEM((eb, tm), BF16)],
        compiler_params=_cparams(("parallel", "parallel", "arbitrary")),
        name="peer_experts",
    )(*args)


def _pad_cols(a, width):
    return jnp.pad(a, ((0, 0), (0, width - a.shape[1])))


def _layout_w_in(w):
    d = w.shape[0]
    z = lambda n: jnp.zeros((d, n), w.dtype)
    gla, hg, mla, dif = 0, 800, 2080, 2432
    parts = [w[:, gla:gla + 768], w[:, hg:hg + 768],
             w[:, mla:mla + 192], z(64), w[:, mla + 192:mla + 320], z(64), w[:, mla + 320:mla + 352], z(32)]
    for base in (dif, dif + 256, dif + 512):
        for h in range(DIFF_HEADS):
            parts += [w[:, base + 64 * h:base + 64 * (h + 1)], z(64)]
    parts += [w[:, gla + 768:gla + 800], z(224), w[:, hg + 768:hg + 1280]]
    out = jnp.concatenate(parts, axis=1)
    assert out.shape[1] == MAIN_COLS + GATE_COLS
    return out.astype(BF16)


def _layout_mla(w_uq, w_ukv):
    dq = MLA_NOPE + MLA_ROPE
    wq = jnp.concatenate([_pad_cols(w_uq[:, dq * h:dq * (h + 1)], LANES) for h in range(MLA_HEADS)], axis=1)
    wq = jnp.pad(wq, ((0, 256 - MLA_Q_RANK), (0, 0)))
    per = MLA_NOPE + MLA_DV
    wk = jnp.concatenate([_pad_cols(w_ukv[:, per * h:per * h + MLA_NOPE], LANES) for h in range(MLA_HEADS)], axis=1)
    wv = jnp.concatenate([_pad_cols(w_ukv[:, per * h + MLA_NOPE:per * (h + 1)], LANES) for h in range(MLA_HEADS)], axis=1)
    return wq.astype(BF16), jnp.concatenate([wk, wv], axis=1).astype(BF16)


def kernel(x, c, ctx, c_ctx, ada_w, ada_b, norm_mix_g, norm_ffn_g, w_in, w_out, gla_gate_w, gla_gate_b, gla_norm_g, hgrn_lb_raw, hgrn_norm_g, mla_q_norm_g, mla_kv_norm_g, mla_w_uq, mla_w_ukv, diff_lambda, diff_norm_g, peer_wq, peer_keys, peer_u, peer_v, final_norm_g):
    b, l, d = x.shape
    lc = ctx.shape[1]
    depth = ada_w.shape[0]
    assert lc <= ATT_TK, "context keys are consumed as one attention step"

    rows = -(-(b + 1) // 8) * 8
    cpad = jnp.zeros((rows, d), F32).at[:b].set(c).at[b].set(c_ctx)
    mod_all = _adaln(cpad, ada_w, ada_b)

    lb_sm = jax.nn.softmax(hgrn_lb_raw.astype(F32), axis=0)
    lb_all = jnp.cumsum(lb_sm, axis=0) - lb_sm[0]

    lat_mla_tab = _rope_tables(l, 64, 32, True)
    ctx_mla_tab = _rope_tables(lc, 64, 32, False)
    lat_dif_tab = _rope_tables(l, 0, 64, True)
    ctx_dif_tab = _rope_tables(lc, 0, 64, False)

    xc = ctx
    for li in range(depth):
        need_ctx = li < depth - 1
        lam_init = 0.8 - 0.6 * math.exp(-0.3 * li)
        mod_l = mod_all[li, :b].reshape(b, 6, d)
        mod_c = jnp.broadcast_to(mod_all[li, b].reshape(1, 6, d), (b, 6, d))

        w_perm = _layout_w_in(w_in[li])
        w_o = w_out[li].astype(BF16)
        wg = jnp.zeros((2, 256, 128), F32)
        wg = wg.at[0, 0:GLA_GATE_RANK].set(gla_gate_w[li, 0]).at[1, GLA_GATE_RANK:2 * GLA_GATE_RANK].set(gla_gate_w[li, 1])
        wg = wg.astype(BF16)
        bg = gla_gate_b[li].reshape(2, 1, 128)
        gla_gn = jnp.tile(gla_norm_g[li], GLA_HEADS).reshape(1, 256)
        hg_gn = jnp.tile(hgrn_norm_g[li], HG_HEADS).reshape(1, 256)
        lb = lb_all[li]
        hg_p = jnp.zeros((8, 256), F32).at[0].set(jnp.log(jnp.maximum(lb, LB_FLOOR))).at[1].set(jnp.log1p(-lb)).at[2].set(1.0 - lb)
        hg_dummy = jnp.zeros((8, 128), F32)
        gq = jnp.pad(mla_q_norm_g[li], (0, 256 - MLA_Q_RANK)).reshape(1, 256)
        gkv = mla_kv_norm_g[li].reshape(1, MLA_KV_RANK)
        wq_mla, wkv_mla = _layout_mla(mla_w_uq[li], mla_w_ukv[li])
        lq1, lk1, lq2, lk2 = diff_lambda[li, 0], diff_lambda[li, 1], diff_lambda[li, 2], diff_lambda[li, 3]
        lam = (jnp.exp(jnp.sum(lq1 * lk1).astype(F32)) - jnp.exp(jnp.sum(lq2 * lk2).astype(F32)) + lam_init)
        lam_row = jnp.full((1, LANES), lam, F32)
        dif_gn = jnp.tile(diff_norm_g[li], 2).reshape(1, LANES)
        wq_peer = peer_wq[li].astype(BF16)
        keys = peer_keys[li].astype(BF16)
        u_bf = peer_u[li].astype(BF16)
        vt_bf = peer_v[li].astype(BF16).T

        main_c, gate_c = _inproj(xc, norm_mix_g[li], mod_c, w_perm)
        main_l, gate_l = _inproj(x, norm_mix_g[li], mod_l, w_perm)

        mixes_c, mixes_l = [], []
        for mixer, pa_f, pb_f, pa_b, pb_b, gn, dk in (
            ("gla", wg[0], bg[0], wg[1], bg[1], gla_gn, 128),
            ("hg", hg_p, hg_dummy, hg_p, hg_dummy, hg_gn, 256),
        ):
            zero = jnp.zeros((b, 256, dk), F32)
            o_cf, s_f = _scan(main_c, gate_c, pa_f, pb_f, gn, zero, None, mixer=mixer, reverse=False, chunk=64)
            mix_c, s_b = _scan(main_c, gate_c, pa_b, pb_b, gn, zero, o_cf, mixer=mixer, reverse=True, chunk=64)
            o_lf, _ = _scan(main_l, gate_l, pa_f, pb_f, gn, s_f, None, mixer=mixer, reverse=False, chunk=64)
            mix_l, _ = _scan(main_l, gate_l, pa_b, pb_b, gn, s_b, o_lf, mixer=mixer, reverse=True, chunk=64)
            mixes_c.append(mix_c)
            mixes_l.append(mix_l)

        qm_c, km_c, vm_c = _mla_prep(main_c, gq, gkv, wq_mla, wkv_mla, ctx_mla_tab)
        qm_l, km_l, vm_l = _mla_prep(main_l, gq, gkv, wq_mla, wkv_mla, lat_mla_tab)
        mla_l = _attention(qm_l, km_c, vm_c, km_l, vm_l, None, n_maps=1, lam_init=lam_init)
        qd_c, kd_c, vd_c = _diff_prep(main_c, ctx_dif_tab)
        qd_l, kd_l, vd_l = _diff_prep(main_l, lat_dif_tab)
        dif_l = _attention(qd_l, kd_c, vd_c, kd_l, vd_l, (lam_row, dif_gn), n_maps=2, lam_init=lam_init)

        x = _outproj(x, mixes_l[0], mixes_l[1], mla_l, dif_l, w_o, mod_l)
        rt = _peer_route(x, norm_ffn_g[li], mod_l, wq_peer, keys)
        x = _peer_experts(x, mod_l, *rt, u_bf, vt_bf, None if need_ctx else final_norm_g)

        if need_ctx:
            mla_c = _attention(qm_c, km_c, vm_c, None, None, None, n_maps=1, lam_init=lam_init)
            dif_c = _attention(qd_c, kd_c, vd_c, None, None, (lam_row, dif_gn), n_maps=2, lam_init=lam_init)
            xc = _outproj(xc, mixes_c[0], mixes_c[1], mla_c, dif_c, w_o, mod_c)
            rtc = _peer_route(xc, norm_ffn_g[li], mod_c, wq_peer, keys)
            xc = _peer_experts(xc, mod_c, *rtc, u_bf, vt_bf, None)
    return x
```

```python
import functools
import math

import jax
import jax.numpy as jnp
import numpy as np
from jax import lax
from jax.experimental import pallas as pl
from jax.experimental.pallas import tpu as pltpu

F32 = jnp.float32
BF16 = jnp.bfloat16

NORM_EPS = 1e-6
ROPE_BASE = 10000.0
GRID_W = 64
LB_FLOOR = 1e-30

GLA_HEADS, GLA_DK, GLA_DV, GLA_GATE_RANK, GLA_GATE_NORM = 4, 32, 64, 16, 16.0
HG_HEADS, HG_DIM = 4, 64
MLA_HEADS, MLA_Q_RANK, MLA_KV_RANK, MLA_NOPE, MLA_ROPE, MLA_DV = 4, 192, 128, 64, 32, 64
DIFF_HEADS, DIFF_DQK, DIFF_DV = 4, 32, 64
PEER_HEADS, PEER_NKEYS, PEER_DQ, PEER_TOPK = 8, 128, 256, 16

LANES = 128
VMEM_LIMIT = 48 * 1024 * 1024
EXP_CAP = 80.0
NEG_BIG = -1e30
ATT_TK = 512
LOG2E = 1.4426950408889634
ATT_R = 512
VT_ROWS = 80

COL_GLA, COL_HG, COL_MLA, COL_DQ, COL_DK, COL_DV, MAIN_COLS = 0, 768, 1536, 2048, 2560, 3072, 3584
GATE_COLS = 768

STAIR = [(a, b) for a in range(PEER_TOPK) for b in range(PEER_TOPK) if (a + 1) * (b + 1) <= PEER_TOPK]


def _cparams(sem, vmem=VMEM_LIMIT):
    return pltpu.CompilerParams(dimension_semantics=sem, vmem_limit_bytes=vmem)


def _sigmoid(x):
    return 1.0 / (1.0 + jnp.exp(-x))


def _logsig(x):
    return jnp.minimum(x, 0.0) - jnp.log(1.0 + jnp.exp(-jnp.abs(x)))


def _nt(a, b):
    return lax.dot_general(a, b, (((1,), (1,)), ((), ())), preferred_element_type=F32)


def _tn(a, b):
    return lax.dot_general(a, b, (((0,), (0,)), ((), ())), preferred_element_type=F32)


def _adaln_kernel(c_ref, w_ref, b_ref, o_ref):
    c = c_ref[...]
    s = c * _sigmoid(c)
    o_ref[...] = jnp.dot(s.astype(BF16), w_ref[...].astype(BF16), preferred_element_type=F32) + b_ref[...]


def _adaln(cpad, ada_w, ada_b):
    depth, d, n6 = ada_w.shape
    rows = cpad.shape[0]
    tn = 512
    return pl.pallas_call(
        _adaln_kernel,
        grid=(depth, n6 // tn),
        in_specs=[
            pl.BlockSpec((rows, d), lambda l, j: (0, 0)),
            pl.BlockSpec((None, d, tn), lambda l, j: (l, 0, j)),
            pl.BlockSpec((None, 1, tn), lambda l, j: (l, 0, j)),
        ],
        out_specs=pl.BlockSpec((None, rows, tn), lambda l, j: (l, 0, j)),
        out_shape=jax.ShapeDtypeStruct((depth, rows, n6), F32),
        compiler_params=_cparams(("parallel", "parallel")),
        name="adaln",
    )(cpad, ada_w, ada_b.reshape(depth, 1, n6))


def _modnorm(x, g, mod_ref, shift_row, scale_row):
    ms = jnp.mean(x * x, axis=-1, keepdims=True)
    h = x * lax.rsqrt(ms + NORM_EPS) * g
    return h * (1.0 + mod_ref[scale_row:scale_row + 1, :]) + mod_ref[shift_row:shift_row + 1, :]


def _inproj_kernel(x_ref, g_ref, mod_ref, w_ref, main_ref, gate_ref):
    h = _modnorm(x_ref[...], g_ref[...], mod_ref, 0, 1)
    y = jnp.dot(h.astype(BF16), w_ref[...], preferred_element_type=F32)
    main_ref[...] = y[:, :MAIN_COLS].astype(BF16)
    gate_ref[...] = y[:, MAIN_COLS:]


def _inproj(x, g, mod, w_perm):
    b, l, d = x.shape
    tm = min(256, l)
    nc = w_perm.shape[1]
    return pl.pallas_call(
        _inproj_kernel,
        grid=(b, l // tm),
        in_specs=[
            pl.BlockSpec((None, tm, d), lambda bi, i: (bi, i, 0)),
            pl.BlockSpec((1, d), lambda bi, i: (0, 0)),
            pl.BlockSpec((None, 6, d), lambda bi, i: (bi, 0, 0)),
            pl.BlockSpec((d, nc), lambda bi, i: (0, 0)),
        ],
        out_specs=[
            pl.BlockSpec((None, tm, MAIN_COLS), lambda bi, i: (bi, i, 0)),
            pl.BlockSpec((None, tm, GATE_COLS), lambda bi, i: (bi, i, 0)),
        ],
        out_shape=[
            jax.ShapeDtypeStruct((b, l, MAIN_COLS), BF16),
            jax.ShapeDtypeStruct((b, l, GATE_COLS), F32),
        ],
        compiler_params=_cparams(("parallel", "parallel")),
        name="inproj",
    )(x, g.reshape(1, d), mod, w_perm)


def _scan_kernel(*refs, mixer, reverse, finish, chunk):
    if finish:
        main_ref, gate_ref, pa_ref, pb_ref, gn_ref, s0_ref, oprev_ref, o_ref, sfin_ref, st_scr = refs
    else:
        main_ref, gate_ref, pa_ref, pb_ref, gn_ref, s0_ref, o_ref, sfin_ref, st_scr = refs
        oprev_ref = None
    i = pl.program_id(1)

    @pl.when(i == 0)
    def _():
        st_scr[...] = s0_ref[...]

    main = main_ref[...]
    t = main.shape[0]
    c = chunk
    nc = t // c
    heads = 4
    dv = 256
    v = main[:, 256:512]
    gcol = main[:, 512:768].astype(F32)
    if mixer == "gla":
        dk = 128
        q = main[:, 0:128].astype(F32) * (GLA_DK ** -0.5)
        k = main[:, 128:256].astype(F32)
        z = jnp.dot(gate_ref[...].astype(BF16), pa_ref[...], preferred_element_type=F32) + pb_ref[...]
        g = _logsig(z) * (1.0 / GLA_GATE_NORM)
    else:
        dk = 256
        qq = main[:, 0:256].astype(F32)
        q = qq * _sigmoid(qq)
        z = gate_ref[...]
        la = pa_ref[0:1, :]
        lb = pa_ref[1:2, :] + _logsig(z)
        g = jnp.maximum(la, lb) + jnp.log(1.0 + jnp.exp(-jnp.abs(la - lb)))
        k = pa_ref[2:3, :] * _sigmoid(-z)
    hk = dk // heads
    hv = dv // heads

    rt = lax.broadcasted_iota(jnp.int32, (t, t), 0)
    ct = lax.broadcasted_iota(jnp.int32, (t, t), 1)
    same = (rt // c) == (ct // c)
    tri = jnp.where(same & ((ct >= rt) if reverse else (ct <= rt)), 1.0, 0.0).astype(BF16)
    g1 = g.astype(BF16)
    r1 = g - g1.astype(F32)
    g2 = r1.astype(BF16)
    g3 = (r1 - g2.astype(F32)).astype(BF16)
    bcum = (jnp.dot(tri, g1, preferred_element_type=F32) + jnp.dot(tri, g2, preferred_element_type=F32)
            + jnp.dot(tri, g3, preferred_element_type=F32))

    b3 = bcum.reshape(nc, c, dk)
    q3 = q.reshape(nc, c, dk)
    k3 = k.reshape(nc, c, dk)
    if reverse:
        bmid = b3[:, c // 2:c // 2 + 1, :]
        bend = b3[:, 0:1, :]
    else:
        bmid = b3[:, c // 2 - 1:c // 2, :]
        bend = b3[:, c - 1:c, :]
    eb = b3 - bmid
    qa = (q3 * jnp.exp(jnp.minimum(eb, EXP_CAP))).reshape(t, dk).astype(BF16)
    ka = (k3 * jnp.exp(jnp.minimum(-eb, EXP_CAP))).reshape(t, dk).astype(BF16)
    qi = (q3 * jnp.exp(b3)).reshape(t, dk).astype(BF16)
    kb = (k3 * jnp.exp(bend - b3)).reshape(t, dk).astype(BF16)
    dec = jnp.exp(bend)

    lane_k = lax.broadcasted_iota(jnp.int32, (1, dk), 1) // hk
    lane_v = lax.broadcasted_iota(jnp.int32, (1, dv), 1) // hv
    rr = lax.broadcasted_iota(jnp.int32, (c, heads * c), 0)
    cc = lax.broadcasted_iota(jnp.int32, (c, heads * c), 1) % c
    causal = (cc >= rr) if reverse else (cc <= rr)
    bd = (lax.broadcasted_iota(jnp.int32, (dv, dk), 0) // hv) == (lax.broadcasted_iota(jnp.int32, (dv, dk), 1) // hk)

    o_intra = []
    upd = []
    for ci in range(nc):
        sl = slice(ci * c, (ci + 1) * c)
        ka_c = ka[sl]
        v_c = v[sl]
        kst = jnp.concatenate([jnp.where(lane_k == h, ka_c, 0) for h in range(heads)], axis=0)
        vst = jnp.concatenate([jnp.where(lane_v == h, v_c, 0) for h in range(heads)], axis=0)
        sw = _nt(qa[sl], kst)
        p = jnp.where(causal, sw, 0.0).astype(BF16)
        o_intra.append(jnp.dot(p, vst, preferred_element_type=F32))
        upd.append(jnp.where(bd, _tn(v_c, kb[sl]), 0.0))

    outs = [None] * nc
    order = range(nc - 1, -1, -1) if reverse else range(nc)
    for ci in order:
        sl = slice(ci * c, (ci + 1) * c)
        st = st_scr[...]
        outs[ci] = o_intra[ci] + _nt(qi[sl], st.astype(BF16))
        st_scr[...] = st * dec[ci] + upd[ci]
    o = jnp.concatenate(outs, axis=0)

    @pl.when(i == pl.num_programs(1) - 1)
    def _():
        sfin_ref[...] = st_scr[...]

    if not finish:
        o_ref[...] = o
    else:
        ot = o + oprev_ref[...]
        sq = ot * ot
        jv = ((lax.broadcasted_iota(jnp.int32, (dv, dv), 0) // hv)
              == (lax.broadcasted_iota(jnp.int32, (dv, dv), 1) // hv))
        jm = jnp.where(jv, 1.0, 0.0).astype(BF16)
        s1 = sq.astype(BF16)
        s2 = (sq - s1.astype(F32)).astype(BF16)
        ms = (jnp.dot(s1, jm, preferred_element_type=F32) + jnp.dot(s2, jm, preferred_element_type=F32)) * (1.0 / hv)
        y = ot * lax.rsqrt(ms + NORM_EPS) * gn_ref[...]
        o_ref[...] = (y * (gcol * _sigmoid(gcol))).astype(BF16)


def _scan(main, gate, pa, pb, gn, s0, oprev, *, mixer, reverse, chunk):
    b, l, _ = main.shape
    t = min(256, l)
    nb = l // t
    finish = oprev is not None
    dk = 128 if mixer == "gla" else 256
    dv = 256
    main_blk = 0 if mixer == "gla" else 1
    gate_blk = 0 if mixer == "gla" else (2 if reverse else 1)

    def tok(bi, i):
        return (nb - 1 - i) if reverse else i

    in_specs = [
        pl.BlockSpec((None, t, 768), lambda bi, i: (bi, tok(bi, i), main_blk)),
        pl.BlockSpec((None, t, 256), lambda bi, i: (bi, tok(bi, i), gate_blk)),
        pl.BlockSpec(pa.shape, lambda bi, i: (0, 0)),
        pl.BlockSpec(pb.shape, lambda bi, i: (0, 0)),
        pl.BlockSpec((1, dv), lambda bi, i: (0, 0)),
        pl.BlockSpec((None, dv, dk), lambda bi, i: (bi, 0, 0)),
    ]
    args = [main, gate, pa, pb, gn, s0]
    if finish:
        in_specs.append(pl.BlockSpec((None, t, dv), lambda bi, i: (bi, tok(bi, i), 0)))
        args.append(oprev)
    return pl.pallas_call(
        functools.partial(_scan_kernel, mixer=mixer, reverse=reverse, finish=finish, chunk=chunk),
        grid=(b, nb),
        in_specs=in_specs,
        out_specs=[
            pl.BlockSpec((None, t, dv), lambda bi, i: (bi, tok(bi, i), 0)),
            pl.BlockSpec((None, dv, dk), lambda bi, i: (bi, 0, 0)),
        ],
        out_shape=[
            jax.ShapeDtypeStruct((b, l, dv), BF16 if finish else F32),
            jax.ShapeDtypeStruct((b, dv, dk), F32),
        ],
        scratch_shapes=[pltpu.VMEM((dv, dk), F32)],
        compiler_params=_cparams(("parallel", "arbitrary")),
        name=f"scan_{mixer}_{'bwd' if reverse else 'fwd'}",
    )(*args)


def _rope(x, cos, sin_lo, sin_hi):
    return x * cos + pltpu.roll(x, LANES - 8, 1) * sin_lo + pltpu.roll(x, 8, 1) * sin_hi


def _rope_tables(l, first_lane, n_lanes, rotate):
    lane = np.arange(LANES)
    d = (lane - first_lane) % 32
    active = (lane >= first_lane) & (lane < first_lane + n_lanes)
    freqs = ROPE_BASE ** (-(d % 8).astype(np.float32) / 8.0)
    tt = jnp.arange(l, dtype=jnp.int32)
    rows = (tt // GRID_W).astype(F32)[:, None]
    cols = (tt % GRID_W).astype(F32)[:, None]
    pos = jnp.where(jnp.asarray(d < 16)[None, :], rows, cols)
    ang = pos * jnp.asarray(freqs, F32)[None, :]
    act = jnp.asarray(active)[None, :] & rotate
    lo = jnp.asarray((d % 16) < 8)[None, :]
    cos = jnp.where(act, jnp.cos(ang), 1.0)
    sin = jnp.where(act, jnp.sin(ang), 0.0)
    return cos, jnp.where(lo, -sin, 0.0), jnp.where(lo, 0.0, sin)


def _vt_block(vgrp):
    lane = lax.broadcasted_iota(jnp.int32, (1, LANES), 1)
    return jnp.where(lane == 64, 1.0, vgrp).T[:VT_ROWS].astype(BF16)


def _mla_prep_kernel(slab_ref, gq_ref, gkv_ref, wq_ref, wkv_ref, cos_ref, slo_ref, shi_ref, qt_ref, k_ref, vt_ref):
    slab = slab_ref[...]
    cos, slo, shi = cos_ref[...], slo_ref[...], shi_ref[...]
    cq = slab[:, 0:256].astype(F32)
    qn = cq * lax.rsqrt(jnp.sum(cq * cq, axis=-1, keepdims=True) * (1.0 / MLA_Q_RANK) + NORM_EPS) * gq_ref[...]
    qall = jnp.dot(qn.astype(BF16), wq_ref[...], preferred_element_type=F32)
    ckv = slab[:, 256:384].astype(F32)
    kvn = ckv * lax.rsqrt(jnp.mean(ckv * ckv, axis=-1, keepdims=True) + NORM_EPS) * gkv_ref[...]
    kvall = jnp.dot(kvn.astype(BF16), wkv_ref[...], preferred_element_type=F32)
    kr = _rope(slab[:, 384:512].astype(F32), cos, slo, shi)
    scale = (MLA_NOPE + MLA_ROPE) ** -0.5 * LOG2E
    for h in range(MLA_HEADS):
        qh = _rope(qall[:, h * LANES:(h + 1) * LANES], cos, slo, shi)
        qt_ref[h] = (qh * scale).T.astype(BF16)
        k_ref[h] = (kvall[:, h * LANES:(h + 1) * LANES] + kr).astype(BF16)
        vt_ref[h] = _vt_block(kvall[:, (MLA_HEADS + h) * LANES:(MLA_HEADS + h + 1) * LANES])


def _attn_prep_specs(b, l, tm, heads):
    qt = pl.BlockSpec((None, heads, LANES, tm), lambda bi, i: (bi, 0, 0, i))
    kk = pl.BlockSpec((None, heads, tm, LANES), lambda bi, i: (bi, 0, i, 0))
    vt = pl.BlockSpec((None, heads, None, VT_ROWS, tm), lambda bi, i: (bi, 0, i, 0, 0))
    shapes = [
        jax.ShapeDtypeStruct((b, heads, LANES, l), BF16),
        jax.ShapeDtypeStruct((b, heads, l, LANES), BF16),
        jax.ShapeDtypeStruct((b, heads, l // tm, VT_ROWS, tm), BF16),
    ]
    return [qt, kk, vt], shapes


def _mla_prep(main, gq, gkv, wq, wkv, tables):
    b, l, _ = main.shape
    tm = min(ATT_TK, l)
    cos, slo, shi = tables
    tab = pl.BlockSpec((tm, LANES), lambda bi, i: (i, 0))
    out_specs, out_shape = _attn_prep_specs(b, l, tm, MLA_HEADS)
    return pl.pallas_call(
        _mla_prep_kernel,
        grid=(b, l // tm),
        in_specs=[
            pl.BlockSpec((None, tm, 512), lambda bi, i: (bi, i, COL_MLA // 512)),
            pl.BlockSpec(gq.shape, lambda bi, i: (0, 0)),
            pl.BlockSpec(gkv.shape, lambda bi, i: (0, 0)),
            pl.BlockSpec(wq.shape, lambda bi, i: (0, 0)),
            pl.BlockSpec(wkv.shape, lambda bi, i: (0, 0)),
            tab, tab, tab,
        ],
        out_specs=out_specs,
        out_shape=out_shape,
        compiler_params=_cparams(("parallel", "parallel")),
        name="mla_prep",
    )(main, gq, gkv, wq, wkv, cos, slo, shi)


def _diff_prep_kernel(q_in, k_in, v_in, cos_ref, slo_ref, shi_ref, qt_ref, k_ref, vt_ref):
    cos, slo, shi = cos_ref[...], slo_ref[...], shi_ref[...]
    qs = q_in[...]
    ks = k_in[...]
    vs = v_in[...]
    scale = DIFF_DQK ** -0.5 * LOG2E
    for h in range(DIFF_HEADS):
        sl = slice(h * LANES, (h + 1) * LANES)
        qt_ref[h] = (_rope(qs[:, sl].astype(F32), cos, slo, shi) * scale).T.astype(BF16)
        k_ref[h] = _rope(ks[:, sl].astype(F32), cos, slo, shi).astype(BF16)
        vt_ref[h] = _vt_block(vs[:, sl].astype(F32))


def _diff_prep(main, tables):
    b, l, _ = main.shape
    tm = min(ATT_TK, l)
    cos, slo, shi = tables
    tab = pl.BlockSpec((tm, LANES), lambda bi, i: (i, 0))
    out_specs, out_shape = _attn_prep_specs(b, l, tm, DIFF_HEADS)
    return pl.pallas_call(
        _diff_prep_kernel,
        grid=(b, l // tm),
        in_specs=[
            pl.BlockSpec((None, tm, 512), lambda bi, i: (bi, i, COL_DQ // 512)),
            pl.BlockSpec((None, tm, 512), lambda bi, i: (bi, i, COL_DK // 512)),
            pl.BlockSpec((None, tm, 512), lambda bi, i: (bi, i, COL_DV // 512)),
            tab, tab, tab,
        ],
        out_specs=out_specs,
        out_shape=out_shape,
        compiler_params=_cparams(("parallel", "parallel")),
        name="diff_prep",
    )(main, main, main, cos, slo, shi)


def _attn_kernel(*refs, n_maps, has_lat, lam_init):
    refs = list(refs)
    qt_ref, kc_ref, vtc_ref = refs[:3]
    pos = 3
    if has_lat:
        kl_ref, vtl_ref = refs[pos:pos + 2]
        pos += 2
    if n_maps == 2:
        lam_ref, gn_ref = refs[pos:pos + 2]
        pos += 2
    o_ref = refs[pos]
    if has_lat:
        m_scr, acc_scr, s_scr, cm_scr, al_scr, p_scr = refs[pos + 1:pos + 7]
    tq = qt_ref.shape[2]
    feat = lax.broadcasted_iota(jnp.int32, (LANES, 1), 0)

    qts = []
    for hh in range(2):
        qt = qt_ref[hh]
        if n_maps == 2:
            zero = jnp.zeros_like(qt)
            qt = jnp.concatenate(
                [jnp.where((feat >= DIFF_DQK * mi) & (feat < DIFF_DQK * (mi + 1)), qt, zero) for mi in range(2)], axis=1)
        qts.append(qt)

    accs = []
    for hh in range(2):
        st = jnp.dot(kc_ref[hh], qts[hh], preferred_element_type=F32)
        m0 = jnp.max(st, axis=0, keepdims=True)
        acc0 = jnp.dot(vtc_ref[hh, 0], jnp.exp2(st - m0).astype(BF16), preferred_element_type=F32)
        accs.append(acc0)
        if has_lat:
            m_scr[hh] = m0
            acc_scr[hh] = acc0

    if has_lat:
        n = vtl_ref.shape[1]
        tk = vtl_ref.shape[3]

        def scores(j, slot):
            off = pl.multiple_of(j * tk, tk)
            for hh in range(2):
                st = jnp.dot(kl_ref[hh, pl.ds(off, tk), :], qts[hh], preferred_element_type=F32)
                s_scr[hh, slot] = st
                cm_scr[hh, slot] = jnp.max(st, axis=0, keepdims=True)

        def numer(slot):
            for hh in range(2):
                m_old = m_scr[hh]
                m_new = jnp.maximum(m_old, cm_scr[hh, slot])
                al_scr[hh, slot] = jnp.exp2(m_old - m_new)
                p_scr[hh, slot] = jnp.exp2(s_scr[hh, slot] - m_new).astype(BF16)
                m_scr[hh] = m_new

        def values(j, slot):
            for hh in range(2):
                acc_scr[hh] = (al_scr[hh, slot] * acc_scr[hh]
                               + jnp.dot(vtl_ref[hh, j], p_scr[hh, slot], preferred_element_type=F32))

        first = n % 2
        if first:
            scores(0, 0)
            numer(0)
            values(0, 0)
        if n > first:
            for hh in range(2):
                p_scr[hh, 1] = jnp.zeros(p_scr.shape[2:], BF16)
                al_scr[hh, 1] = jnp.ones(al_scr.shape[2:], F32)
            scores(first, 0)

            def body(t, carry):
                j = first + 2 * t
                scores(j + 1, 1)
                numer(0)
                values(jnp.maximum(j - 1, first), 1)
                scores(jnp.minimum(j + 2, n - 1), 0)
                numer(1)
                values(j, 0)
                return carry

            lax.fori_loop(0, (n - first) // 2, body, 0)
            values(n - 1, 1)
        accs = [acc_scr[0], acc_scr[1]]

    outs = []
    for hh in range(2):
        acc = accs[hh]
        ot = acc[0:64, :] / acc[64:65, :]
        if n_maps == 2:
            oh = ot[:, :tq] - lam_ref[:, 0:1] * ot[:, tq:]
            ms = jnp.mean(oh * oh, axis=0, keepdims=True)
            ot = oh * lax.rsqrt(ms + NORM_EPS)
        outs.append(ot)
    o = jnp.concatenate(outs, axis=0).T
    if n_maps == 2:
        o = o * gn_ref[...] * (1.0 - lam_init)
    o_ref[...] = o.astype(BF16)


def _attention(qt, kc, vtc, kl, vtl, extra, *, n_maps, lam_init):
    b, nh, _, l = qt.shape
    lc = kc.shape[2]
    tq = min(ATT_R // n_maps, l)
    r = n_maps * tq
    has_lat = kl is not None
    scratch = []
    in_specs = [
        pl.BlockSpec((None, 2, LANES, tq), lambda bi, hp, i: (bi, hp, 0, i)),
        pl.BlockSpec((None, 2, lc, LANES), lambda bi, hp, i: (bi, hp, 0, 0)),
        pl.BlockSpec((None, 2) + vtc.shape[2:], lambda bi, hp, i: (bi, hp, 0, 0, 0)),
    ]
    args = [qt, kc, vtc]
    if has_lat:
        in_specs += [
            pl.BlockSpec((None, 2, kl.shape[2], LANES), lambda bi, hp, i: (bi, hp, 0, 0)),
            pl.BlockSpec((None, 2) + vtl.shape[2:], lambda bi, hp, i: (bi, hp, 0, 0, 0)),
        ]
        args += [kl, vtl]
        tk = vtl.shape[4]
        scratch = [
            pltpu.VMEM((2, 1, r), F32), pltpu.VMEM((2, VT_ROWS, r), F32),
            pltpu.VMEM((2, 2, tk, r), F32), pltpu.VMEM((2, 2, 1, r), F32), pltpu.VMEM((2, 2, 1, r), F32),
            pltpu.VMEM((2, 2, tk, r), BF16),
        ]
    if n_maps == 2:
        in_specs += [pl.BlockSpec((1, LANES), lambda bi, hp, i: (0, 0))] * 2
        args += list(extra)
    return pl.pallas_call(
        functools.partial(_attn_kernel, n_maps=n_maps, has_lat=has_lat, lam_init=lam_init),
        grid=(b, nh // 2, l // tq),
        in_specs=in_specs,
        out_specs=pl.BlockSpec((None, tq, LANES), lambda bi, hp, i: (bi, i, hp)),
        out_shape=jax.ShapeDtypeStruct((b, l, nh * 64), BF16),
        scratch_shapes=scratch,
        compiler_params=_cparams(("parallel", "parallel", "arbitrary")),
        name="attn_mla" if n_maps == 1 else "attn_diff",
    )(*args)


def _outproj_kernel(x_ref, a_ref, b_ref, m_ref, d_ref, w_ref, mod_ref, o_ref):
    mix = jnp.concatenate([a_ref[...], b_ref[...], m_ref[...], d_ref[...]], axis=-1)
    y = jnp.dot(mix, w_ref[...], preferred_element_type=F32)
    o_ref[...] = x_ref[...] + mod_ref[2:3, :] * y


def _outproj(x, a, bb, m, dd, w, mod):
    b, l, d = x.shape
    tm = min(512, l)
    slab = pl.BlockSpec((None, tm, 256), lambda bi, i: (bi, i, 0))
    return pl.pallas_call(
        _outproj_kernel,
        grid=(b, l // tm),
        in_specs=[
            pl.BlockSpec((None, tm, d), lambda bi, i: (bi, i, 0)),
            slab, slab, slab, slab,
            pl.BlockSpec(w.shape, lambda bi, i: (0, 0)),
            pl.BlockSpec((None, 6, d), lambda bi, i: (bi, 0, 0)),
        ],
        out_specs=pl.BlockSpec((None, tm, d), lambda bi, i: (bi, i, 0)),
        out_shape=jax.ShapeDtypeStruct((b, l, d), F32),
        compiler_params=_cparams(("parallel", "parallel")),
        name="outproj",
    )(x, a, bb, m, dd, w, mod)


def _top_rows(s, n):
    rows = []
    for r in range(n):
        mx = jnp.max(s, axis=0, keepdims=True)
        rows.append(mx)
        if r + 1 < n:
            s = jnp.where(s == mx, -jnp.inf, s)
    return rows


def _peer_route_kernel(x_ref, g_ref, mod_ref, wq_ref, keys_ref, xnt_ref, e1_ref, th_ref, s2_ref, e2_ref, xn_scr):
    h = pl.program_id(2)

    @pl.when(h == 0)
    def _():
        xn = _modnorm(x_ref[...], g_ref[...], mod_ref, 3, 4)
        xn_scr[...] = xn.astype(BF16)
        xnt_ref[...] = xn.T.astype(BF16)

    q = jnp.dot(xn_scr[...], wq_ref[...], preferred_element_type=F32).astype(BF16)
    half = PEER_DQ // 2
    s1 = _nt(keys_ref[0], q[:, :half])
    s2 = _nt(keys_ref[1], q[:, half:])
    top1 = _top_rows(s1, PEER_TOPK)
    top2 = _top_rows(s2, PEER_TOPK)
    pad_rows = [jnp.full_like(top1[0], -jnp.inf)] * (-len(STAIR) % 8)
    cand = jnp.concatenate([top1[a] + top2[b] for a, b in STAIR] + pad_rows, axis=0)
    thr = _top_rows(cand, PEER_TOPK)[-1]
    m1, m2 = top1[0], top2[0]
    zsum = jnp.sum(jnp.where(cand >= thr, jnp.exp(cand - (m1 + m2)), 0.0), axis=0, keepdims=True)
    e1_ref[...] = jnp.exp(s1 - m1) / zsum
    th_ref[...] = thr - s1
    s2_ref[...] = s2
    e2_ref[...] = jnp.exp(s2 - m2)


def _peer_route(x, g, mod, wq, keys):
    b, l, d = x.shape
    tm = min(256, l)
    nt = l // tm
    n = b * l
    hb = pl.BlockSpec((None, PEER_NKEYS, tm), lambda bi, i, h: (h, 0, bi * nt + i))
    return pl.pallas_call(
        _peer_route_kernel,
        grid=(b, nt, PEER_HEADS),
        in_specs=[
            pl.BlockSpec((None, tm, d), lambda bi, i, h: (bi, i, 0)),
            pl.BlockSpec((1, d), lambda bi, i, h: (0, 0)),
            pl.BlockSpec((None, 6, d), lambda bi, i, h: (bi, 0, 0)),
            pl.BlockSpec((d, PEER_DQ), lambda bi, i, h: (0, h)),
            pl.BlockSpec((None, 2, PEER_NKEYS, PEER_DQ // 2), lambda bi, i, h: (h, 0, 0, 0)),
        ],
        out_specs=[pl.BlockSpec((d, tm), lambda bi, i, h: (0, bi * nt + i)), hb, hb, hb, hb],
        out_shape=[jax.ShapeDtypeStruct((d, n), BF16)] + [jax.ShapeDtypeStruct((PEER_HEADS, PEER_NKEYS, n), F32)] * 4,
        scratch_shapes=[pltpu.VMEM((tm, d), BF16)],
        compiler_params=_cparams(("parallel", "parallel", "arbitrary")),
        name="peer_route",
    )(x, g.reshape(1, d), mod, wq, keys)


GELU_C = math.sqrt(2.0 / math.pi)
PEER_ROWS = 8
PEER_VMEM_LIMIT = 58 * 1024 * 1024
GATE_RB = 32
GATE_IG = 4


def _gelu_tanh(x):
    k = -2.0 * GELU_C * LOG2E
    return x / (1.0 + jnp.exp2(x * (k + (k * 0.044715) * (x * x))))


def _peer_expert_kernel(*refs, final):
    refs = list(refs)
    x_ref, mod_ref, xnt_ref, e1p_ref, thp_ref, e1c_ref, thc_ref, s2_ref, e2_ref, u_ref, vt_ref = refs[:11]
    pos = 11
    if final:
        fg_ref = refs[pos]
        pos += 1
    o_ref, acc_scr = refs[pos:pos + 2]
    a_scr = refs[pos + 2:pos + 4]
    wg_scr = refs[pos + 4:pos + 6]
    be1_scr, bth_scr = refs[pos + 6:pos + 8]
    g = pl.program_id(2)
    last = pl.num_programs(2) - 1
    eb = PEER_ROWS * PEER_NKEYS

    def scores(blk, slot):
        a_scr[slot][...] = jnp.dot(u_ref[blk * eb:(blk + 1) * eb, :], xnt_ref[...], preferred_element_type=F32)

    def gate(slot, e1_ref, th_ref):
        tm = s2_ref.shape[2]
        zero8 = jnp.zeros((8, tm), F32)
        for h in range(PEER_HEADS):
            for ii in range(PEER_ROWS):
                be1_scr[h * PEER_ROWS + ii] = e1_ref[h, pl.ds(ii, 1), :] + zero8
                bth_scr[h * PEER_ROWS + ii] = th_ref[h, pl.ds(ii, 1), :] + zero8
        for c0 in range(0, tm, LANES):
            cs = slice(c0, c0 + LANES)
            for r0 in range(0, PEER_NKEYS, GATE_RB):
                for ig in range(0, PEER_ROWS, GATE_IG):
                    ws = [None] * GATE_IG
                    for h in range(PEER_HEADS):
                        s2 = s2_ref[h, r0:r0 + GATE_RB, cs].reshape(GATE_RB // 8, 8, LANES)
                        e2 = e2_ref[h, r0:r0 + GATE_RB, cs].reshape(GATE_RB // 8, 8, LANES)
                        for k in range(GATE_IG):
                            row = h * PEER_ROWS + ig + k
                            contrib = jnp.where(s2 >= bth_scr[row, :, cs][None], e2 * be1_scr[row, :, cs][None], 0.0)
                            ws[k] = contrib if ws[k] is None else ws[k] + contrib
                    for k in range(GATE_IG):
                        rs = slice((ig + k) * PEER_NKEYS + r0, (ig + k) * PEER_NKEYS + r0 + GATE_RB)
                        wk = ws[k].reshape(GATE_RB, LANES)
                        wg_scr[slot][rs, cs] = (wk * _gelu_tanh(a_scr[slot][rs, cs])).astype(BF16)

    def accum(slot, blk):
        acc_scr[...] += jnp.dot(vt_ref[:, blk * eb:(blk + 1) * eb], wg_scr[slot][...], preferred_element_type=F32)

    @pl.when(g == 0)
    def _():
        acc_scr[...] = jnp.zeros_like(acc_scr)
        scores(1, 1)
        scores(0, 0)
        gate(0, e1c_ref, thc_ref)

    @pl.when((g > 0) & (g < last))
    def _():
        gate(1, e1p_ref, thp_ref)
        accum(0, 0)
        scores(1, 1)
        accum(1, 1)
        scores(0, 0)
        gate(0, e1c_ref, thc_ref)

    @pl.when(g == last)
    def _():
        gate(1, e1p_ref, thp_ref)
        accum(0, 0)
        accum(1, 1)
        y = x_ref[...] + mod_ref[5:6, :] * acc_scr[...].T
        if final:
            ms = jnp.mean(y * y, axis=-1, keepdims=True)
            y = y * lax.rsqrt(ms + NORM_EPS) * fg_ref[...]
        o_ref[...] = y


def _peer_experts(x, mod, xnt, e1, th, s2, e2, u, vt, final_g):
    b, l, d = x.shape
    tm = min(512, l)
    nt = l // tm
    eb = PEER_ROWS * PEER_NKEYS
    n_blocks = u.shape[0] // eb
    assert n_blocks % 2 == 0
    n_steps = n_blocks // 2 + 1
    final = final_g is not None
    tok = lambda bi, i: bi * nt + i
    prev_rows = lambda bi, i, g: (0, jnp.maximum(2 * g - 1, 0), tok(bi, i))
    cur_rows = lambda bi, i, g: (0, jnp.minimum(2 * g, n_blocks - 1), tok(bi, i))
    row_blk = (PEER_HEADS, PEER_ROWS, tm)
    in_specs = [
        pl.BlockSpec((None, tm, d), lambda bi, i, g: (bi, i, 0)),
        pl.BlockSpec((None, 6, d), lambda bi, i, g: (bi, 0, 0)),
        pl.BlockSpec((d, tm), lambda bi, i, g: (0, tok(bi, i))),
        pl.BlockSpec(row_blk, prev_rows),
        pl.BlockSpec(row_blk, prev_rows),
        pl.BlockSpec(row_blk, cur_rows),
        pl.BlockSpec(row_blk, cur_rows),
        pl.BlockSpec((PEER_HEADS, PEER_NKEYS, tm), lambda bi, i, g: (0, 0, tok(bi, i))),
        pl.BlockSpec((PEER_HEADS, PEER_NKEYS, tm), lambda bi, i, g: (0, 0, tok(bi, i))),
        pl.BlockSpec((2 * eb, d), lambda bi, i, g: (jnp.minimum(g, n_steps - 2), 0)),
        pl.BlockSpec((d, 2 * eb), lambda bi, i, g: (0, jnp.maximum(g - 1, 0))),
    ]
    args = [x, mod, xnt, e1, th, e1, th, s2, e2, u, vt]
    if final:
        in_specs.append(pl.BlockSpec((1, d), lambda bi, i, g: (0, 0)))
        args.append(final_g.reshape(1, d))
    return pl.pallas_call(
        functools.partial(_peer_expert_kernel, final=final),
        grid=(b, nt, n_steps),
        in_specs=in_specs,
        out_specs=pl.BlockSpec((None, tm, d), lambda bi, i, g: (bi, i, 0)),
        out_shape=jax.ShapeDtypeStruct((b, l, d), F32),
        scratch_shapes=[pltpu.VMEM((d, tm), F32), pltpu.VMEM((eb, tm), F32), pltpu.VMEM((eb, tm), F32),
                        pltpu.VMEM((eb, tm), BF16), pltpu.VMEM((eb, tm), BF16),
                        pltpu.VMEM((PEER_HEADS * PEER_ROWS, 8, tm), F32), pltpu.VMEM((PEER_HEADS * PEER_ROWS, 8, tm), F32)],
        compiler_params=_cparams(("parallel", "parallel", "arbitrary"), PEER_VMEM_LIMIT),
        name="peer_experts",
    )(*args)


def _pad_cols(a, width):
    return jnp.pad(a, ((0, 0), (0, width - a.shape[1])))


def _layout_w_in(w):
    d = w.shape[0]
    z = lambda n: jnp.zeros((d, n), w.dtype)
    gla, hg, mla, dif = 0, 800, 2080, 2432
    parts = [w[:, gla:gla + 768], w[:, hg:hg + 768],
             w[:, mla:mla + 192], z(64), w[:, mla + 192:mla + 320], z(64), w[:, mla + 320:mla + 352], z(32)]
    for base in (dif, dif + 256, dif + 512):
        for h in range(DIFF_HEADS):
            parts += [w[:, base + 64 * h:base + 64 * (h + 1)], z(64)]
    parts += [w[:, gla + 768:gla + 800], z(224), w[:, hg + 768:hg + 1280]]
    out = jnp.concatenate(parts, axis=1)
    assert out.shape[1] == MAIN_COLS + GATE_COLS
    return out.astype(BF16)


def _layout_mla(w_uq, w_ukv):
    dq = MLA_NOPE + MLA_ROPE
    wq = jnp.concatenate([_pad_cols(w_uq[:, dq * h:dq * (h + 1)], LANES) for h in range(MLA_HEADS)], axis=1)
    wq = jnp.pad(wq, ((0, 256 - MLA_Q_RANK), (0, 0)))
    per = MLA_NOPE + MLA_DV
    wk = jnp.concatenate([_pad_cols(w_ukv[:, per * h:per * h + MLA_NOPE], LANES) for h in range(MLA_HEADS)], axis=1)
    wv = jnp.concatenate([_pad_cols(w_ukv[:, per * h + MLA_NOPE:per * (h + 1)], LANES) for h in range(MLA_HEADS)], axis=1)
    return wq.astype(BF16), jnp.concatenate([wk, wv], axis=1).astype(BF16)


def kernel(x, c, ctx, c_ctx, ada_w, ada_b, norm_mix_g, norm_ffn_g, w_in, w_out, gla_gate_w, gla_gate_b, gla_norm_g, hgrn_lb_raw, hgrn_norm_g, mla_q_norm_g, mla_kv_norm_g, mla_w_uq, mla_w_ukv, diff_lambda, diff_norm_g, peer_wq, peer_keys, peer_u, peer_v, final_norm_g):
    b, l, d = x.shape
    lc = ctx.shape[1]
    depth = ada_w.shape[0]
    assert lc <= ATT_TK, "context keys are consumed as one attention step"

    rows = -(-(b + 1) // 8) * 8
    cpad = jnp.zeros((rows, d), F32).at[:b].set(c).at[b].set(c_ctx)
    mod_all = _adaln(cpad, ada_w, ada_b)

    lb_sm = jax.nn.softmax(hgrn_lb_raw.astype(F32), axis=0)
    lb_all = jnp.cumsum(lb_sm, axis=0) - lb_sm[0]

    lat_mla_tab = _rope_tables(l, 64, 32, True)
    ctx_mla_tab = _rope_tables(lc, 64, 32, False)
    lat_dif_tab = _rope_tables(l, 0, 64, True)
    ctx_dif_tab = _rope_tables(lc, 0, 64, False)

    xc = ctx
    for li in range(depth):
        need_ctx = li < depth - 1
        lam_init = 0.8 - 0.6 * math.exp(-0.3 * li)
        mod_l = mod_all[li, :b].reshape(b, 6, d)
        mod_c = jnp.broadcast_to(mod_all[li, b].reshape(1, 6, d), (b, 6, d))

        w_perm = _layout_w_in(w_in[li])
        w_o = w_out[li].astype(BF16)
        wg = jnp.zeros((2, 256, 128), F32)
        wg = wg.at[0, 0:GLA_GATE_RANK].set(gla_gate_w[li, 0]).at[1, GLA_GATE_RANK:2 * GLA_GATE_RANK].set(gla_gate_w[li, 1])
        wg = wg.astype(BF16)
        bg = gla_gate_b[li].reshape(2, 1, 128)
        gla_gn = jnp.tile(gla_norm_g[li], GLA_HEADS).reshape(1, 256)
        hg_gn = jnp.tile(hgrn_norm_g[li], HG_HEADS).reshape(1, 256)
        lb = lb_all[li]
        hg_p = jnp.zeros((8, 256), F32).at[0].set(jnp.log(jnp.maximum(lb, LB_FLOOR))).at[1].set(jnp.log1p(-lb)).at[2].set(1.0 - lb)
        hg_dummy = jnp.zeros((8, 128), F32)
        gq = jnp.pad(mla_q_norm_g[li], (0, 256 - MLA_Q_RANK)).reshape(1, 256)
        gkv = mla_kv_norm_g[li].reshape(1, MLA_KV_RANK)
        wq_mla, wkv_mla = _layout_mla(mla_w_uq[li], mla_w_ukv[li])
        lq1, lk1, lq2, lk2 = diff_lambda[li, 0], diff_lambda[li, 1], diff_lambda[li, 2], diff_lambda[li, 3]
        lam = (jnp.exp(jnp.sum(lq1 * lk1).astype(F32)) - jnp.exp(jnp.sum(lq2 * lk2).astype(F32)) + lam_init)
        lam_row = jnp.full((1, LANES), lam, F32)
        dif_gn = jnp.tile(diff_norm_g[li], 2).reshape(1, LANES)
        wq_peer = peer_wq[li].astype(BF16)
        keys = peer_keys[li].astype(BF16)
        u_bf = peer_u[li].astype(BF16)
        vt_bf = peer_v[li].astype(BF16).T

        main_c, gate_c = _inproj(xc, norm_mix_g[li], mod_c, w_perm)
        main_l, gate_l = _inproj(x, norm_mix_g[li], mod_l, w_perm)

        mixes_c, mixes_l = [], []
        for mixer, pa_f, pb_f, pa_b, pb_b, gn, dk in (
            ("gla", wg[0], bg[0], wg[1], bg[1], gla_gn, 128),
            ("hg", hg_p, hg_dummy, hg_p, hg_dummy, hg_gn, 256),
        ):
            zero = jnp.zeros((b, 256, dk), F32)
            o_cf, s_f = _scan(main_c, gate_c, pa_f, pb_f, gn, zero, None, mixer=mixer, reverse=False, chunk=64)
            mix_c, s_b = _scan(main_c, gate_c, pa_b, pb_b, gn, zero, o_cf, mixer=mixer, reverse=True, chunk=64)
            o_lf, _ = _scan(main_l, gate_l, pa_f, pb_f, gn, s_f, None, mixer=mixer, reverse=False, chunk=64)
            mix_l, _ = _scan(main_l, gate_l, pa_b, pb_b, gn, s_b, o_lf, mixer=mixer, reverse=True, chunk=64)
            mixes_c.append(mix_c)
            mixes_l.append(mix_l)

        qm_c, km_c, vm_c = _mla_prep(main_c, gq, gkv, wq_mla, wkv_mla, ctx_mla_tab)
        qm_l, km_l, vm_l = _mla_prep(main_l, gq, gkv, wq_mla, wkv_mla, lat_mla_tab)
        mla_l = _attention(qm_l, km_c, vm_c, km_l, vm_l, None, n_maps=1, lam_init=lam_init)
        qd_c, kd_c, vd_c = _diff_prep(main_c, ctx_dif_tab)
        qd_l, kd_l, vd_l = _diff_prep(main_l, lat_dif_tab)
        dif_l = _attention(qd_l, kd_c, vd_c, kd_l, vd_l, (lam_row, dif_gn), n_maps=2, lam_init=lam_init)

        x = _outproj(x, mixes_l[0], mixes_l[1], mla_l, dif_l, w_o, mod_l)
        rt = _peer_route(x, norm_ffn_g[li], mod_l, wq_peer, keys)
        x = _peer_experts(x, mod_l, *rt, u_bf, vt_bf, None if need_ctx else final_norm_g)

        if need_ctx:
            mla_c = _attention(qm_c, km_c, vm_c, None, None, None, n_maps=1, lam_init=lam_init)
            dif_c = _attention(qd_c, kd_c, vd_c, None, None, (lam_row, dif_gn), n_maps=2, lam_init=lam_init)
            xc = _outproj(xc, mixes_c[0], mixes_c[1], mla_c, dif_c, w_o, mod_c)
            rtc = _peer_route(xc, norm_ffn_g[li], mod_c, wq_peer, keys)
            xc = _peer_experts(xc, mod_c, *rtc, u_bf, vt_bf, None)
    return x
```

```python
import functools
import math

import jax
import jax.numpy as jnp
import numpy as np
from jax import lax
from jax.experimental import pallas as pl
from jax.experimental.pallas import tpu as pltpu

F32 = jnp.float32
BF16 = jnp.bfloat16

NORM_EPS = 1e-6
ROPE_BASE = 10000.0
GRID_W = 64
LB_FLOOR = 1e-30

GLA_HEADS, GLA_DK, GLA_DV, GLA_GATE_RANK, GLA_GATE_NORM = 4, 32, 64, 16, 16.0
HG_HEADS, HG_DIM = 4, 64
MLA_HEADS, MLA_Q_RANK, MLA_KV_RANK, MLA_NOPE, MLA_ROPE, MLA_DV = 4, 192, 128, 64, 32, 64
DIFF_HEADS, DIFF_DQK, DIFF_DV = 4, 32, 64
PEER_HEADS, PEER_NKEYS, PEER_DQ, PEER_TOPK = 8, 128, 256, 16

LANES = 128
VMEM_LIMIT = 48 * 1024 * 1024
EXP_CAP = 80.0
NEG_BIG = -1e30
ATT_TK = 512
LOG2E = 1.4426950408889634
ATT_R = 512
VT_ROWS = 80

COL_GLA, COL_HG, COL_MLA, COL_DQ, COL_DK, COL_DV, MAIN_COLS = 0, 768, 1536, 2048, 2560, 3072, 3584
GATE_COLS = 768

STAIR = [(a, b) for a in range(PEER_TOPK) for b in range(PEER_TOPK) if (a + 1) * (b + 1) <= PEER_TOPK]


def _cparams(sem, vmem=VMEM_LIMIT):
    return pltpu.CompilerParams(dimension_semantics=sem, vmem_limit_bytes=vmem)


def _sigmoid(x):
    return 1.0 / (1.0 + jnp.exp(-x))


def _logsig(x):
    return jnp.minimum(x, 0.0) - jnp.log(1.0 + jnp.exp(-jnp.abs(x)))


def _nt(a, b):
    return lax.dot_general(a, b, (((1,), (1,)), ((), ())), preferred_element_type=F32)


def _tn(a, b):
    return lax.dot_general(a, b, (((0,), (0,)), ((), ())), preferred_element_type=F32)


def _adaln_kernel(c_ref, w_ref, b_ref, o_ref):
    c = c_ref[...]
    s = c * _sigmoid(c)
    o_ref[...] = jnp.dot(s.astype(BF16), w_ref[...].astype(BF16), preferred_element_type=F32) + b_ref[...]


def _adaln(cpad, ada_w, ada_b):
    depth, d, n6 = ada_w.shape
    rows = cpad.shape[0]
    tn = 512
    return pl.pallas_call(
        _adaln_kernel,
        grid=(depth, n6 // tn),
        in_specs=[
            pl.BlockSpec((rows, d), lambda l, j: (0, 0)),
            pl.BlockSpec((None, d, tn), lambda l, j: (l, 0, j)),
            pl.BlockSpec((None, 1, tn), lambda l, j: (l, 0, j)),
        ],
        out_specs=pl.BlockSpec((None, rows, tn), lambda l, j: (l, 0, j)),
        out_shape=jax.ShapeDtypeStruct((depth, rows, n6), F32),
        compiler_params=_cparams(("parallel", "parallel")),
        name="adaln",
    )(cpad, ada_w, ada_b.reshape(depth, 1, n6))


def _modnorm(x, g, mod_ref, shift_row, scale_row):
    ms = jnp.mean(x * x, axis=-1, keepdims=True)
    h = x * lax.rsqrt(ms + NORM_EPS) * g
    return h * (1.0 + mod_ref[scale_row:scale_row + 1, :]) + mod_ref[shift_row:shift_row + 1, :]


def _inproj_kernel(x_ref, g_ref, mod_ref, w_ref, main_ref, gate_ref):
    h = _modnorm(x_ref[...], g_ref[...], mod_ref, 0, 1)
    y = jnp.dot(h.astype(BF16), w_ref[...], preferred_element_type=F32)
    main_ref[...] = y[:, :MAIN_COLS].astype(BF16)
    gate_ref[...] = y[:, MAIN_COLS:]


def _inproj(x, g, mod, w_perm):
    b, l, d = x.shape
    tm = min(256, l)
    nc = w_perm.shape[1]
    return pl.pallas_call(
        _inproj_kernel,
        grid=(b, l // tm),
        in_specs=[
            pl.BlockSpec((None, tm, d), lambda bi, i: (bi, i, 0)),
            pl.BlockSpec((1, d), lambda bi, i: (0, 0)),
            pl.BlockSpec((None, 6, d), lambda bi, i: (bi, 0, 0)),
            pl.BlockSpec((d, nc), lambda bi, i: (0, 0)),
        ],
        out_specs=[
            pl.BlockSpec((None, tm, MAIN_COLS), lambda bi, i: (bi, i, 0)),
            pl.BlockSpec((None, tm, GATE_COLS), lambda bi, i: (bi, i, 0)),
        ],
        out_shape=[
            jax.ShapeDtypeStruct((b, l, MAIN_COLS), BF16),
            jax.ShapeDtypeStruct((b, l, GATE_COLS), F32),
        ],
        compiler_params=_cparams(("parallel", "parallel")),
        name="inproj",
    )(x, g.reshape(1, d), mod, w_perm)


def _scan_kernel(*refs, mixer, reverse, finish, chunk):
    if finish:
        main_ref, gate_ref, pa_ref, pb_ref, gn_ref, s0_ref, oprev_ref, o_ref, sfin_ref, st_scr = refs
    else:
        main_ref, gate_ref, pa_ref, pb_ref, gn_ref, s0_ref, o_ref, sfin_ref, st_scr = refs
        oprev_ref = None
    i = pl.program_id(1)

    @pl.when(i == 0)
    def _():
        st_scr[...] = s0_ref[...]

    main = main_ref[...]
    t = main.shape[0]
    c = chunk
    nc = t // c
    heads = 4
    dv = 256
    v = main[:, 256:512]
    gcol = main[:, 512:768].astype(F32)
    if mixer == "gla":
        dk = 128
        q = main[:, 0:128].astype(F32) * (GLA_DK ** -0.5)
        k = main[:, 128:256].astype(F32)
        z = jnp.dot(gate_ref[...].astype(BF16), pa_ref[...], preferred_element_type=F32) + pb_ref[...]
        g = _logsig(z) * (1.0 / GLA_GATE_NORM)
    else:
        dk = 256
        qq = main[:, 0:256].astype(F32)
        q = qq * _sigmoid(qq)
        z = gate_ref[...]
        la = pa_ref[0:1, :]
        lb = pa_ref[1:2, :] + _logsig(z)
        g = jnp.maximum(la, lb) + jnp.log(1.0 + jnp.exp(-jnp.abs(la - lb)))
        k = pa_ref[2:3, :] * _sigmoid(-z)
    hk = dk // heads
    hv = dv // heads

    rt = lax.broadcasted_iota(jnp.int32, (t, t), 0)
    ct = lax.broadcasted_iota(jnp.int32, (t, t), 1)
    same = (rt // c) == (ct // c)
    tri = jnp.where(same & ((ct >= rt) if reverse else (ct <= rt)), 1.0, 0.0).astype(BF16)
    g1 = g.astype(BF16)
    r1 = g - g1.astype(F32)
    g2 = r1.astype(BF16)
    g3 = (r1 - g2.astype(F32)).astype(BF16)
    bcum = (jnp.dot(tri, g1, preferred_element_type=F32) + jnp.dot(tri, g2, preferred_element_type=F32)
            + jnp.dot(tri, g3, preferred_element_type=F32))

    b3 = bcum.reshape(nc, c, dk)
    q3 = q.reshape(nc, c, dk)
    k3 = k.reshape(nc, c, dk)
    if reverse:
        bmid = b3[:, c // 2:c // 2 + 1, :]
        bend = b3[:, 0:1, :]
    else:
        bmid = b3[:, c // 2 - 1:c // 2, :]
        bend = b3[:, c - 1:c, :]
    eb = b3 - bmid
    qa = (q3 * jnp.exp(jnp.minimum(eb, EXP_CAP))).reshape(t, dk).astype(BF16)
    ka = (k3 * jnp.exp(jnp.minimum(-eb, EXP_CAP))).reshape(t, dk).astype(BF16)
    qi = (q3 * jnp.exp(b3)).reshape(t, dk).astype(BF16)
    kb = (k3 * jnp.exp(bend - b3)).reshape(t, dk).astype(BF16)
    dec = jnp.exp(bend)

    lane_k = lax.broadcasted_iota(jnp.int32, (1, dk), 1) // hk
    lane_v = lax.broadcasted_iota(jnp.int32, (1, dv), 1) // hv
    rr = lax.broadcasted_iota(jnp.int32, (c, heads * c), 0)
    cc = lax.broadcasted_iota(jnp.int32, (c, heads * c), 1) % c
    causal = (cc >= rr) if reverse else (cc <= rr)
    bd = (lax.broadcasted_iota(jnp.int32, (dv, dk), 0) // hv) == (lax.broadcasted_iota(jnp.int32, (dv, dk), 1) // hk)

    o_intra = []
    upd = []
    for ci in range(nc):
        sl = slice(ci * c, (ci + 1) * c)
        ka_c = ka[sl]
        v_c = v[sl]
        kst = jnp.concatenate([jnp.where(lane_k == h, ka_c, 0) for h in range(heads)], axis=0)
        vst = jnp.concatenate([jnp.where(lane_v == h, v_c, 0) for h in range(heads)], axis=0)
        sw = _nt(qa[sl], kst)
        p = jnp.where(causal, sw, 0.0).astype(BF16)
        o_intra.append(jnp.dot(p, vst, preferred_element_type=F32))
        upd.append(jnp.where(bd, _tn(v_c, kb[sl]), 0.0))

    outs = [None] * nc
    order = range(nc - 1, -1, -1) if reverse else range(nc)
    for ci in order:
        sl = slice(ci * c, (ci + 1) * c)
        st = st_scr[...]
        outs[ci] = o_intra[ci] + _nt(qi[sl], st.astype(BF16))
        st_scr[...] = st * dec[ci] + upd[ci]
    o = jnp.concatenate(outs, axis=0)

    @pl.when(i == pl.num_programs(1) - 1)
    def _():
        sfin_ref[...] = st_scr[...]

    if not finish:
        o_ref[...] = o
    else:
        ot = o + oprev_ref[...]
        sq = ot * ot
        jv = ((lax.broadcasted_iota(jnp.int32, (dv, dv), 0) // hv)
              == (lax.broadcasted_iota(jnp.int32, (dv, dv), 1) // hv))
        jm = jnp.where(jv, 1.0, 0.0).astype(BF16)
        s1 = sq.astype(BF16)
        s2 = (sq - s1.astype(F32)).astype(BF16)
        ms = (jnp.dot(s1, jm, preferred_element_type=F32) + jnp.dot(s2, jm, preferred_element_type=F32)) * (1.0 / hv)
        y = ot * lax.rsqrt(ms + NORM_EPS) * gn_ref[...]
        o_ref[...] = (y * (gcol * _sigmoid(gcol))).astype(BF16)


def _scan(main, gate, pa, pb, gn, s0, oprev, *, mixer, reverse, chunk):
    b, l, _ = main.shape
    t = min(256, l)
    nb = l // t
    finish = oprev is not None
    dk = 128 if mixer == "gla" else 256
    dv = 256
    main_blk = 0 if mixer == "gla" else 1
    gate_blk = 0 if mixer == "gla" else (2 if reverse else 1)

    def tok(bi, i):
        return (nb - 1 - i) if reverse else i

    in_specs = [
        pl.BlockSpec((None, t, 768), lambda bi, i: (bi, tok(bi, i), main_blk)),
        pl.BlockSpec((None, t, 256), lambda bi, i: (bi, tok(bi, i), gate_blk)),
        pl.BlockSpec(pa.shape, lambda bi, i: (0, 0)),
        pl.BlockSpec(pb.shape, lambda bi, i: (0, 0)),
        pl.BlockSpec((1, dv), lambda bi, i: (0, 0)),
        pl.BlockSpec((None, dv, dk), lambda bi, i: (bi, 0, 0)),
    ]
    args = [main, gate, pa, pb, gn, s0]
    if finish:
        in_specs.append(pl.BlockSpec((None, t, dv), lambda bi, i: (bi, tok(bi, i), 0)))
        args.append(oprev)
    return pl.pallas_call(
        functools.partial(_scan_kernel, mixer=mixer, reverse=reverse, finish=finish, chunk=chunk),
        grid=(b, nb),
        in_specs=in_specs,
        out_specs=[
            pl.BlockSpec((None, t, dv), lambda bi, i: (bi, tok(bi, i), 0)),
            pl.BlockSpec((None, dv, dk), lambda bi, i: (bi, 0, 0)),
        ],
        out_shape=[
            jax.ShapeDtypeStruct((b, l, dv), BF16 if finish else F32),
            jax.ShapeDtypeStruct((b, dv, dk), F32),
        ],
        scratch_shapes=[pltpu.VMEM((dv, dk), F32)],
        compiler_params=_cparams(("parallel", "arbitrary")),
        name=f"scan_{mixer}_{'bwd' if reverse else 'fwd'}",
    )(*args)


def _rope(x, cos, sin_lo, sin_hi):
    return x * cos + pltpu.roll(x, LANES - 8, 1) * sin_lo + pltpu.roll(x, 8, 1) * sin_hi


def _rope_tables(l, first_lane, n_lanes, rotate):
    lane = np.arange(LANES)
    d = (lane - first_lane) % 32
    active = (lane >= first_lane) & (lane < first_lane + n_lanes)
    freqs = ROPE_BASE ** (-(d % 8).astype(np.float32) / 8.0)
    tt = jnp.arange(l, dtype=jnp.int32)
    rows = (tt // GRID_W).astype(F32)[:, None]
    cols = (tt % GRID_W).astype(F32)[:, None]
    pos = jnp.where(jnp.asarray(d < 16)[None, :], rows, cols)
    ang = pos * jnp.asarray(freqs, F32)[None, :]
    act = jnp.asarray(active)[None, :] & rotate
    lo = jnp.asarray((d % 16) < 8)[None, :]
    cos = jnp.where(act, jnp.cos(ang), 1.0)
    sin = jnp.where(act, jnp.sin(ang), 0.0)
    return cos, jnp.where(lo, -sin, 0.0), jnp.where(lo, 0.0, sin)


def _vt_block(vgrp):
    lane = lax.broadcasted_iota(jnp.int32, (1, LANES), 1)
    return jnp.where(lane == 64, 1.0, vgrp).T[:VT_ROWS].astype(BF16)


def _mla_prep_kernel(slab_ref, gq_ref, gkv_ref, wq_ref, wkv_ref, cos_ref, slo_ref, shi_ref, qt_ref, k_ref, vt_ref):
    slab = slab_ref[...]
    cos, slo, shi = cos_ref[...], slo_ref[...], shi_ref[...]
    cq = slab[:, 0:256].astype(F32)
    qn = cq * lax.rsqrt(jnp.sum(cq * cq, axis=-1, keepdims=True) * (1.0 / MLA_Q_RANK) + NORM_EPS) * gq_ref[...]
    qall = jnp.dot(qn.astype(BF16), wq_ref[...], preferred_element_type=F32)
    ckv = slab[:, 256:384].astype(F32)
    kvn = ckv * lax.rsqrt(jnp.mean(ckv * ckv, axis=-1, keepdims=True) + NORM_EPS) * gkv_ref[...]
    kvall = jnp.dot(kvn.astype(BF16), wkv_ref[...], preferred_element_type=F32)
    kr = _rope(slab[:, 384:512].astype(F32), cos, slo, shi)
    scale = (MLA_NOPE + MLA_ROPE) ** -0.5 * LOG2E
    for h in range(MLA_HEADS):
        qh = _rope(qall[:, h * LANES:(h + 1) * LANES], cos, slo, shi)
        qt_ref[h] = (qh * scale).T.astype(BF16)
        k_ref[h] = (kvall[:, h * LANES:(h + 1) * LANES] + kr).astype(BF16)
        vt_ref[h] = _vt_block(kvall[:, (MLA_HEADS + h) * LANES:(MLA_HEADS + h + 1) * LANES])


def _attn_prep_specs(b, l, tm, heads):
    qt = pl.BlockSpec((None, heads, LANES, tm), lambda bi, i: (bi, 0, 0, i))
    kk = pl.BlockSpec((None, heads, tm, LANES), lambda bi, i: (bi, 0, i, 0))
    vt = pl.BlockSpec((None, heads, None, VT_ROWS, tm), lambda bi, i: (bi, 0, i, 0, 0))
    shapes = [
        jax.ShapeDtypeStruct((b, heads, LANES, l), BF16),
        jax.ShapeDtypeStruct((b, heads, l, LANES), BF16),
        jax.ShapeDtypeStruct((b, heads, l // tm, VT_ROWS, tm), BF16),
    ]
    return [qt, kk, vt], shapes


def _mla_prep(main, gq, gkv, wq, wkv, tables):
    b, l, _ = main.shape
    tm = min(ATT_TK, l)
    cos, slo, shi = tables
    tab = pl.BlockSpec((tm, LANES), lambda bi, i: (i, 0))
    out_specs, out_shape = _attn_prep_specs(b, l, tm, MLA_HEADS)
    return pl.pallas_call(
        _mla_prep_kernel,
        grid=(b, l // tm),
        in_specs=[
            pl.BlockSpec((None, tm, 512), lambda bi, i: (bi, i, COL_MLA // 512)),
            pl.BlockSpec(gq.shape, lambda bi, i: (0, 0)),
            pl.BlockSpec(gkv.shape, lambda bi, i: (0, 0)),
            pl.BlockSpec(wq.shape, lambda bi, i: (0, 0)),
            pl.BlockSpec(wkv.shape, lambda bi, i: (0, 0)),
            tab, tab, tab,
        ],
        out_specs=out_specs,
        out_shape=out_shape,
        compiler_params=_cparams(("parallel", "parallel")),
        name="mla_prep",
    )(main, gq, gkv, wq, wkv, cos, slo, shi)


def _diff_prep_kernel(q_in, k_in, v_in, cos_ref, slo_ref, shi_ref, qt_ref, k_ref, vt_ref):
    cos, slo, shi = cos_ref[...], slo_ref[...], shi_ref[...]
    qs = q_in[...]
    ks = k_in[...]
    vs = v_in[...]
    scale = DIFF_DQK ** -0.5 * LOG2E
    for h in range(DIFF_HEADS):
        sl = slice(h * LANES, (h + 1) * LANES)
        qt_ref[h] = (_rope(qs[:, sl].astype(F32), cos, slo, shi) * scale).T.astype(BF16)
        k_ref[h] = _rope(ks[:, sl].astype(F32), cos, slo, shi).astype(BF16)
        vt_ref[h] = _vt_block(vs[:, sl].astype(F32))


def _diff_prep(main, tables):
    b, l, _ = main.shape
    tm = min(ATT_TK, l)
    cos, slo, shi = tables
    tab = pl.BlockSpec((tm, LANES), lambda bi, i: (i, 0))
    out_specs, out_shape = _attn_prep_specs(b, l, tm, DIFF_HEADS)
    return pl.pallas_call(
        _diff_prep_kernel,
        grid=(b, l // tm),
        in_specs=[
            pl.BlockSpec((None, tm, 512), lambda bi, i: (bi, i, COL_DQ // 512)),
            pl.BlockSpec((None, tm, 512), lambda bi, i: (bi, i, COL_DK // 512)),
            pl.BlockSpec((None, tm, 512), lambda bi, i: (bi, i, COL_DV // 512)),
            tab, tab, tab,
        ],
        out_specs=out_specs,
        out_shape=out_shape,
        compiler_params=_cparams(("parallel", "parallel")),
        name="diff_prep",
    )(main, main, main, cos, slo, shi)


def _attn_kernel(*refs, n_maps, has_lat, lam_init):
    refs = list(refs)
    qt_ref, kc_ref, vtc_ref = refs[:3]
    pos = 3
    if has_lat:
        kl_ref, vtl_ref = refs[pos:pos + 2]
        pos += 2
    if n_maps == 2:
        lam_ref, gn_ref = refs[pos:pos + 2]
        pos += 2
    o_ref = refs[pos]
    if has_lat:
        m_scr, acc_scr, s_scr, cm_scr, al_scr, p_scr = refs[pos + 1:pos + 7]
    tq = qt_ref.shape[2]
    feat = lax.broadcasted_iota(jnp.int32, (LANES, 1), 0)

    qts = []
    for hh in range(2):
        qt = qt_ref[hh]
        if n_maps == 2:
            zero = jnp.zeros_like(qt)
            qt = jnp.concatenate(
                [jnp.where((feat >= DIFF_DQK * mi) & (feat < DIFF_DQK * (mi + 1)), qt, zero) for mi in range(2)], axis=1)
        qts.append(qt)

    accs = []
    for hh in range(2):
        st = jnp.dot(kc_ref[hh], qts[hh], preferred_element_type=F32)
        m0 = jnp.max(st, axis=0, keepdims=True)
        acc0 = jnp.dot(vtc_ref[hh, 0], jnp.exp2(st - m0).astype(BF16), preferred_element_type=F32)
        accs.append(acc0)
        if has_lat:
            m_scr[hh] = m0
            acc_scr[hh] = acc0

    if has_lat:
        n = vtl_ref.shape[1]
        tk = vtl_ref.shape[3]

        def scores(j, slot):
            off = pl.multiple_of(j * tk, tk)
            for hh in range(2):
                st = jnp.dot(kl_ref[hh, pl.ds(off, tk), :], qts[hh], preferred_element_type=F32)
                s_scr[hh, slot] = st
                cm_scr[hh, slot] = jnp.max(st, axis=0, keepdims=True)

        def numer(slot):
            for hh in range(2):
                m_old = m_scr[hh]
                m_new = jnp.maximum(m_old, cm_scr[hh, slot])
                al_scr[hh, slot] = jnp.exp2(m_old - m_new)
                p_scr[hh, slot] = jnp.exp2(s_scr[hh, slot] - m_new).astype(BF16)
                m_scr[hh] = m_new

        def values(j, slot):
            for hh in range(2):
                acc_scr[hh] = (al_scr[hh, slot] * acc_scr[hh]
                               + jnp.dot(vtl_ref[hh, j], p_scr[hh, slot], preferred_element_type=F32))

        first = n % 2
        if first:
            scores(0, 0)
            numer(0)
            values(0, 0)
        if n > first:
            for hh in range(2):
                p_scr[hh, 1] = jnp.zeros(p_scr.shape[2:], BF16)
                al_scr[hh, 1] = jnp.ones(al_scr.shape[2:], F32)
            scores(first, 0)

            def body(t, carry):
                j = first + 2 * t
                scores(j + 1, 1)
                numer(0)
                values(jnp.maximum(j - 1, first), 1)
                scores(jnp.minimum(j + 2, n - 1), 0)
                numer(1)
                values(j, 0)
                return carry

            lax.fori_loop(0, (n - first) // 2, body, 0)
            values(n - 1, 1)
        accs = [acc_scr[0], acc_scr[1]]

    outs = []
    for hh in range(2):
        acc = accs[hh]
        ot = acc[0:64, :] / acc[64:65, :]
        if n_maps == 2:
            oh = ot[:, :tq] - lam_ref[:, 0:1] * ot[:, tq:]
            ms = jnp.mean(oh * oh, axis=0, keepdims=True)
            ot = oh * lax.rsqrt(ms + NORM_EPS)
        outs.append(ot)
    o = jnp.concatenate(outs, axis=0).T
    if n_maps == 2:
        o = o * gn_ref[...] * (1.0 - lam_init)
    o_ref[...] = o.astype(BF16)


def _attention(qt, kc, vtc, kl, vtl, extra, *, n_maps, lam_init):
    b, nh, _, l = qt.shape
    lc = kc.shape[2]
    tq = min(ATT_R // n_maps, l)
    r = n_maps * tq
    has_lat = kl is not None
    scratch = []
    in_specs = [
        pl.BlockSpec((None, 2, LANES, tq), lambda bi, hp, i: (bi, hp, 0, i)),
        pl.BlockSpec((None, 2, lc, LANES), lambda bi, hp, i: (bi, hp, 0, 0)),
        pl.BlockSpec((None, 2) + vtc.shape[2:], lambda bi, hp, i: (bi, hp, 0, 0, 0)),
    ]
    args = [qt, kc, vtc]
    if has_lat:
        in_specs += [
            pl.BlockSpec((None, 2, kl.shape[2], LANES), lambda bi, hp, i: (bi, hp, 0, 0)),
            pl.BlockSpec((None, 2) + vtl.shape[2:], lambda bi, hp, i: (bi, hp, 0, 0, 0)),
        ]
        args += [kl, vtl]
        tk = vtl.shape[4]
        scratch = [
            pltpu.VMEM((2, 1, r), F32), pltpu.VMEM((2, VT_ROWS, r), F32),
            pltpu.VMEM((2, 2, tk, r), F32), pltpu.VMEM((2, 2, 1, r), F32), pltpu.VMEM((2, 2, 1, r), F32),
            pltpu.VMEM((2, 2, tk, r), BF16),
        ]
    if n_maps == 2:
        in_specs += [pl.BlockSpec((1, LANES), lambda bi, hp, i: (0, 0))] * 2
        args += list(extra)
    return pl.pallas_call(
        functools.partial(_attn_kernel, n_maps=n_maps, has_lat=has_lat, lam_init=lam_init),
        grid=(b, nh // 2, l // tq),
        in_specs=in_specs,
        out_specs=pl.BlockSpec((None, tq, LANES), lambda bi, hp, i: (bi, i, hp)),
        out_shape=jax.ShapeDtypeStruct((b, l, nh * 64), BF16),
        scratch_shapes=scratch,
        compiler_params=_cparams(("parallel", "parallel", "arbitrary")),
        name="attn_mla" if n_maps == 1 else "attn_diff",
    )(*args)


def _outproj_kernel(x_ref, a_ref, b_ref, m_ref, d_ref, w_ref, mod_ref, o_ref):
    mix = jnp.concatenate([a_ref[...], b_ref[...], m_ref[...], d_ref[...]], axis=-1)
    y = jnp.dot(mix, w_ref[...], preferred_element_type=F32)
    o_ref[...] = x_ref[...] + mod_ref[2:3, :] * y


def _outproj(x, a, bb, m, dd, w, mod):
    b, l, d = x.shape
    tm = min(512, l)
    slab = pl.BlockSpec((None, tm, 256), lambda bi, i: (bi, i, 0))
    return pl.pallas_call(
        _outproj_kernel,
        grid=(b, l // tm),
        in_specs=[
            pl.BlockSpec((None, tm, d), lambda bi, i: (bi, i, 0)),
            slab, slab, slab, slab,
            pl.BlockSpec(w.shape, lambda bi, i: (0, 0)),
            pl.BlockSpec((None, 6, d), lambda bi, i: (bi, 0, 0)),
        ],
        out_specs=pl.BlockSpec((None, tm, d), lambda bi, i: (bi, i, 0)),
        out_shape=jax.ShapeDtypeStruct((b, l, d), F32),
        compiler_params=_cparams(("parallel", "parallel")),
        name="outproj",
    )(x, a, bb, m, dd, w, mod)


def _top_rows(s, n):
    rows = []
    for r in range(n):
        mx = jnp.max(s, axis=0, keepdims=True)
        rows.append(mx)
        if r + 1 < n:
            s = jnp.where(s == mx, -jnp.inf, s)
    return rows


def _peer_route_kernel(x_ref, g_ref, mod_ref, wq_ref, keys_ref, xnt_ref, e1_ref, th_ref, s2_ref, e2_ref, xn_scr):
    h = pl.program_id(2)

    @pl.when(h == 0)
    def _():
        xn = _modnorm(x_ref[...], g_ref[...], mod_ref, 3, 4)
        xn_scr[...] = xn.astype(BF16)
        xnt_ref[...] = xn.T.astype(BF16)

    q = jnp.dot(xn_scr[...], wq_ref[...], preferred_element_type=F32).astype(BF16)
    half = PEER_DQ // 2
    s1 = _nt(keys_ref[0], q[:, :half])
    s2 = _nt(keys_ref[1], q[:, half:])
    top1 = _top_rows(s1, PEER_TOPK)
    top2 = _top_rows(s2, PEER_TOPK)
    pad_rows = [jnp.full_like(top1[0], -jnp.inf)] * (-len(STAIR) % 8)
    cand = jnp.concatenate([top1[a] + top2[b] for a, b in STAIR] + pad_rows, axis=0)
    thr = _top_rows(cand, PEER_TOPK)[-1]
    m1, m2 = top1[0], top2[0]
    zsum = jnp.sum(jnp.where(cand >= thr, jnp.exp(cand - (m1 + m2)), 0.0), axis=0, keepdims=True)
    e1_ref[...] = jnp.exp(s1 - m1) / zsum
    th_ref[...] = thr - s1
    s2_ref[...] = s2
    e2_ref[...] = jnp.exp(s2 - m2)


def _peer_route(x, g, mod, wq, keys):
    b, l, d = x.shape
    tm = min(256, l)
    nt = l // tm
    n = b * l
    hb = pl.BlockSpec((None, PEER_NKEYS, tm), lambda bi, i, h: (h, 0, bi * nt + i))
    return pl.pallas_call(
        _peer_route_kernel,
        grid=(b, nt, PEER_HEADS),
        in_specs=[
            pl.BlockSpec((None, tm, d), lambda bi, i, h: (bi, i, 0)),
            pl.BlockSpec((1, d), lambda bi, i, h: (0, 0)),
            pl.BlockSpec((None, 6, d), lambda bi, i, h: (bi, 0, 0)),
            pl.BlockSpec((d, PEER_DQ), lambda bi, i, h: (0, h)),
            pl.BlockSpec((None, 2, PEER_NKEYS, PEER_DQ // 2), lambda bi, i, h: (h, 0, 0, 0)),
        ],
        out_specs=[pl.BlockSpec((d, tm), lambda bi, i, h: (0, bi * nt + i)), hb, hb, hb, hb],
        out_shape=[jax.ShapeDtypeStruct((d, n), BF16)] + [jax.ShapeDtypeStruct((PEER_HEADS, PEER_NKEYS, n), F32)] * 4,
        scratch_shapes=[pltpu.VMEM((tm, d), BF16)],
        compiler_params=_cparams(("parallel", "parallel", "arbitrary")),
        name="peer_route",
    )(x, g.reshape(1, d), mod, wq, keys)


GELU_C = math.sqrt(2.0 / math.pi)
PEER_ROWS = 8
PEER_VMEM_LIMIT = 58 * 1024 * 1024
GATE_RB = 32
GATE_IG = 4


def _gelu_tanh(x):
    k = -2.0 * GELU_C * LOG2E
    return x / (1.0 + jnp.exp2(x * (k + (k * 0.044715) * (x * x))))


def _peer_expert_kernel(*refs, final):
    refs = list(refs)
    x_ref, mod_ref, xnt_ref, e1p_ref, thp_ref, e1c_ref, thc_ref, s2_ref, e2_ref, u_ref, vt_ref = refs[:11]
    pos = 11
    if final:
        fg_ref = refs[pos]
        pos += 1
    o_ref, acc_scr = refs[pos:pos + 2]
    a_scr = refs[pos + 2:pos + 4]
    wg_scr = refs[pos + 4:pos + 6]
    be1_scr, bth_scr = refs[pos + 6:pos + 8]
    g = pl.program_id(2)
    last = pl.num_programs(2) - 1
    eb = PEER_ROWS * PEER_NKEYS

    def scores(blk, slot):
        a_scr[slot][...] = jnp.dot(u_ref[blk * eb:(blk + 1) * eb, :], xnt_ref[...], preferred_element_type=F32)

    def gate(slot, e1_ref, th_ref):
        tm = s2_ref.shape[2]
        zero8 = jnp.zeros((8, tm), F32)
        for h in range(PEER_HEADS):
            for ii in range(PEER_ROWS):
                be1_scr[h * PEER_ROWS + ii] = e1_ref[h, pl.ds(ii, 1), :] + zero8
                bth_scr[h * PEER_ROWS + ii] = th_ref[h, pl.ds(ii, 1), :] + zero8
        for c0 in range(0, tm, LANES):
            cs = slice(c0, c0 + LANES)
            for r0 in range(0, PEER_NKEYS, GATE_RB):
                for ig in range(0, PEER_ROWS, GATE_IG):
                    ws = [None] * GATE_IG
                    for h in range(PEER_HEADS):
                        s2 = s2_ref[h, r0:r0 + GATE_RB, cs].reshape(GATE_RB // 8, 8, LANES)
                        e2 = e2_ref[h, r0:r0 + GATE_RB, cs].reshape(GATE_RB // 8, 8, LANES)
                        for k in range(GATE_IG):
                            row = h * PEER_ROWS + ig + k
                            contrib = jnp.where(s2 >= bth_scr[row, :, cs][None], e2 * be1_scr[row, :, cs][None], 0.0)
                            ws[k] = contrib if ws[k] is None else ws[k] + contrib
                    for k in range(GATE_IG):
                        rs = slice((ig + k) * PEER_NKEYS + r0, (ig + k) * PEER_NKEYS + r0 + GATE_RB)
                        wk = ws[k].reshape(GATE_RB, LANES)
                        wg_scr[slot][rs, cs] = (wk * _gelu_tanh(a_scr[slot][rs, cs])).astype(BF16)

    def accum(slot, blk):
        acc_scr[...] += jnp.dot(vt_ref[:, blk * eb:(blk + 1) * eb], wg_scr[slot][...], preferred_element_type=F32)

    @pl.when(g == 0)
    def _():
        acc_scr[...] = jnp.zeros_like(acc_scr)
        scores(1, 1)
        scores(0, 0)
        gate(0, e1c_ref, thc_ref)

    @pl.when((g > 0) & (g < last))
    def _():
        gate(1, e1p_ref, thp_ref)
        accum(0, 0)
        scores(1, 1)
        accum(1, 1)
        scores(0, 0)
        gate(0, e1c_ref, thc_ref)

    @pl.when(g == last)
    def _():
        gate(1, e1p_ref, thp_ref)
        accum(0, 0)
        accum(1, 1)
        y = x_ref[...] + mod_ref[5:6, :] * acc_scr[...].T
        if final:
            ms = jnp.mean(y * y, axis=-1, keepdims=True)
            y = y * lax.rsqrt(ms + NORM_EPS) * fg_ref[...]
        o_ref[...] = y


def _peer_experts(x, mod, xnt, e1, th, s2, e2, u, vt, final_g):
    b, l, d = x.shape
    tm = min(512, l)
    nt = l // tm
    eb = PEER_ROWS * PEER_NKEYS
    n_blocks = u.shape[0] // eb
    assert n_blocks % 2 == 0
    n_steps = n_blocks // 2 + 1
    final = final_g is not None
    tok = lambda bi, i: bi * nt + i
    prev_rows = lambda bi, i, g: (0, jnp.maximum(2 * g - 1, 0), tok(bi, i))
    cur_rows = lambda bi, i, g: (0, jnp.minimum(2 * g, n_blocks - 1), tok(bi, i))
    row_blk = (PEER_HEADS, PEER_ROWS, tm)
    in_specs = [
        pl.BlockSpec((None, tm, d), lambda bi, i, g: (bi, i, 0)),
        pl.BlockSpec((None, 6, d), lambda bi, i, g: (bi, 0, 0)),
        pl.BlockSpec((d, tm), lambda bi, i, g: (0, tok(bi, i))),
        pl.BlockSpec(row_blk, prev_rows),
        pl.BlockSpec(row_blk, prev_rows),
        pl.BlockSpec(row_blk, cur_rows),
        pl.BlockSpec(row_blk, cur_rows),
        pl.BlockSpec((PEER_HEADS, PEER_NKEYS, tm), lambda bi, i, g: (0, 0, tok(bi, i))),
        pl.BlockSpec((PEER_HEADS, PEER_NKEYS, tm), lambda bi, i, g: (0, 0, tok(bi, i))),
        pl.BlockSpec((2 * eb, d), lambda bi, i, g: (jnp.minimum(g, n_steps - 2), 0)),
        pl.BlockSpec((d, 2 * eb), lambda bi, i, g: (0, jnp.maximum(g - 1, 0))),
    ]
    args = [x, mod, xnt, e1, th, e1, th, s2, e2, u, vt]
    if final:
        in_specs.append(pl.BlockSpec((1, d), lambda bi, i, g: (0, 0)))
        args.append(final_g.reshape(1, d))
    return pl.pallas_call(
        functools.partial(_peer_expert_kernel, final=final),
        grid=(b, nt, n_steps),
        in_specs=in_specs,
        out_specs=pl.BlockSpec((None, tm, d), lambda bi, i, g: (bi, i, 0)),
        out_shape=jax.ShapeDtypeStruct((b, l, d), F32),
        scratch_shapes=[pltpu.VMEM((d, tm), F32), pltpu.VMEM((eb, tm), F32), pltpu.VMEM((eb, tm), F32),
                        pltpu.VMEM((eb, tm), BF16), pltpu.VMEM((eb, tm), BF16),
                        pltpu.VMEM((PEER_HEADS * PEER_ROWS, 8, tm), F32), pltpu.VMEM((PEER_HEADS * PEER_ROWS, 8, tm), F32)],
        compiler_params=_cparams(("parallel", "parallel", "arbitrary"), PEER_VMEM_LIMIT),
        name="peer_experts",
    )(*args)


def _peer_pipe_kernel(*refs, final, n_pairs):
    refs = list(refs)
    x_ref, mod_ref, xnt_ref, e1_ref, th_ref, s2_ref, e2_ref, u_ref, vt_ref = refs[:9]
    pos = 9
    if final:
        fg_ref = refs[pos]
        pos += 1
    o_ref, acc_scr = refs[pos:pos + 2]
    a_scr = (refs[pos + 2:pos + 4], refs[pos + 4:pos + 6])
    wg_scr = (refs[pos + 6:pos + 8], refs[pos + 8:pos + 10])
    be1_scr = refs[pos + 10:pos + 12]
    bth_scr = refs[pos + 12:pos + 14]
    g = pl.program_id(2)
    eb = PEER_ROWS * PEER_NKEYS
    tm = s2_ref.shape[2]

    def scores(p):
        for k in range(2):
            a_scr[p][k][...] = jnp.dot(u_ref[k * eb:(k + 1) * eb, :], xnt_ref[...], preferred_element_type=F32)

    def gate_one(a_ref, wg_ref, k):
        zero8 = jnp.zeros((8, tm), F32)
        for h in range(PEER_HEADS):
            for ii in range(PEER_ROWS):
                be1_scr[k][h * PEER_ROWS + ii] = e1_ref[h, pl.ds(k * PEER_ROWS + ii, 1), :] + zero8
                bth_scr[k][h * PEER_ROWS + ii] = th_ref[h, pl.ds(k * PEER_ROWS + ii, 1), :] + zero8
        for c0 in range(0, tm, LANES):
            cs = slice(c0, c0 + LANES)
            for r0 in range(0, PEER_NKEYS, GATE_RB):
                for ig in range(0, PEER_ROWS, GATE_IG):
                    ws = [None] * GATE_IG
                    for h in range(PEER_HEADS):
                        s2 = s2_ref[h, r0:r0 + GATE_RB, cs].reshape(GATE_RB // 8, 8, LANES)
                        e2 = e2_ref[h, r0:r0 + GATE_RB, cs].reshape(GATE_RB // 8, 8, LANES)
                        for j in range(GATE_IG):
                            row = h * PEER_ROWS + ig + j
                            contrib = jnp.where(s2 >= bth_scr[k][row, :, cs][None], e2 * be1_scr[k][row, :, cs][None], 0.0)
                            ws[j] = contrib if ws[j] is None else ws[j] + contrib
                    for j in range(GATE_IG):
                        rs = slice((ig + j) * PEER_NKEYS + r0, (ig + j) * PEER_NKEYS + r0 + GATE_RB)
                        wj = ws[j].reshape(GATE_RB, LANES)
                        wg_ref[rs, cs] = (wj * _gelu_tanh(a_ref[rs, cs])).astype(BF16)

    def gates(p):
        for k in range(2):
            gate_one(a_scr[1 - p][k], wg_scr[p][k], k)

    def accums(p):
        for k in range(2):
            acc_scr[...] += jnp.dot(vt_ref[:, k * eb:(k + 1) * eb], wg_scr[1 - p][k][...], preferred_element_type=F32)

    @pl.when(g == 0)
    def _():
        acc_scr[...] = jnp.zeros_like(acc_scr)
        scores(0)

    @pl.when(g == 1)
    def _():
        scores(1)
        gates(1)

    for p in range(2):
        @pl.when((g >= 2) & (g < n_pairs) & (g % 2 == p))
        def _(p=p):
            scores(p)
            gates(p)
            accums(p)

    @pl.when(g == n_pairs)
    def _():
        gates(n_pairs % 2)
        accums(n_pairs % 2)

    @pl.when(g == n_pairs + 1)
    def _():
        accums((n_pairs + 1) % 2)
        y = x_ref[...] + mod_ref[5:6, :] * acc_scr[...].T
        if final:
            ms = jnp.mean(y * y, axis=-1, keepdims=True)
            y = y * lax.rsqrt(ms + NORM_EPS) * fg_ref[...]
        o_ref[...] = y


def _peer_pipe(x, mod, xnt, e1, th, s2, e2, u, vt, final_g):
    b, l, d = x.shape
    tm = min(512, l)
    nt = l // tm
    eb = PEER_ROWS * PEER_NKEYS
    n_pairs = u.shape[0] // (2 * eb)
    assert n_pairs >= 2
    final = final_g is not None
    tok = lambda bi, i: bi * nt + i
    clamp = lambda v: jnp.clip(v, 0, n_pairs - 1)
    rows = pl.BlockSpec((PEER_HEADS, 2 * PEER_ROWS, tm), lambda bi, i, g: (0, clamp(g - 1), tok(bi, i)))
    keyblk = pl.BlockSpec((PEER_HEADS, PEER_NKEYS, tm), lambda bi, i, g: (0, 0, tok(bi, i)))
    in_specs = [
        pl.BlockSpec((None, tm, d), lambda bi, i, g: (bi, i, 0)),
        pl.BlockSpec((None, 6, d), lambda bi, i, g: (bi, 0, 0)),
        pl.BlockSpec((d, tm), lambda bi, i, g: (0, tok(bi, i))),
        rows, rows, keyblk, keyblk,
        pl.BlockSpec((2 * eb, d), lambda bi, i, g: (clamp(g), 0)),
        pl.BlockSpec((d, 2 * eb), lambda bi, i, g: (0, clamp(g - 2))),
    ]
    args = [x, mod, xnt, e1, th, s2, e2, u, vt]
    if final:
        in_specs.append(pl.BlockSpec((1, d), lambda bi, i, g: (0, 0)))
        args.append(final_g.reshape(1, d))
    bcast = pltpu.VMEM((PEER_HEADS * PEER_ROWS, 8, tm), F32)
    return pl.pallas_call(
        functools.partial(_peer_pipe_kernel, final=final, n_pairs=n_pairs),
        grid=(b, nt, n_pairs + 2),
        in_specs=in_specs,
        out_specs=pl.BlockSpec((None, tm, d), lambda bi, i, g: (bi, i, 0)),
        out_shape=jax.ShapeDtypeStruct((b, l, d), F32),
        scratch_shapes=([pltpu.VMEM((d, tm), F32)] + [pltpu.VMEM((eb, tm), F32)] * 4 + [pltpu.VMEM((eb, tm), BF16)] * 4
                        + [bcast] * 4),
        compiler_params=_cparams(("parallel", "parallel", "arbitrary"), PEER_VMEM_LIMIT),
        name="peer_experts",
    )(*args)


def _pad_cols(a, width):
    return jnp.pad(a, ((0, 0), (0, width - a.shape[1])))


def _layout_w_in(w):
    d = w.shape[0]
    z = lambda n: jnp.zeros((d, n), w.dtype)
    gla, hg, mla, dif = 0, 800, 2080, 2432
    parts = [w[:, gla:gla + 768], w[:, hg:hg + 768],
             w[:, mla:mla + 192], z(64), w[:, mla + 192:mla + 320], z(64), w[:, mla + 320:mla + 352], z(32)]
    for base in (dif, dif + 256, dif + 512):
        for h in range(DIFF_HEADS):
            parts += [w[:, base + 64 * h:base + 64 * (h + 1)], z(64)]
    parts += [w[:, gla + 768:gla + 800], z(224), w[:, hg + 768:hg + 1280]]
    out = jnp.concatenate(parts, axis=1)
    assert out.shape[1] == MAIN_COLS + GATE_COLS
    return out.astype(BF16)


def _layout_mla(w_uq, w_ukv):
    dq = MLA_NOPE + MLA_ROPE
    wq = jnp.concatenate([_pad_cols(w_uq[:, dq * h:dq * (h + 1)], LANES) for h in range(MLA_HEADS)], axis=1)
    wq = jnp.pad(wq, ((0, 256 - MLA_Q_RANK), (0, 0)))
    per = MLA_NOPE + MLA_DV
    wk = jnp.concatenate([_pad_cols(w_ukv[:, per * h:per * h + MLA_NOPE], LANES) for h in range(MLA_HEADS)], axis=1)
    wv = jnp.concatenate([_pad_cols(w_ukv[:, per * h + MLA_NOPE:per * (h + 1)], LANES) for h in range(MLA_HEADS)], axis=1)
    return wq.astype(BF16), jnp.concatenate([wk, wv], axis=1).astype(BF16)


def kernel(x, c, ctx, c_ctx, ada_w, ada_b, norm_mix_g, norm_ffn_g, w_in, w_out, gla_gate_w, gla_gate_b, gla_norm_g, hgrn_lb_raw, hgrn_norm_g, mla_q_norm_g, mla_kv_norm_g, mla_w_uq, mla_w_ukv, diff_lambda, diff_norm_g, peer_wq, peer_keys, peer_u, peer_v, final_norm_g):
    b, l, d = x.shape
    lc = ctx.shape[1]
    depth = ada_w.shape[0]
    assert lc <= ATT_TK, "context keys are consumed as one attention step"

    rows = -(-(b + 1) // 8) * 8
    cpad = jnp.zeros((rows, d), F32).at[:b].set(c).at[b].set(c_ctx)
    mod_all = _adaln(cpad, ada_w, ada_b)

    lb_sm = jax.nn.softmax(hgrn_lb_raw.astype(F32), axis=0)
    lb_all = jnp.cumsum(lb_sm, axis=0) - lb_sm[0]

    lat_mla_tab = _rope_tables(l, 64, 32, True)
    ctx_mla_tab = _rope_tables(lc, 64, 32, False)
    lat_dif_tab = _rope_tables(l, 0, 64, True)
    ctx_dif_tab = _rope_tables(lc, 0, 64, False)

    xc = ctx
    for li in range(depth):
        need_ctx = li < depth - 1
        lam_init = 0.8 - 0.6 * math.exp(-0.3 * li)
        mod_l = mod_all[li, :b].reshape(b, 6, d)
        mod_c = jnp.broadcast_to(mod_all[li, b].reshape(1, 6, d), (b, 6, d))

        w_perm = _layout_w_in(w_in[li])
        w_o = w_out[li].astype(BF16)
        wg = jnp.zeros((2, 256, 128), F32)
        wg = wg.at[0, 0:GLA_GATE_RANK].set(gla_gate_w[li, 0]).at[1, GLA_GATE_RANK:2 * GLA_GATE_RANK].set(gla_gate_w[li, 1])
        wg = wg.astype(BF16)
        bg = gla_gate_b[li].reshape(2, 1, 128)
        gla_gn = jnp.tile(gla_norm_g[li], GLA_HEADS).reshape(1, 256)
        hg_gn = jnp.tile(hgrn_norm_g[li], HG_HEADS).reshape(1, 256)
        lb = lb_all[li]
        hg_p = jnp.zeros((8, 256), F32).at[0].set(jnp.log(jnp.maximum(lb, LB_FLOOR))).at[1].set(jnp.log1p(-lb)).at[2].set(1.0 - lb)
        hg_dummy = jnp.zeros((8, 128), F32)
        gq = jnp.pad(mla_q_norm_g[li], (0, 256 - MLA_Q_RANK)).reshape(1, 256)
        gkv = mla_kv_norm_g[li].reshape(1, MLA_KV_RANK)
        wq_mla, wkv_mla = _layout_mla(mla_w_uq[li], mla_w_ukv[li])
        lq1, lk1, lq2, lk2 = diff_lambda[li, 0], diff_lambda[li, 1], diff_lambda[li, 2], diff_lambda[li, 3]
        lam = (jnp.exp(jnp.sum(lq1 * lk1).astype(F32)) - jnp.exp(jnp.sum(lq2 * lk2).astype(F32)) + lam_init)
        lam_row = jnp.full((1, LANES), lam, F32)
        dif_gn = jnp.tile(diff_norm_g[li], 2).reshape(1, LANES)
        wq_peer = peer_wq[li].astype(BF16)
        keys = peer_keys[li].astype(BF16)
        u_bf = peer_u[li].astype(BF16)
        vt_bf = peer_v[li].astype(BF16).T

        main_c, gate_c = _inproj(xc, norm_mix_g[li], mod_c, w_perm)
        main_l, gate_l = _inproj(x, norm_mix_g[li], mod_l, w_perm)

        mixes_c, mixes_l = [], []
        for mixer, pa_f, pb_f, pa_b, pb_b, gn, dk in (
            ("gla", wg[0], bg[0], wg[1], bg[1], gla_gn, 128),
            ("hg", hg_p, hg_dummy, hg_p, hg_dummy, hg_gn, 256),
        ):
            zero = jnp.zeros((b, 256, dk), F32)
            o_cf, s_f = _scan(main_c, gate_c, pa_f, pb_f, gn, zero, None, mixer=mixer, reverse=False, chunk=64)
            mix_c, s_b = _scan(main_c, gate_c, pa_b, pb_b, gn, zero, o_cf, mixer=mixer, reverse=True, chunk=64)
            o_lf, _ = _scan(main_l, gate_l, pa_f, pb_f, gn, s_f, None, mixer=mixer, reverse=False, chunk=64)
            mix_l, _ = _scan(main_l, gate_l, pa_b, pb_b, gn, s_b, o_lf, mixer=mixer, reverse=True, chunk=64)
            mixes_c.append(mix_c)
            mixes_l.append(mix_l)

        qm_c, km_c, vm_c = _mla_prep(main_c, gq, gkv, wq_mla, wkv_mla, ctx_mla_tab)
        qm_l, km_l, vm_l = _mla_prep(main_l, gq, gkv, wq_mla, wkv_mla, lat_mla_tab)
        mla_l = _attention(qm_l, km_c, vm_c, km_l, vm_l, None, n_maps=1, lam_init=lam_init)
        qd_c, kd_c, vd_c = _diff_prep(main_c, ctx_dif_tab)
        qd_l, kd_l, vd_l = _diff_prep(main_l, lat_dif_tab)
        dif_l = _attention(qd_l, kd_c, vd_c, kd_l, vd_l, (lam_row, dif_gn), n_maps=2, lam_init=lam_init)

        x = _outproj(x, mixes_l[0], mixes_l[1], mla_l, dif_l, w_o, mod_l)
        rt = _peer_route(x, norm_ffn_g[li], mod_l, wq_peer, keys)
        x = _peer_pipe(x, mod_l, *rt, u_bf, vt_bf, None if need_ctx else final_norm_g)

        if need_ctx:
            mla_c = _attention(qm_c, km_c, vm_c, None, None, None, n_maps=1, lam_init=lam_init)
            dif_c = _attention(qd_c, kd_c, vd_c, None, None, (lam_row, dif_gn), n_maps=2, lam_init=lam_init)
            xc = _outproj(xc, mixes_c[0], mixes_c[1], mla_c, dif_c, w_o, mod_c)
            rtc = _peer_route(xc, norm_ffn_g[li], mod_c, wq_peer, keys)
            xc = _peer_pipe(xc, mod_c, *rtc, u_bf, vt_bf, None)
    return x
```

```python
import functools
import math

import jax
import jax.numpy as jnp
import numpy as np
from jax import lax
from jax.experimental import pallas as pl
from jax.experimental.pallas import tpu as pltpu

F32 = jnp.float32
BF16 = jnp.bfloat16

NORM_EPS = 1e-6
ROPE_BASE = 10000.0
GRID_W = 64
LB_FLOOR = 1e-30

GLA_HEADS, GLA_DK, GLA_DV, GLA_GATE_RANK, GLA_GATE_NORM = 4, 32, 64, 16, 16.0
HG_HEADS, HG_DIM = 4, 64
MLA_HEADS, MLA_Q_RANK, MLA_KV_RANK, MLA_NOPE, MLA_ROPE, MLA_DV = 4, 192, 128, 64, 32, 64
DIFF_HEADS, DIFF_DQK, DIFF_DV = 4, 32, 64
PEER_HEADS, PEER_NKEYS, PEER_DQ, PEER_TOPK = 8, 128, 256, 16

LANES = 128
VMEM_LIMIT = 48 * 1024 * 1024
EXP_CAP = 80.0
NEG_BIG = -1e30
ATT_TK = 512
LOG2E = 1.4426950408889634
ATT_R = 512
VT_ROWS = 80

COL_GLA, COL_HG, COL_MLA, COL_DQ, COL_DK, COL_DV, MAIN_COLS = 0, 768, 1536, 2048, 2560, 3072, 3584
GATE_COLS = 768

STAIR = [(a, b) for a in range(PEER_TOPK) for b in range(PEER_TOPK) if (a + 1) * (b + 1) <= PEER_TOPK]


def _cparams(sem, vmem=VMEM_LIMIT):
    return pltpu.CompilerParams(dimension_semantics=sem, vmem_limit_bytes=vmem)


def _sigmoid(x):
    return 1.0 / (1.0 + jnp.exp(-x))


def _logsig(x):
    return jnp.minimum(x, 0.0) - jnp.log(1.0 + jnp.exp(-jnp.abs(x)))


def _nt(a, b):
    return lax.dot_general(a, b, (((1,), (1,)), ((), ())), preferred_element_type=F32)


def _tn(a, b):
    return lax.dot_general(a, b, (((0,), (0,)), ((), ())), preferred_element_type=F32)


def _adaln_kernel(c_ref, w_ref, b_ref, o_ref):
    c = c_ref[...]
    s = c * _sigmoid(c)
    o_ref[...] = jnp.dot(s.astype(BF16), w_ref[...].astype(BF16), preferred_element_type=F32) + b_ref[...]


def _adaln(cpad, ada_w, ada_b):
    depth, d, n6 = ada_w.shape
    rows = cpad.shape[0]
    tn = 512
    return pl.pallas_call(
        _adaln_kernel,
        grid=(depth, n6 // tn),
        in_specs=[
            pl.BlockSpec((rows, d), lambda l, j: (0, 0)),
            pl.BlockSpec((None, d, tn), lambda l, j: (l, 0, j)),
            pl.BlockSpec((None, 1, tn), lambda l, j: (l, 0, j)),
        ],
        out_specs=pl.BlockSpec((None, rows, tn), lambda l, j: (l, 0, j)),
        out_shape=jax.ShapeDtypeStruct((depth, rows, n6), F32),
        compiler_params=_cparams(("parallel", "parallel")),
        name="adaln",
    )(cpad, ada_w, ada_b.reshape(depth, 1, n6))


def _modnorm(x, g, mod_ref, shift_row, scale_row):
    ms = jnp.mean(x * x, axis=-1, keepdims=True)
    h = x * lax.rsqrt(ms + NORM_EPS) * g
    return h * (1.0 + mod_ref[scale_row:scale_row + 1, :]) + mod_ref[shift_row:shift_row + 1, :]


def _inproj_kernel(x_ref, g_ref, mod_ref, w_ref, main_ref, gate_ref):
    h = _modnorm(x_ref[...], g_ref[...], mod_ref, 0, 1)
    y = jnp.dot(h.astype(BF16), w_ref[...], preferred_element_type=F32)
    main_ref[...] = y[:, :MAIN_COLS].astype(BF16)
    gate_ref[...] = y[:, MAIN_COLS:]


def _inproj(x, g, mod, w_perm):
    b, l, d = x.shape
    tm = min(256, l)
    nc = w_perm.shape[1]
    return pl.pallas_call(
        _inproj_kernel,
        grid=(b, l // tm),
        in_specs=[
            pl.BlockSpec((None, tm, d), lambda bi, i: (bi, i, 0)),
            pl.BlockSpec((1, d), lambda bi, i: (0, 0)),
            pl.BlockSpec((None, 6, d), lambda bi, i: (bi, 0, 0)),
            pl.BlockSpec((d, nc), lambda bi, i: (0, 0)),
        ],
        out_specs=[
            pl.BlockSpec((None, tm, MAIN_COLS), lambda bi, i: (bi, i, 0)),
            pl.BlockSpec((None, tm, GATE_COLS), lambda bi, i: (bi, i, 0)),
        ],
        out_shape=[
            jax.ShapeDtypeStruct((b, l, MAIN_COLS), BF16),
            jax.ShapeDtypeStruct((b, l, GATE_COLS), F32),
        ],
        compiler_params=_cparams(("parallel", "parallel")),
        name="inproj",
    )(x, g.reshape(1, d), mod, w_perm)


def _scan_kernel(*refs, mixer, reverse, finish, chunk):
    if finish:
        main_ref, gate_ref, pa_ref, pb_ref, gn_ref, s0_ref, oprev_ref, o_ref, sfin_ref, st_scr = refs
    else:
        main_ref, gate_ref, pa_ref, pb_ref, gn_ref, s0_ref, o_ref, sfin_ref, st_scr = refs
        oprev_ref = None
    i = pl.program_id(1)

    @pl.when(i == 0)
    def _():
        st_scr[...] = s0_ref[...]

    main = main_ref[...]
    t = main.shape[0]
    c = chunk
    nc = t // c
    heads = 4
    dv = 256
    v = main[:, 256:512]
    gcol = main[:, 512:768].astype(F32)
    if mixer == "gla":
        dk = 128
        q = main[:, 0:128].astype(F32) * (GLA_DK ** -0.5)
        k = main[:, 128:256].astype(F32)
        z = jnp.dot(gate_ref[...].astype(BF16), pa_ref[...], preferred_element_type=F32) + pb_ref[...]
        g = _logsig(z) * (1.0 / GLA_GATE_NORM)
    else:
        dk = 256
        qq = main[:, 0:256].astype(F32)
        q = qq * _sigmoid(qq)
        z = gate_ref[...]
        la = pa_ref[0:1, :]
        lb = pa_ref[1:2, :] + _logsig(z)
        g = jnp.maximum(la, lb) + jnp.log(1.0 + jnp.exp(-jnp.abs(la - lb)))
        k = pa_ref[2:3, :] * _sigmoid(-z)
    hk = dk // heads
    hv = dv // heads

    rt = lax.broadcasted_iota(jnp.int32, (t, t), 0)
    ct = lax.broadcasted_iota(jnp.int32, (t, t), 1)
    same = (rt // c) == (ct // c)
    tri = jnp.where(same & ((ct >= rt) if reverse else (ct <= rt)), 1.0, 0.0).astype(BF16)
    g1 = g.astype(BF16)
    r1 = g - g1.astype(F32)
    g2 = r1.astype(BF16)
    g3 = (r1 - g2.astype(F32)).astype(BF16)
    bcum = (jnp.dot(tri, g1, preferred_element_type=F32) + jnp.dot(tri, g2, preferred_element_type=F32)
            + jnp.dot(tri, g3, preferred_element_type=F32))

    b3 = bcum.reshape(nc, c, dk)
    q3 = q.reshape(nc, c, dk)
    k3 = k.reshape(nc, c, dk)
    if reverse:
        bmid = b3[:, c // 2:c // 2 + 1, :]
        bend = b3[:, 0:1, :]
    else:
        bmid = b3[:, c // 2 - 1:c // 2, :]
        bend = b3[:, c - 1:c, :]
    eb = b3 - bmid
    qa = (q3 * jnp.exp(jnp.minimum(eb, EXP_CAP))).reshape(t, dk).astype(BF16)
    ka = (k3 * jnp.exp(jnp.minimum(-eb, EXP_CAP))).reshape(t, dk).astype(BF16)
    qi = (q3 * jnp.exp(b3)).reshape(t, dk).astype(BF16)
    kb = (k3 * jnp.exp(bend - b3)).reshape(t, dk).astype(BF16)
    dec = jnp.exp(bend)

    lane_k = lax.broadcasted_iota(jnp.int32, (1, dk), 1) // hk
    lane_v = lax.broadcasted_iota(jnp.int32, (1, dv), 1) // hv
    rr = lax.broadcasted_iota(jnp.int32, (c, heads * c), 0)
    cc = lax.broadcasted_iota(jnp.int32, (c, heads * c), 1) % c
    causal = (cc >= rr) if reverse else (cc <= rr)
    bd = (lax.broadcasted_iota(jnp.int32, (dv, dk), 0) // hv) == (lax.broadcasted_iota(jnp.int32, (dv, dk), 1) // hk)

    o_intra = []
    upd = []
    for ci in range(nc):
        sl = slice(ci * c, (ci + 1) * c)
        ka_c = ka[sl]
        v_c = v[sl]
        kst = jnp.concatenate([jnp.where(lane_k == h, ka_c, 0) for h in range(heads)], axis=0)
        vst = jnp.concatenate([jnp.where(lane_v == h, v_c, 0) for h in range(heads)], axis=0)
        sw = _nt(qa[sl], kst)
        p = jnp.where(causal, sw, 0.0).astype(BF16)
        o_intra.append(jnp.dot(p, vst, preferred_element_type=F32))
        upd.append(jnp.where(bd, _tn(v_c, kb[sl]), 0.0))

    outs = [None] * nc
    order = range(nc - 1, -1, -1) if reverse else range(nc)
    for ci in order:
        sl = slice(ci * c, (ci + 1) * c)
        st = st_scr[...]
        outs[ci] = o_intra[ci] + _nt(qi[sl], st.astype(BF16))
        st_scr[...] = st * dec[ci] + upd[ci]
    o = jnp.concatenate(outs, axis=0)

    @pl.when(i == pl.num_programs(1) - 1)
    def _():
        sfin_ref[...] = st_scr[...]

    if not finish:
        o_ref[...] = o
    else:
        ot = o + oprev_ref[...]
        sq = ot * ot
        jv = ((lax.broadcasted_iota(jnp.int32, (dv, dv), 0) // hv)
              == (lax.broadcasted_iota(jnp.int32, (dv, dv), 1) // hv))
        jm = jnp.where(jv, 1.0, 0.0).astype(BF16)
        s1 = sq.astype(BF16)
        s2 = (sq - s1.astype(F32)).astype(BF16)
        ms = (jnp.dot(s1, jm, preferred_element_type=F32) + jnp.dot(s2, jm, preferred_element_type=F32)) * (1.0 / hv)
        y = ot * lax.rsqrt(ms + NORM_EPS) * gn_ref[...]
        o_ref[...] = (y * (gcol * _sigmoid(gcol))).astype(BF16)


def _scan(main, gate, pa, pb, gn, s0, oprev, *, mixer, reverse, chunk):
    b, l, _ = main.shape
    t = min(256, l)
    nb = l // t
    finish = oprev is not None
    dk = 128 if mixer == "gla" else 256
    dv = 256
    main_blk = 0 if mixer == "gla" else 1
    gate_blk = 0 if mixer == "gla" else (2 if reverse else 1)

    def tok(bi, i):
        return (nb - 1 - i) if reverse else i

    in_specs = [
        pl.BlockSpec((None, t, 768), lambda bi, i: (bi, tok(bi, i), main_blk)),
        pl.BlockSpec((None, t, 256), lambda bi, i: (bi, tok(bi, i), gate_blk)),
        pl.BlockSpec(pa.shape, lambda bi, i: (0, 0)),
        pl.BlockSpec(pb.shape, lambda bi, i: (0, 0)),
        pl.BlockSpec((1, dv), lambda bi, i: (0, 0)),
        pl.BlockSpec((None, dv, dk), lambda bi, i: (bi, 0, 0)),
    ]
    args = [main, gate, pa, pb, gn, s0]
    if finish:
        in_specs.append(pl.BlockSpec((None, t, dv), lambda bi, i: (bi, tok(bi, i), 0)))
        args.append(oprev)
    return pl.pallas_call(
        functools.partial(_scan_kernel, mixer=mixer, reverse=reverse, finish=finish, chunk=chunk),
        grid=(b, nb),
        in_specs=in_specs,
        out_specs=[
            pl.BlockSpec((None, t, dv), lambda bi, i: (bi, tok(bi, i), 0)),
            pl.BlockSpec((None, dv, dk), lambda bi, i: (bi, 0, 0)),
        ],
        out_shape=[
            jax.ShapeDtypeStruct((b, l, dv), BF16 if finish else F32),
            jax.ShapeDtypeStruct((b, dv, dk), F32),
        ],
        scratch_shapes=[pltpu.VMEM((dv, dk), F32)],
        compiler_params=_cparams(("parallel", "arbitrary")),
        name=f"scan_{mixer}_{'bwd' if reverse else 'fwd'}",
    )(*args)


def _rope(x, cos, sin_lo, sin_hi):
    return x * cos + pltpu.roll(x, LANES - 8, 1) * sin_lo + pltpu.roll(x, 8, 1) * sin_hi


def _rope_tables(l, first_lane, n_lanes, rotate):
    lane = np.arange(LANES)
    d = (lane - first_lane) % 32
    active = (lane >= first_lane) & (lane < first_lane + n_lanes)
    freqs = ROPE_BASE ** (-(d % 8).astype(np.float32) / 8.0)
    tt = jnp.arange(l, dtype=jnp.int32)
    rows = (tt // GRID_W).astype(F32)[:, None]
    cols = (tt % GRID_W).astype(F32)[:, None]
    pos = jnp.where(jnp.asarray(d < 16)[None, :], rows, cols)
    ang = pos * jnp.asarray(freqs, F32)[None, :]
    act = jnp.asarray(active)[None, :] & rotate
    lo = jnp.asarray((d % 16) < 8)[None, :]
    cos = jnp.where(act, jnp.cos(ang), 1.0)
    sin = jnp.where(act, jnp.sin(ang), 0.0)
    return cos, jnp.where(lo, -sin, 0.0), jnp.where(lo, 0.0, sin)


def _vt_block(vgrp):
    lane = lax.broadcasted_iota(jnp.int32, (1, LANES), 1)
    return jnp.where(lane == 64, 1.0, vgrp).T[:VT_ROWS].astype(BF16)


def _mla_prep_kernel(slab_ref, gq_ref, gkv_ref, wq_ref, wkv_ref, cos_ref, slo_ref, shi_ref, qt_ref, k_ref, vt_ref):
    slab = slab_ref[...]
    cos, slo, shi = cos_ref[...], slo_ref[...], shi_ref[...]
    cq = slab[:, 0:256].astype(F32)
    qn = cq * lax.rsqrt(jnp.sum(cq * cq, axis=-1, keepdims=True) * (1.0 / MLA_Q_RANK) + NORM_EPS) * gq_ref[...]
    qall = jnp.dot(qn.astype(BF16), wq_ref[...], preferred_element_type=F32)
    ckv = slab[:, 256:384].astype(F32)
    kvn = ckv * lax.rsqrt(jnp.mean(ckv * ckv, axis=-1, keepdims=True) + NORM_EPS) * gkv_ref[...]
    kvall = jnp.dot(kvn.astype(BF16), wkv_ref[...], preferred_element_type=F32)
    kr = _rope(slab[:, 384:512].astype(F32), cos, slo, shi)
    scale = (MLA_NOPE + MLA_ROPE) ** -0.5 * LOG2E
    for h in range(MLA_HEADS):
        qh = _rope(qall[:, h * LANES:(h + 1) * LANES], cos, slo, shi)
        qt_ref[h] = (qh * scale).T.astype(BF16)
        k_ref[h] = (kvall[:, h * LANES:(h + 1) * LANES] + kr).astype(BF16)
        vt_ref[h] = _vt_block(kvall[:, (MLA_HEADS + h) * LANES:(MLA_HEADS + h + 1) * LANES])


def _attn_prep_specs(b, l, tm, heads):
    qt = pl.BlockSpec((None, heads, LANES, tm), lambda bi, i: (bi, 0, 0, i))
    kk = pl.BlockSpec((None, heads, tm, LANES), lambda bi, i: (bi, 0, i, 0))
    vt = pl.BlockSpec((None, heads, None, VT_ROWS, tm), lambda bi, i: (bi, 0, i, 0, 0))
    shapes = [
        jax.ShapeDtypeStruct((b, heads, LANES, l), BF16),
        jax.ShapeDtypeStruct((b, heads, l, LANES), BF16),
        jax.ShapeDtypeStruct((b, heads, l // tm, VT_ROWS, tm), BF16),
    ]
    return [qt, kk, vt], shapes


def _mla_prep(main, gq, gkv, wq, wkv, tables):
    b, l, _ = main.shape
    tm = min(ATT_TK, l)
    cos, slo, shi = tables
    tab = pl.BlockSpec((tm, LANES), lambda bi, i: (i, 0))
    out_specs, out_shape = _attn_prep_specs(b, l, tm, MLA_HEADS)
    return pl.pallas_call(
        _mla_prep_kernel,
        grid=(b, l // tm),
        in_specs=[
            pl.BlockSpec((None, tm, 512), lambda bi, i: (bi, i, COL_MLA // 512)),
            pl.BlockSpec(gq.shape, lambda bi, i: (0, 0)),
            pl.BlockSpec(gkv.shape, lambda bi, i: (0, 0)),
            pl.BlockSpec(wq.shape, lambda bi, i: (0, 0)),
            pl.BlockSpec(wkv.shape, lambda bi, i: (0, 0)),
            tab, tab, tab,
        ],
        out_specs=out_specs,
        out_shape=out_shape,
        compiler_params=_cparams(("parallel", "parallel")),
        name="mla_prep",
    )(main, gq, gkv, wq, wkv, cos, slo, shi)


def _diff_prep_kernel(q_in, k_in, v_in, cos_ref, slo_ref, shi_ref, qt_ref, k_ref, vt_ref):
    cos, slo, shi = cos_ref[...], slo_ref[...], shi_ref[...]
    qs = q_in[...]
    ks = k_in[...]
    vs = v_in[...]
    scale = DIFF_DQK ** -0.5 * LOG2E
    for h in range(DIFF_HEADS):
        sl = slice(h * LANES, (h + 1) * LANES)
        qt_ref[h] = (_rope(qs[:, sl].astype(F32), cos, slo, shi) * scale).T.astype(BF16)
        k_ref[h] = _rope(ks[:, sl].astype(F32), cos, slo, shi).astype(BF16)
        vt_ref[h] = _vt_block(vs[:, sl].astype(F32))


def _diff_prep(main, tables):
    b, l, _ = main.shape
    tm = min(ATT_TK, l)
    cos, slo, shi = tables
    tab = pl.BlockSpec((tm, LANES), lambda bi, i: (i, 0))
    out_specs, out_shape = _attn_prep_specs(b, l, tm, DIFF_HEADS)
    return pl.pallas_call(
        _diff_prep_kernel,
        grid=(b, l // tm),
        in_specs=[
            pl.BlockSpec((None, tm, 512), lambda bi, i: (bi, i, COL_DQ // 512)),
            pl.BlockSpec((None, tm, 512), lambda bi, i: (bi, i, COL_DK // 512)),
            pl.BlockSpec((None, tm, 512), lambda bi, i: (bi, i, COL_DV // 512)),
            tab, tab, tab,
        ],
        out_specs=out_specs,
        out_shape=out_shape,
        compiler_params=_cparams(("parallel", "parallel")),
        name="diff_prep",
    )(main, main, main, cos, slo, shi)


def _attn_kernel(*refs, n_maps, has_lat, lam_init):
    refs = list(refs)
    qt_ref, kc_ref, vtc_ref = refs[:3]
    pos = 3
    if has_lat:
        kl_ref, vtl_ref = refs[pos:pos + 2]
        pos += 2
    if n_maps == 2:
        lam_ref, gn_ref = refs[pos:pos + 2]
        pos += 2
    o_ref = refs[pos]
    if has_lat:
        m_scr, acc_scr, s_scr, cm_scr, al_scr, p_scr = refs[pos + 1:pos + 7]
    tq = qt_ref.shape[2]
    feat = lax.broadcasted_iota(jnp.int32, (LANES, 1), 0)

    qts = []
    for hh in range(2):
        qt = qt_ref[hh]
        if n_maps == 2:
            zero = jnp.zeros_like(qt)
            qt = jnp.concatenate(
                [jnp.where((feat >= DIFF_DQK * mi) & (feat < DIFF_DQK * (mi + 1)), qt, zero) for mi in range(2)], axis=1)
        qts.append(qt)

    accs = []
    for hh in range(2):
        st = jnp.dot(kc_ref[hh], qts[hh], preferred_element_type=F32)
        m0 = jnp.max(st, axis=0, keepdims=True)
        acc0 = jnp.dot(vtc_ref[hh, 0], jnp.exp2(st - m0).astype(BF16), preferred_element_type=F32)
        accs.append(acc0)
        if has_lat:
            m_scr[hh] = m0
            acc_scr[hh] = acc0

    if has_lat:
        n = vtl_ref.shape[1]
        tk = vtl_ref.shape[3]

        def scores(j, slot):
            off = pl.multiple_of(j * tk, tk)
            for hh in range(2):
                st = jnp.dot(kl_ref[hh, pl.ds(off, tk), :], qts[hh], preferred_element_type=F32)
                s_scr[hh, slot] = st
                cm_scr[hh, slot] = jnp.max(st, axis=0, keepdims=True)

        def numer(slot):
            for hh in range(2):
                m_old = m_scr[hh]
                m_new = jnp.maximum(m_old, cm_scr[hh, slot])
                al_scr[hh, slot] = jnp.exp2(m_old - m_new)
                p_scr[hh, slot] = jnp.exp2(s_scr[hh, slot] - m_new).astype(BF16)
                m_scr[hh] = m_new

        def values(j, slot):
            for hh in range(2):
                acc_scr[hh] = (al_scr[hh, slot] * acc_scr[hh]
                               + jnp.dot(vtl_ref[hh, j], p_scr[hh, slot], preferred_element_type=F32))

        first = n % 2
        if first:
            scores(0, 0)
            numer(0)
            values(0, 0)
        if n > first:
            for hh in range(2):
                p_scr[hh, 1] = jnp.zeros(p_scr.shape[2:], BF16)
                al_scr[hh, 1] = jnp.ones(al_scr.shape[2:], F32)
            scores(first, 0)

            def body(t, carry):
                j = first + 2 * t
                scores(j + 1, 1)
                numer(0)
                values(jnp.maximum(j - 1, first), 1)
                scores(jnp.minimum(j + 2, n - 1), 0)
                numer(1)
                values(j, 0)
                return carry

            lax.fori_loop(0, (n - first) // 2, body, 0)
            values(n - 1, 1)
        accs = [acc_scr[0], acc_scr[1]]

    outs = []
    for hh in range(2):
        acc = accs[hh]
        ot = acc[0:64, :] / acc[64:65, :]
        if n_maps == 2:
            oh = ot[:, :tq] - lam_ref[:, 0:1] * ot[:, tq:]
            ms = jnp.mean(oh * oh, axis=0, keepdims=True)
            ot = oh * lax.rsqrt(ms + NORM_EPS)
        outs.append(ot)
    o = jnp.concatenate(outs, axis=0).T
    if n_maps == 2:
        o = o * gn_ref[...] * (1.0 - lam_init)
    o_ref[...] = o.astype(BF16)


def _attention(qt, kc, vtc, kl, vtl, extra, *, n_maps, lam_init):
    b, nh, _, l = qt.shape
    lc = kc.shape[2]
    tq = min(ATT_R // n_maps, l)
    r = n_maps * tq
    has_lat = kl is not None
    scratch = []
    in_specs = [
        pl.BlockSpec((None, 2, LANES, tq), lambda bi, hp, i: (bi, hp, 0, i)),
        pl.BlockSpec((None, 2, lc, LANES), lambda bi, hp, i: (bi, hp, 0, 0)),
        pl.BlockSpec((None, 2) + vtc.shape[2:], lambda bi, hp, i: (bi, hp, 0, 0, 0)),
    ]
    args = [qt, kc, vtc]
    if has_lat:
        in_specs += [
            pl.BlockSpec((None, 2, kl.shape[2], LANES), lambda bi, hp, i: (bi, hp, 0, 0)),
            pl.BlockSpec((None, 2) + vtl.shape[2:], lambda bi, hp, i: (bi, hp, 0, 0, 0)),
        ]
        args += [kl, vtl]
        tk = vtl.shape[4]
        scratch = [
            pltpu.VMEM((2, 1, r), F32), pltpu.VMEM((2, VT_ROWS, r), F32),
            pltpu.VMEM((2, 2, tk, r), F32), pltpu.VMEM((2, 2, 1, r), F32), pltpu.VMEM((2, 2, 1, r), F32),
            pltpu.VMEM((2, 2, tk, r), BF16),
        ]
    if n_maps == 2:
        in_specs += [pl.BlockSpec((1, LANES), lambda bi, hp, i: (0, 0))] * 2
        args += list(extra)
    return pl.pallas_call(
        functools.partial(_attn_kernel, n_maps=n_maps, has_lat=has_lat, lam_init=lam_init),
        grid=(b, nh // 2, l // tq),
        in_specs=in_specs,
        out_specs=pl.BlockSpec((None, tq, LANES), lambda bi, hp, i: (bi, i, hp)),
        out_shape=jax.ShapeDtypeStruct((b, l, nh * 64), BF16),
        scratch_shapes=scratch,
        compiler_params=_cparams(("parallel", "parallel", "arbitrary")),
        name="attn_mla" if n_maps == 1 else "attn_diff",
    )(*args)


def _outproj_kernel(x_ref, a_ref, b_ref, m_ref, d_ref, w_ref, mod_ref, o_ref):
    mix = jnp.concatenate([a_ref[...], b_ref[...], m_ref[...], d_ref[...]], axis=-1)
    y = jnp.dot(mix, w_ref[...], preferred_element_type=F32)
    o_ref[...] = x_ref[...] + mod_ref[2:3, :] * y


def _outproj(x, a, bb, m, dd, w, mod):
    b, l, d = x.shape
    tm = min(512, l)
    slab = pl.BlockSpec((None, tm, 256), lambda bi, i: (bi, i, 0))
    return pl.pallas_call(
        _outproj_kernel,
        grid=(b, l // tm),
        in_specs=[
            pl.BlockSpec((None, tm, d), lambda bi, i: (bi, i, 0)),
            slab, slab, slab, slab,
            pl.BlockSpec(w.shape, lambda bi, i: (0, 0)),
            pl.BlockSpec((None, 6, d), lambda bi, i: (bi, 0, 0)),
        ],
        out_specs=pl.BlockSpec((None, tm, d), lambda bi, i: (bi, i, 0)),
        out_shape=jax.ShapeDtypeStruct((b, l, d), F32),
        compiler_params=_cparams(("parallel", "parallel")),
        name="outproj",
    )(x, a, bb, m, dd, w, mod)


def _top_rows(s, n):
    rows = []
    for r in range(n):
        mx = jnp.max(s, axis=0, keepdims=True)
        rows.append(mx)
        if r + 1 < n:
            s = jnp.where(s == mx, -jnp.inf, s)
    return rows


def _oddeven_merge(lo, hi, r):
    step = r * 2
    if step < hi - lo:
        yield from _oddeven_merge(lo, hi, step)
        yield from _oddeven_merge(lo + r, hi, step)
        yield from [(i, i + r) for i in range(lo + r, hi - r, step)]
    else:
        yield (lo, lo + r)


def _oddeven_sort(lo, hi):
    if hi - lo >= 1:
        mid = lo + (hi - lo) // 2
        yield from _oddeven_sort(lo, mid)
        yield from _oddeven_sort(mid + 1, hi)
        yield from _oddeven_merge(lo, hi, 1)


SORT16 = list(_oddeven_sort(0, PEER_TOPK - 1))


def _colmax8(x):
    for sh in (1, 2, 4):
        x = jnp.maximum(x, pltpu.roll(x, sh, 0))
    return x


def _top16_rows(s):
    ls = [s[8 * r:8 * r + 8, :] for r in range(PEER_TOPK)]
    for i, j in SORT16:
        ls[i], ls[j] = jnp.maximum(ls[i], ls[j]), jnp.minimum(ls[i], ls[j])
    rows = []
    for r in range(PEER_TOPK):
        m = _colmax8(ls[0])
        rows.append(m[0:1, :])
        keep = PEER_TOPK - r - 1
        hit = ls[0] == m
        ls = [jnp.where(hit, ls[d + 1], ls[d]) for d in range(keep)]
    return rows


def _peer_route_kernel(x_ref, g_ref, mod_ref, wq_ref, keys_ref, xnt_ref, e1_ref, th_ref, s2_ref, e2_ref, xn_scr):
    h = pl.program_id(2)

    @pl.when(h == 0)
    def _():
        xn = _modnorm(x_ref[...], g_ref[...], mod_ref, 3, 4)
        xn_scr[...] = xn.astype(BF16)
        xnt_ref[...] = xn.T.astype(BF16)

    q = jnp.dot(xn_scr[...], wq_ref[...], preferred_element_type=F32).astype(BF16)
    half = PEER_DQ // 2
    s1 = _nt(keys_ref[0], q[:, :half])
    s2 = _nt(keys_ref[1], q[:, half:])
    top1 = _top16_rows(s1)
    top2 = _top16_rows(s2)
    pad_rows = [jnp.full_like(top1[0], -jnp.inf)] * (-len(STAIR) % 8)
    cand = jnp.concatenate([top1[a] + top2[b] for a, b in STAIR] + pad_rows, axis=0)
    thr = _top_rows(cand, PEER_TOPK)[-1]
    m1, m2 = top1[0], top2[0]
    zsum = jnp.sum(jnp.where(cand >= thr, jnp.exp(cand - (m1 + m2)), 0.0), axis=0, keepdims=True)
    e1_ref[...] = jnp.exp(s1 - m1) / zsum
    th_ref[...] = thr - s1
    s2_ref[...] = s2
    e2_ref[...] = jnp.exp(s2 - m2)


def _peer_route(x, g, mod, wq, keys):
    b, l, d = x.shape
    tm = min(256, l)
    nt = l // tm
    n = b * l
    hb = pl.BlockSpec((None, PEER_NKEYS, tm), lambda bi, i, h: (h, 0, bi * nt + i))
    return pl.pallas_call(
        _peer_route_kernel,
        grid=(b, nt, PEER_HEADS),
        in_specs=[
            pl.BlockSpec((None, tm, d), lambda bi, i, h: (bi, i, 0)),
            pl.BlockSpec((1, d), lambda bi, i, h: (0, 0)),
            pl.BlockSpec((None, 6, d), lambda bi, i, h: (bi, 0, 0)),
            pl.BlockSpec((d, PEER_DQ), lambda bi, i, h: (0, h)),
            pl.BlockSpec((None, 2, PEER_NKEYS, PEER_DQ // 2), lambda bi, i, h: (h, 0, 0, 0)),
        ],
        out_specs=[pl.BlockSpec((d, tm), lambda bi, i, h: (0, bi * nt + i)), hb, hb, hb, hb],
        out_shape=[jax.ShapeDtypeStruct((d, n), BF16)] + [jax.ShapeDtypeStruct((PEER_HEADS, PEER_NKEYS, n), F32)] * 4,
        scratch_shapes=[pltpu.VMEM((tm, d), BF16)],
        compiler_params=_cparams(("parallel", "parallel", "arbitrary")),
        name="peer_route",
    )(x, g.reshape(1, d), mod, wq, keys)


GELU_C = math.sqrt(2.0 / math.pi)
PEER_ROWS = 8
PEER_VMEM_LIMIT = 58 * 1024 * 1024
GATE_RB = 32
GATE_IG = 4


def _gelu_tanh(x):
    k = -2.0 * GELU_C * LOG2E
    return x / (1.0 + jnp.exp2(x * (k + (k * 0.044715) * (x * x))))


def _peer_expert_kernel(*refs, final):
    refs = list(refs)
    x_ref, mod_ref, xnt_ref, e1p_ref, thp_ref, e1c_ref, thc_ref, s2_ref, e2_ref, u_ref, vt_ref = refs[:11]
    pos = 11
    if final:
        fg_ref = refs[pos]
        pos += 1
    o_ref, acc_scr = refs[pos:pos + 2]
    a_scr = refs[pos + 2:pos + 4]
    wg_scr = refs[pos + 4:pos + 6]
    be1_scr, bth_scr = refs[pos + 6:pos + 8]
    g = pl.program_id(2)
    last = pl.num_programs(2) - 1
    eb = PEER_ROWS * PEER_NKEYS

    def scores(blk, slot):
        a_scr[slot][...] = jnp.dot(u_ref[blk * eb:(blk + 1) * eb, :], xnt_ref[...], preferred_element_type=F32)

    def gate(slot, e1_ref, th_ref):
        tm = s2_ref.shape[2]
        zero8 = jnp.zeros((8, tm), F32)
        for h in range(PEER_HEADS):
            for ii in range(PEER_ROWS):
                be1_scr[h * PEER_ROWS + ii] = e1_ref[h, pl.ds(ii, 1), :] + zero8
                bth_scr[h * PEER_ROWS + ii] = th_ref[h, pl.ds(ii, 1), :] + zero8
        for c0 in range(0, tm, LANES):
            cs = slice(c0, c0 + LANES)
            for r0 in range(0, PEER_NKEYS, GATE_RB):
                for ig in range(0, PEER_ROWS, GATE_IG):
                    ws = [None] * GATE_IG
                    for h in range(PEER_HEADS):
                        s2 = s2_ref[h, r0:r0 + GATE_RB, cs].reshape(GATE_RB // 8, 8, LANES)
                        e2 = e2_ref[h, r0:r0 + GATE_RB, cs].reshape(GATE_RB // 8, 8, LANES)
                        for k in range(GATE_IG):
                            row = h * PEER_ROWS + ig + k
                            contrib = jnp.where(s2 >= bth_scr[row, :, cs][None], e2 * be1_scr[row, :, cs][None], 0.0)
                            ws[k] = contrib if ws[k] is None else ws[k] + contrib
                    for k in range(GATE_IG):
                        rs = slice((ig + k) * PEER_NKEYS + r0, (ig + k) * PEER_NKEYS + r0 + GATE_RB)
                        wk = ws[k].reshape(GATE_RB, LANES)
                        wg_scr[slot][rs, cs] = (wk * _gelu_tanh(a_scr[slot][rs, cs])).astype(BF16)

    def accum(slot, blk):
        acc_scr[...] += jnp.dot(vt_ref[:, blk * eb:(blk + 1) * eb], wg_scr[slot][...], preferred_element_type=F32)

    @pl.when(g == 0)
    def _():
        acc_scr[...] = jnp.zeros_like(acc_scr)
        scores(1, 1)
        scores(0, 0)
        gate(0, e1c_ref, thc_ref)

    @pl.when((g > 0) & (g < last))
    def _():
        gate(1, e1p_ref, thp_ref)
        accum(0, 0)
        scores(1, 1)
        accum(1, 1)
        scores(0, 0)
        gate(0, e1c_ref, thc_ref)

    @pl.when(g == last)
    def _():
        gate(1, e1p_ref, thp_ref)
        accum(0, 0)
        accum(1, 1)
        y = x_ref[...] + mod_ref[5:6, :] * acc_scr[...].T
        if final:
            ms = jnp.mean(y * y, axis=-1, keepdims=True)
            y = y * lax.rsqrt(ms + NORM_EPS) * fg_ref[...]
        o_ref[...] = y


def _peer_experts(x, mod, xnt, e1, th, s2, e2, u, vt, final_g):
    b, l, d = x.shape
    tm = min(512, l)
    nt = l // tm
    eb = PEER_ROWS * PEER_NKEYS
    n_blocks = u.shape[0] // eb
    assert n_blocks % 2 == 0
    n_steps = n_blocks // 2 + 1
    final = final_g is not None
    tok = lambda bi, i: bi * nt + i
    prev_rows = lambda bi, i, g: (0, jnp.maximum(2 * g - 1, 0), tok(bi, i))
    cur_rows = lambda bi, i, g: (0, jnp.minimum(2 * g, n_blocks - 1), tok(bi, i))
    row_blk = (PEER_HEADS, PEER_ROWS, tm)
    in_specs = [
        pl.BlockSpec((None, tm, d), lambda bi, i, g: (bi, i, 0)),
        pl.BlockSpec((None, 6, d), lambda bi, i, g: (bi, 0, 0)),
        pl.BlockSpec((d, tm), lambda bi, i, g: (0, tok(bi, i))),
        pl.BlockSpec(row_blk, prev_rows),
        pl.BlockSpec(row_blk, prev_rows),
        pl.BlockSpec(row_blk, cur_rows),
        pl.BlockSpec(row_blk, cur_rows),
        pl.BlockSpec((PEER_HEADS, PEER_NKEYS, tm), lambda bi, i, g: (0, 0, tok(bi, i))),
        pl.BlockSpec((PEER_HEADS, PEER_NKEYS, tm), lambda bi, i, g: (0, 0, tok(bi, i))),
        pl.BlockSpec((2 * eb, d), lambda bi, i, g: (jnp.minimum(g, n_steps - 2), 0)),
        pl.BlockSpec((d, 2 * eb), lambda bi, i, g: (0, jnp.maximum(g - 1, 0))),
    ]
    args = [x, mod, xnt, e1, th, e1, th, s2, e2, u, vt]
    if final:
        in_specs.append(pl.BlockSpec((1, d), lambda bi, i, g: (0, 0)))
        args.append(final_g.reshape(1, d))
    return pl.pallas_call(
        functools.partial(_peer_expert_kernel, final=final),
        grid=(b, nt, n_steps),
        in_specs=in_specs,
        out_specs=pl.BlockSpec((None, tm, d), lambda bi, i, g: (bi, i, 0)),
        out_shape=jax.ShapeDtypeStruct((b, l, d), F32),
        scratch_shapes=[pltpu.VMEM((d, tm), F32), pltpu.VMEM((eb, tm), F32), pltpu.VMEM((eb, tm), F32),
                        pltpu.VMEM((eb, tm), BF16), pltpu.VMEM((eb, tm), BF16),
                        pltpu.VMEM((PEER_HEADS * PEER_ROWS, 8, tm), F32), pltpu.VMEM((PEER_HEADS * PEER_ROWS, 8, tm), F32)],
        compiler_params=_cparams(("parallel", "parallel", "arbitrary"), PEER_VMEM_LIMIT),
        name="peer_experts",
    )(*args)


def _peer_pipe_kernel(*refs, final, n_pairs):
    refs = list(refs)
    x_ref, mod_ref, xnt_ref, e1_ref, th_ref, s2_ref, e2_ref, u_ref, vt_ref = refs[:9]
    pos = 9
    if final:
        fg_ref = refs[pos]
        pos += 1
    o_ref, acc_scr = refs[pos:pos + 2]
    a_scr = (refs[pos + 2:pos + 4], refs[pos + 4:pos + 6])
    wg_scr = (refs[pos + 6:pos + 8], refs[pos + 8:pos + 10])
    be1_scr = refs[pos + 10:pos + 12]
    bth_scr = refs[pos + 12:pos + 14]
    g = pl.program_id(2)
    eb = PEER_ROWS * PEER_NKEYS
    tm = s2_ref.shape[2]

    def scores(p):
        for k in range(2):
            a_scr[p][k][...] = jnp.dot(u_ref[k * eb:(k + 1) * eb, :], xnt_ref[...], preferred_element_type=F32)

    def gate_one(a_ref, wg_ref, k):
        zero8 = jnp.zeros((8, tm), F32)
        for h in range(PEER_HEADS):
            for ii in range(PEER_ROWS):
                be1_scr[k][h * PEER_ROWS + ii] = e1_ref[h, pl.ds(k * PEER_ROWS + ii, 1), :] + zero8
                bth_scr[k][h * PEER_ROWS + ii] = th_ref[h, pl.ds(k * PEER_ROWS + ii, 1), :] + zero8
        for c0 in range(0, tm, LANES):
            cs = slice(c0, c0 + LANES)
            for r0 in range(0, PEER_NKEYS, GATE_RB):
                for ig in range(0, PEER_ROWS, GATE_IG):
                    ws = [None] * GATE_IG
                    for h in range(PEER_HEADS):
                        s2 = s2_ref[h, r0:r0 + GATE_RB, cs].reshape(GATE_RB // 8, 8, LANES)
                        e2 = e2_ref[h, r0:r0 + GATE_RB, cs].reshape(GATE_RB // 8, 8, LANES)
                        for j in range(GATE_IG):
                            row = h * PEER_ROWS + ig + j
                            contrib = jnp.where(s2 >= bth_scr[k][row, :, cs][None], e2 * be1_scr[k][row, :, cs][None], 0.0)
                            ws[j] = contrib if ws[j] is None else ws[j] + contrib
                    for j in range(GATE_IG):
                        rs = slice((ig + j) * PEER_NKEYS + r0, (ig + j) * PEER_NKEYS + r0 + GATE_RB)
                        wj = ws[j].reshape(GATE_RB, LANES)
                        wg_ref[rs, cs] = (wj * _gelu_tanh(a_ref[rs, cs])).astype(BF16)

    def gates(p):
        for k in range(2):
            gate_one(a_scr[1 - p][k], wg_scr[p][k], k)

    def accums(p):
        for k in range(2):
            acc_scr[...] += jnp.dot(vt_ref[:, k * eb:(k + 1) * eb], wg_scr[1 - p][k][...], preferred_element_type=F32)

    @pl.when(g == 0)
    def _():
        acc_scr[...] = jnp.zeros_like(acc_scr)
        scores(0)

    @pl.when(g == 1)
    def _():
        scores(1)
        gates(1)

    for p in range(2):
        @pl.when((g >= 2) & (g < n_pairs) & (g % 2 == p))
        def _(p=p):
            scores(p)
            gates(p)
            accums(p)

    @pl.when(g == n_pairs)
    def _():
        gates(n_pairs % 2)
        accums(n_pairs % 2)

    @pl.when(g == n_pairs + 1)
    def _():
        accums((n_pairs + 1) % 2)
        y = x_ref[...] + mod_ref[5:6, :] * acc_scr[...].T
        if final:
            ms = jnp.mean(y * y, axis=-1, keepdims=True)
            y = y * lax.rsqrt(ms + NORM_EPS) * fg_ref[...]
        o_ref[...] = y


def _peer_pipe(x, mod, xnt, e1, th, s2, e2, u, vt, final_g):
    b, l, d = x.shape
    tm = min(512, l)
    nt = l // tm
    eb = PEER_ROWS * PEER_NKEYS
    n_pairs = u.shape[0] // (2 * eb)
    assert n_pairs >= 2
    final = final_g is not None
    tok = lambda bi, i: bi * nt + i
    clamp = lambda v: jnp.clip(v, 0, n_pairs - 1)
    rows = pl.BlockSpec((PEER_HEADS, 2 * PEER_ROWS, tm), lambda bi, i, g: (0, clamp(g - 1), tok(bi, i)))
    keyblk = pl.BlockSpec((PEER_HEADS, PEER_NKEYS, tm), lambda bi, i, g: (0, 0, tok(bi, i)))
    in_specs = [
        pl.BlockSpec((None, tm, d), lambda bi, i, g: (bi, i, 0)),
        pl.BlockSpec((None, 6, d), lambda bi, i, g: (bi, 0, 0)),
        pl.BlockSpec((d, tm), lambda bi, i, g: (0, tok(bi, i))),
        rows, rows, keyblk, keyblk,
        pl.BlockSpec((2 * eb, d), lambda bi, i, g: (clamp(g), 0)),
        pl.BlockSpec((d, 2 * eb), lambda bi, i, g: (0, clamp(g - 2))),
    ]
    args = [x, mod, xnt, e1, th, s2, e2, u, vt]
    if final:
        in_specs.append(pl.BlockSpec((1, d), lambda bi, i, g: (0, 0)))
        args.append(final_g.reshape(1, d))
    bcast = pltpu.VMEM((PEER_HEADS * PEER_ROWS, 8, tm), F32)
    return pl.pallas_call(
        functools.partial(_peer_pipe_kernel, final=final, n_pairs=n_pairs),
        grid=(b, nt, n_pairs + 2),
        in_specs=in_specs,
        out_specs=pl.BlockSpec((None, tm, d), lambda bi, i, g: (bi, i, 0)),
        out_shape=jax.ShapeDtypeStruct((b, l, d), F32),
        scratch_shapes=([pltpu.VMEM((d, tm), F32)] + [pltpu.VMEM((eb, tm), F32)] * 4 + [pltpu.VMEM((eb, tm), BF16)] * 4
                        + [bcast] * 4),
        compiler_params=_cparams(("parallel", "parallel", "arbitrary"), PEER_VMEM_LIMIT),
        name="peer_experts",
    )(*args)


def _pad_cols(a, width):
    return jnp.pad(a, ((0, 0), (0, width - a.shape[1])))


def _layout_w_in(w):
    d = w.shape[0]
    z = lambda n: jnp.zeros((d, n), w.dtype)
    gla, hg, mla, dif = 0, 800, 2080, 2432
    parts = [w[:, gla:gla + 768], w[:, hg:hg + 768],
             w[:, mla:mla + 192], z(64), w[:, mla + 192:mla + 320], z(64), w[:, mla + 320:mla + 352], z(32)]
    for base in (dif, dif + 256, dif + 512):
        for h in range(DIFF_HEADS):
            parts += [w[:, base + 64 * h:base + 64 * (h + 1)], z(64)]
    parts += [w[:, gla + 768:gla + 800], z(224), w[:, hg + 768:hg + 1280]]
    out = jnp.concatenate(parts, axis=1)
    assert out.shape[1] == MAIN_COLS + GATE_COLS
    return out.astype(BF16)


def _layout_mla(w_uq, w_ukv):
    dq = MLA_NOPE + MLA_ROPE
    wq = jnp.concatenate([_pad_cols(w_uq[:, dq * h:dq * (h + 1)], LANES) for h in range(MLA_HEADS)], axis=1)
    wq = jnp.pad(wq, ((0, 256 - MLA_Q_RANK), (0, 0)))
    per = MLA_NOPE + MLA_DV
    wk = jnp.concatenate([_pad_cols(w_ukv[:, per * h:per * h + MLA_NOPE], LANES) for h in range(MLA_HEADS)], axis=1)
    wv = jnp.concatenate([_pad_cols(w_ukv[:, per * h + MLA_NOPE:per * (h + 1)], LANES) for h in range(MLA_HEADS)], axis=1)
    return wq.astype(BF16), jnp.concatenate([wk, wv], axis=1).astype(BF16)


def kernel(x, c, ctx, c_ctx, ada_w, ada_b, norm_mix_g, norm_ffn_g, w_in, w_out, gla_gate_w, gla_gate_b, gla_norm_g, hgrn_lb_raw, hgrn_norm_g, mla_q_norm_g, mla_kv_norm_g, mla_w_uq, mla_w_ukv, diff_lambda, diff_norm_g, peer_wq, peer_keys, peer_u, peer_v, final_norm_g):
    b, l, d = x.shape
    lc = ctx.shape[1]
    depth = ada_w.shape[0]
    assert lc <= ATT_TK, "context keys are consumed as one attention step"

    rows = -(-(b + 1) // 8) * 8
    cpad = jnp.zeros((rows, d), F32).at[:b].set(c).at[b].set(c_ctx)
    mod_all = _adaln(cpad, ada_w, ada_b)

    lb_sm = jax.nn.softmax(hgrn_lb_raw.astype(F32), axis=0)
    lb_all = jnp.cumsum(lb_sm, axis=0) - lb_sm[0]

    lat_mla_tab = _rope_tables(l, 64, 32, True)
    ctx_mla_tab = _rope_tables(lc, 64, 32, False)
    lat_dif_tab = _rope_tables(l, 0, 64, True)
    ctx_dif_tab = _rope_tables(lc, 0, 64, False)

    xc = ctx
    for li in range(depth):
        need_ctx = li < depth - 1
        lam_init = 0.8 - 0.6 * math.exp(-0.3 * li)
        mod_l = mod_all[li, :b].reshape(b, 6, d)
        mod_c = jnp.broadcast_to(mod_all[li, b].reshape(1, 6, d), (b, 6, d))

        w_perm = _layout_w_in(w_in[li])
        w_o = w_out[li].astype(BF16)
        wg = jnp.zeros((2, 256, 128), F32)
        wg = wg.at[0, 0:GLA_GATE_RANK].set(gla_gate_w[li, 0]).at[1, GLA_GATE_RANK:2 * GLA_GATE_RANK].set(gla_gate_w[li, 1])
        wg = wg.astype(BF16)
        bg = gla_gate_b[li].reshape(2, 1, 128)
        gla_gn = jnp.tile(gla_norm_g[li], GLA_HEADS).reshape(1, 256)
        hg_gn = jnp.tile(hgrn_norm_g[li], HG_HEADS).reshape(1, 256)
        lb = lb_all[li]
        hg_p = jnp.zeros((8, 256), F32).at[0].set(jnp.log(jnp.maximum(lb, LB_FLOOR))).at[1].set(jnp.log1p(-lb)).at[2].set(1.0 - lb)
        hg_dummy = jnp.zeros((8, 128), F32)
        gq = jnp.pad(mla_q_norm_g[li], (0, 256 - MLA_Q_RANK)).reshape(1, 256)
        gkv = mla_kv_norm_g[li].reshape(1, MLA_KV_RANK)
        wq_mla, wkv_mla = _layout_mla(mla_w_uq[li], mla_w_ukv[li])
        lq1, lk1, lq2, lk2 = diff_lambda[li, 0], diff_lambda[li, 1], diff_lambda[li, 2], diff_lambda[li, 3]
        lam = (jnp.exp(jnp.sum(lq1 * lk1).astype(F32)) - jnp.exp(jnp.sum(lq2 * lk2).astype(F32)) + lam_init)
        lam_row = jnp.full((1, LANES), lam, F32)
        dif_gn = jnp.tile(diff_norm_g[li], 2).reshape(1, LANES)
        wq_peer = peer_wq[li].astype(BF16)
        keys = peer_keys[li].astype(BF16)
        u_bf = peer_u[li].astype(BF16)
        vt_bf = peer_v[li].astype(BF16).T

        main_c, gate_c = _inproj(xc, norm_mix_g[li], mod_c, w_perm)
        main_l, gate_l = _inproj(x, norm_mix_g[li], mod_l, w_perm)

        mixes_c, mixes_l = [], []
        for mixer, pa_f, pb_f, pa_b, pb_b, gn, dk in (
            ("gla", wg[0], bg[0], wg[1], bg[1], gla_gn, 128),
            ("hg", hg_p, hg_dummy, hg_p, hg_dummy, hg_gn, 256),
        ):
            zero = jnp.zeros((b, 256, dk), F32)
            o_cf, s_f = _scan(main_c, gate_c, pa_f, pb_f, gn, zero, None, mixer=mixer, reverse=False, chunk=64)
            mix_c, s_b = _scan(main_c, gate_c, pa_b, pb_b, gn, zero, o_cf, mixer=mixer, reverse=True, chunk=64)
            o_lf, _ = _scan(main_l, gate_l, pa_f, pb_f, gn, s_f, None, mixer=mixer, reverse=False, chunk=64)
            mix_l, _ = _scan(main_l, gate_l, pa_b, pb_b, gn, s_b, o_lf, mixer=mixer, reverse=True, chunk=64)
            mixes_c.append(mix_c)
            mixes_l.append(mix_l)

        qm_c, km_c, vm_c = _mla_prep(main_c, gq, gkv, wq_mla, wkv_mla, ctx_mla_tab)
        qm_l, km_l, vm_l = _mla_prep(main_l, gq, gkv, wq_mla, wkv_mla, lat_mla_tab)
        mla_l = _attention(qm_l, km_c, vm_c, km_l, vm_l, None, n_maps=1, lam_init=lam_init)
        qd_c, kd_c, vd_c = _diff_prep(main_c, ctx_dif_tab)
        qd_l, kd_l, vd_l = _diff_prep(main_l, lat_dif_tab)
        dif_l = _attention(qd_l, kd_c, vd_c, kd_l, vd_l, (lam_row, dif_gn), n_maps=2, lam_init=lam_init)

        x = _outproj(x, mixes_l[0], mixes_l[1], mla_l, dif_l, w_o, mod_l)
        rt = _peer_route(x, norm_ffn_g[li], mod_l, wq_peer, keys)
        x = _peer_pipe(x, mod_l, *rt, u_bf, vt_bf, None if need_ctx else final_norm_g)

        if need_ctx:
            mla_c = _attention(qm_c, km_c, vm_c, None, None, None, n_maps=1, lam_init=lam_init)
            dif_c = _attention(qd_c, kd_c, vd_c, None, None, (lam_row, dif_gn), n_maps=2, lam_init=lam_init)
            xc = _outproj(xc, mixes_c[0], mixes_c[1], mla_c, dif_c, w_o, mod_c)
            rtc = _peer_route(xc, norm_ffn_g[li], mod_c, wq_peer, keys)
            xc = _peer_pipe(xc, mod_c, *rtc, u_bf, vt_bf, None)
    return x
```

```python
import functools
import math

import jax
import jax.numpy as jnp
import numpy as np
from jax import lax
from jax.experimental import pallas as pl
from jax.experimental.pallas import tpu as pltpu

F32 = jnp.float32
BF16 = jnp.bfloat16

NORM_EPS = 1e-6
ROPE_BASE = 10000.0
GRID_W = 64
LB_FLOOR = 1e-30

GLA_HEADS, GLA_DK, GLA_DV, GLA_GATE_RANK, GLA_GATE_NORM = 4, 32, 64, 16, 16.0
HG_HEADS, HG_DIM = 4, 64
MLA_HEADS, MLA_Q_RANK, MLA_KV_RANK, MLA_NOPE, MLA_ROPE, MLA_DV = 4, 192, 128, 64, 32, 64
DIFF_HEADS, DIFF_DQK, DIFF_DV = 4, 32, 64
PEER_HEADS, PEER_NKEYS, PEER_DQ, PEER_TOPK = 8, 128, 256, 16

LANES = 128
VMEM_LIMIT = 48 * 1024 * 1024
EXP_CAP = 80.0
NEG_BIG = -1e30
ATT_TK = 512
LOG2E = 1.4426950408889634
ATT_R = 512
VT_ROWS = 80

COL_GLA, COL_HG, COL_MLA, COL_DQ, COL_DK, COL_DV, MAIN_COLS = 0, 768, 1536, 2048, 2560, 3072, 3584
GATE_COLS = 768

STAIR = [(a, b) for a in range(PEER_TOPK) for b in range(PEER_TOPK) if (a + 1) * (b + 1) <= PEER_TOPK]


def _cparams(sem, vmem=VMEM_LIMIT):
    return pltpu.CompilerParams(dimension_semantics=sem, vmem_limit_bytes=vmem)


def _sigmoid(x):
    return 1.0 / (1.0 + jnp.exp(-x))


def _logsig(x):
    return jnp.minimum(x, 0.0) - jnp.log(1.0 + jnp.exp(-jnp.abs(x)))


def _nt(a, b):
    return lax.dot_general(a, b, (((1,), (1,)), ((), ())), preferred_element_type=F32)


def _tn(a, b):
    return lax.dot_general(a, b, (((0,), (0,)), ((), ())), preferred_element_type=F32)


def _adaln_kernel(c_ref, w_ref, b_ref, o_ref):
    c = c_ref[...]
    s = c * _sigmoid(c)
    o_ref[...] = jnp.dot(s.astype(BF16), w_ref[...].astype(BF16), preferred_element_type=F32) + b_ref[...]


def _adaln(cpad, ada_w, ada_b):
    depth, d, n6 = ada_w.shape
    rows = cpad.shape[0]
    tn = 512
    return pl.pallas_call(
        _adaln_kernel,
        grid=(depth, n6 // tn),
        in_specs=[
            pl.BlockSpec((rows, d), lambda l, j: (0, 0)),
            pl.BlockSpec((None, d, tn), lambda l, j: (l, 0, j)),
            pl.BlockSpec((None, 1, tn), lambda l, j: (l, 0, j)),
        ],
        out_specs=pl.BlockSpec((None, rows, tn), lambda l, j: (l, 0, j)),
        out_shape=jax.ShapeDtypeStruct((depth, rows, n6), F32),
        compiler_params=_cparams(("parallel", "parallel")),
        name="adaln",
    )(cpad, ada_w, ada_b.reshape(depth, 1, n6))


def _modnorm(x, g, mod_ref, shift_row, scale_row):
    ms = jnp.mean(x * x, axis=-1, keepdims=True)
    h = x * lax.rsqrt(ms + NORM_EPS) * g
    return h * (1.0 + mod_ref[scale_row:scale_row + 1, :]) + mod_ref[shift_row:shift_row + 1, :]


def _inproj_kernel(x_ref, g_ref, mod_ref, w_ref, main_ref, gate_ref):
    h = _modnorm(x_ref[...], g_ref[...], mod_ref, 0, 1)
    y = jnp.dot(h.astype(BF16), w_ref[...], preferred_element_type=F32)
    main_ref[...] = y[:, :MAIN_COLS].astype(BF16)
    gate_ref[...] = y[:, MAIN_COLS:]


def _inproj(x, g, mod, w_perm):
    b, l, d = x.shape
    tm = min(256, l)
    nc = w_perm.shape[1]
    return pl.pallas_call(
        _inproj_kernel,
        grid=(b, l // tm),
        in_specs=[
            pl.BlockSpec((None, tm, d), lambda bi, i: (bi, i, 0)),
            pl.BlockSpec((1, d), lambda bi, i: (0, 0)),
            pl.BlockSpec((None, 6, d), lambda bi, i: (bi, 0, 0)),
            pl.BlockSpec((d, nc), lambda bi, i: (0, 0)),
        ],
        out_specs=[
            pl.BlockSpec((None, tm, MAIN_COLS), lambda bi, i: (bi, i, 0)),
            pl.BlockSpec((None, tm, GATE_COLS), lambda bi, i: (bi, i, 0)),
        ],
        out_shape=[
            jax.ShapeDtypeStruct((b, l, MAIN_COLS), BF16),
            jax.ShapeDtypeStruct((b, l, GATE_COLS), F32),
        ],
        compiler_params=_cparams(("parallel", "parallel")),
        name="inproj",
    )(x, g.reshape(1, d), mod, w_perm)


def _scan_kernel(*refs, mixer, reverse, finish, chunk):
    if finish:
        main_ref, gate_ref, pa_ref, pb_ref, gn_ref, s0_ref, oprev_ref, o_ref, sfin_ref, st_scr = refs
    else:
        main_ref, gate_ref, pa_ref, pb_ref, gn_ref, s0_ref, o_ref, sfin_ref, st_scr = refs
        oprev_ref = None
    i = pl.program_id(1)

    @pl.when(i == 0)
    def _():
        st_scr[...] = s0_ref[...]

    main = main_ref[...]
    t = main.shape[0]
    c = chunk
    nc = t // c
    heads = 4
    dv = 256
    v = main[:, 256:512]
    gcol = main[:, 512:768].astype(F32)
    if mixer == "gla":
        dk = 128
        q = main[:, 0:128].astype(F32) * (GLA_DK ** -0.5)
        k = main[:, 128:256].astype(F32)
        z = jnp.dot(gate_ref[...].astype(BF16), pa_ref[...], preferred_element_type=F32) + pb_ref[...]
        g = _logsig(z) * (1.0 / GLA_GATE_NORM)
    else:
        dk = 256
        qq = main[:, 0:256].astype(F32)
        q = qq * _sigmoid(qq)
        z = gate_ref[...]
        la = pa_ref[0:1, :]
        lb = pa_ref[1:2, :] + _logsig(z)
        g = jnp.maximum(la, lb) + jnp.log(1.0 + jnp.exp(-jnp.abs(la - lb)))
        k = pa_ref[2:3, :] * _sigmoid(-z)
    hk = dk // heads
    hv = dv // heads

    rt = lax.broadcasted_iota(jnp.int32, (t, t), 0)
    ct = lax.broadcasted_iota(jnp.int32, (t, t), 1)
    same = (rt // c) == (ct // c)
    tri = jnp.where(same & ((ct >= rt) if reverse else (ct <= rt)), 1.0, 0.0).astype(BF16)
    g1 = g.astype(BF16)
    r1 = g - g1.astype(F32)
    g2 = r1.astype(BF16)
    g3 = (r1 - g2.astype(F32)).astype(BF16)
    bcum = (jnp.dot(tri, g1, preferred_element_type=F32) + jnp.dot(tri, g2, preferred_element_type=F32)
            + jnp.dot(tri, g3, preferred_element_type=F32))

    b3 = bcum.reshape(nc, c, dk)
    q3 = q.reshape(nc, c, dk)
    k3 = k.reshape(nc, c, dk)
    if reverse:
        bmid = b3[:, c // 2:c // 2 + 1, :]
        bend = b3[:, 0:1, :]
    else:
        bmid = b3[:, c // 2 - 1:c // 2, :]
        bend = b3[:, c - 1:c, :]
    eb = b3 - bmid
    qa = (q3 * jnp.exp(jnp.minimum(eb, EXP_CAP))).reshape(t, dk).astype(BF16)
    ka = (k3 * jnp.exp(jnp.minimum(-eb, EXP_CAP))).reshape(t, dk).astype(BF16)
    qi = (q3 * jnp.exp(b3)).reshape(t, dk).astype(BF16)
    kb = (k3 * jnp.exp(bend - b3)).reshape(t, dk).astype(BF16)
    dec = jnp.exp(bend)

    lane_k = lax.broadcasted_iota(jnp.int32, (1, dk), 1) // hk
    lane_v = lax.broadcasted_iota(jnp.int32, (1, dv), 1) // hv
    rr = lax.broadcasted_iota(jnp.int32, (c, heads * c), 0)
    cc = lax.broadcasted_iota(jnp.int32, (c, heads * c), 1) % c
    causal = (cc >= rr) if reverse else (cc <= rr)
    bd = (lax.broadcasted_iota(jnp.int32, (dv, dk), 0) // hv) == (lax.broadcasted_iota(jnp.int32, (dv, dk), 1) // hk)

    o_intra = []
    upd = []
    for ci in range(nc):
        sl = slice(ci * c, (ci + 1) * c)
        ka_c = ka[sl]
        v_c = v[sl]
        kst = jnp.concatenate([jnp.where(lane_k == h, ka_c, 0) for h in range(heads)], axis=0)
        vst = jnp.concatenate([jnp.where(lane_v == h, v_c, 0) for h in range(heads)], axis=0)
        sw = _nt(qa[sl], kst)
        p = jnp.where(causal, sw, 0.0).astype(BF16)
        o_intra.append(jnp.dot(p, vst, preferred_element_type=F32))
        upd.append(jnp.where(bd, _tn(v_c, kb[sl]), 0.0))

    outs = [None] * nc
    order = range(nc - 1, -1, -1) if reverse else range(nc)
    for ci in order:
        sl = slice(ci * c, (ci + 1) * c)
        st = st_scr[...]
        outs[ci] = o_intra[ci] + _nt(qi[sl], st.astype(BF16))
        st_scr[...] = st * dec[ci] + upd[ci]
    o = jnp.concatenate(outs, axis=0)

    @pl.when(i == pl.num_programs(1) - 1)
    def _():
        sfin_ref[...] = st_scr[...]

    if not finish:
        o_ref[...] = o
    else:
        ot = o + oprev_ref[...]
        sq = ot * ot
        jv = ((lax.broadcasted_iota(jnp.int32, (dv, dv), 0) // hv)
              == (lax.broadcasted_iota(jnp.int32, (dv, dv), 1) // hv))
        jm = jnp.where(jv, 1.0, 0.0).astype(BF16)
        s1 = sq.astype(BF16)
        s2 = (sq - s1.astype(F32)).astype(BF16)
        ms = (jnp.dot(s1, jm, preferred_element_type=F32) + jnp.dot(s2, jm, preferred_element_type=F32)) * (1.0 / hv)
        y = ot * lax.rsqrt(ms + NORM_EPS) * gn_ref[...]
        o_ref[...] = (y * (gcol * _sigmoid(gcol))).astype(BF16)


def _scan(main, gate, pa, pb, gn, s0, oprev, *, mixer, reverse, chunk):
    b, l, _ = main.shape
    t = min(256, l)
    nb = l // t
    finish = oprev is not None
    dk = 128 if mixer == "gla" else 256
    dv = 256
    main_blk = 0 if mixer == "gla" else 1
    gate_blk = 0 if mixer == "gla" else (2 if reverse else 1)

    def tok(bi, i):
        return (nb - 1 - i) if reverse else i

    in_specs = [
        pl.BlockSpec((None, t, 768), lambda bi, i: (bi, tok(bi, i), main_blk)),
        pl.BlockSpec((None, t, 256), lambda bi, i: (bi, tok(bi, i), gate_blk)),
        pl.BlockSpec(pa.shape, lambda bi, i: (0, 0)),
        pl.BlockSpec(pb.shape, lambda bi, i: (0, 0)),
        pl.BlockSpec((1, dv), lambda bi, i: (0, 0)),
        pl.BlockSpec((None, dv, dk), lambda bi, i: (bi, 0, 0)),
    ]
    args = [main, gate, pa, pb, gn, s0]
    if finish:
        in_specs.append(pl.BlockSpec((None, t, dv), lambda bi, i: (bi, tok(bi, i), 0)))
        args.append(oprev)
    return pl.pallas_call(
        functools.partial(_scan_kernel, mixer=mixer, reverse=reverse, finish=finish, chunk=chunk),
        grid=(b, nb),
        in_specs=in_specs,
        out_specs=[
            pl.BlockSpec((None, t, dv), lambda bi, i: (bi, tok(bi, i), 0)),
            pl.BlockSpec((None, dv, dk), lambda bi, i: (bi, 0, 0)),
        ],
        out_shape=[
            jax.ShapeDtypeStruct((b, l, dv), BF16 if finish else F32),
            jax.ShapeDtypeStruct((b, dv, dk), F32),
        ],
        scratch_shapes=[pltpu.VMEM((dv, dk), F32)],
        compiler_params=_cparams(("parallel", "arbitrary")),
        name=f"scan_{mixer}_{'bwd' if reverse else 'fwd'}",
    )(*args)


def _rope(x, cos, sin_lo, sin_hi):
    return x * cos + pltpu.roll(x, LANES - 8, 1) * sin_lo + pltpu.roll(x, 8, 1) * sin_hi


def _rope_tables(l, first_lane, n_lanes, rotate):
    lane = np.arange(LANES)
    d = (lane - first_lane) % 32
    active = (lane >= first_lane) & (lane < first_lane + n_lanes)
    freqs = ROPE_BASE ** (-(d % 8).astype(np.float32) / 8.0)
    tt = jnp.arange(l, dtype=jnp.int32)
    rows = (tt // GRID_W).astype(F32)[:, None]
    cols = (tt % GRID_W).astype(F32)[:, None]
    pos = jnp.where(jnp.asarray(d < 16)[None, :], rows, cols)
    ang = pos * jnp.asarray(freqs, F32)[None, :]
    act = jnp.asarray(active)[None, :] & rotate
    lo = jnp.asarray((d % 16) < 8)[None, :]
    cos = jnp.where(act, jnp.cos(ang), 1.0)
    sin = jnp.where(act, jnp.sin(ang), 0.0)
    return cos, jnp.where(lo, -sin, 0.0), jnp.where(lo, 0.0, sin)


def _vt_block(vgrp):
    lane = lax.broadcasted_iota(jnp.int32, (1, LANES), 1)
    return jnp.where(lane == 64, 1.0, vgrp).T[:VT_ROWS].astype(BF16)


def _mla_prep_kernel(slab_ref, gq_ref, gkv_ref, wq_ref, wkv_ref, cos_ref, slo_ref, shi_ref, qt_ref, k_ref, vt_ref):
    slab = slab_ref[...]
    cos, slo, shi = cos_ref[...], slo_ref[...], shi_ref[...]
    cq = slab[:, 0:256].astype(F32)
    qn = cq * lax.rsqrt(jnp.sum(cq * cq, axis=-1, keepdims=True) * (1.0 / MLA_Q_RANK) + NORM_EPS) * gq_ref[...]
    qall = jnp.dot(qn.astype(BF16), wq_ref[...], preferred_element_type=F32)
    ckv = slab[:, 256:384].astype(F32)
    kvn = ckv * lax.rsqrt(jnp.mean(ckv * ckv, axis=-1, keepdims=True) + NORM_EPS) * gkv_ref[...]
    kvall = jnp.dot(kvn.astype(BF16), wkv_ref[...], preferred_element_type=F32)
    kr = _rope(slab[:, 384:512].astype(F32), cos, slo, shi)
    scale = (MLA_NOPE + MLA_ROPE) ** -0.5 * LOG2E
    for h in range(MLA_HEADS):
        qh = _rope(qall[:, h * LANES:(h + 1) * LANES], cos, slo, shi)
        qt_ref[h] = (qh * scale).T.astype(BF16)
        k_ref[h] = (kvall[:, h * LANES:(h + 1) * LANES] + kr).astype(BF16)
        vt_ref[h] = _vt_block(kvall[:, (MLA_HEADS + h) * LANES:(MLA_HEADS + h + 1) * LANES])


def _attn_prep_specs(b, l, tm, heads):
    qt = pl.BlockSpec((None, heads, LANES, tm), lambda bi, i: (bi, 0, 0, i))
    kk = pl.BlockSpec((None, heads, tm, LANES), lambda bi, i: (bi, 0, i, 0))
    vt = pl.BlockSpec((None, heads, None, VT_ROWS, tm), lambda bi, i: (bi, 0, i, 0, 0))
    shapes = [
        jax.ShapeDtypeStruct((b, heads, LANES, l), BF16),
        jax.ShapeDtypeStruct((b, heads, l, LANES), BF16),
        jax.ShapeDtypeStruct((b, heads, l // tm, VT_ROWS, tm), BF16),
    ]
    return [qt, kk, vt], shapes


def _mla_prep(main, gq, gkv, wq, wkv, tables):
    b, l, _ = main.shape
    tm = min(ATT_TK, l)
    cos, slo, shi = tables
    tab = pl.BlockSpec((tm, LANES), lambda bi, i: (i, 0))
    out_specs, out_shape = _attn_prep_specs(b, l, tm, MLA_HEADS)
    return pl.pallas_call(
        _mla_prep_kernel,
        grid=(b, l // tm),
        in_specs=[
            pl.BlockSpec((None, tm, 512), lambda bi, i: (bi, i, COL_MLA // 512)),
            pl.BlockSpec(gq.shape, lambda bi, i: (0, 0)),
            pl.BlockSpec(gkv.shape, lambda bi, i: (0, 0)),
            pl.BlockSpec(wq.shape, lambda bi, i: (0, 0)),
            pl.BlockSpec(wkv.shape, lambda bi, i: (0, 0)),
            tab, tab, tab,
        ],
        out_specs=out_specs,
        out_shape=out_shape,
        compiler_params=_cparams(("parallel", "parallel")),
        name="mla_prep",
    )(main, gq, gkv, wq, wkv, cos, slo, shi)


def _diff_prep_kernel(q_in, k_in, v_in, cos_ref, slo_ref, shi_ref, qt_ref, k_ref, vt_ref):
    cos, slo, shi = cos_ref[...], slo_ref[...], shi_ref[...]
    qs = q_in[...]
    ks = k_in[...]
    vs = v_in[...]
    scale = DIFF_DQK ** -0.5 * LOG2E
    for h in range(DIFF_HEADS):
        sl = slice(h * LANES, (h + 1) * LANES)
        qt_ref[h] = (_rope(qs[:, sl].astype(F32), cos, slo, shi) * scale).T.astype(BF16)
        k_ref[h] = _rope(ks[:, sl].astype(F32), cos, slo, shi).astype(BF16)
        vt_ref[h] = _vt_block(vs[:, sl].astype(F32))


def _diff_prep(main, tables):
    b, l, _ = main.shape
    tm = min(ATT_TK, l)
    cos, slo, shi = tables
    tab = pl.BlockSpec((tm, LANES), lambda bi, i: (i, 0))
    out_specs, out_shape = _attn_prep_specs(b, l, tm, DIFF_HEADS)
    return pl.pallas_call(
        _diff_prep_kernel,
        grid=(b, l // tm),
        in_specs=[
            pl.BlockSpec((None, tm, 512), lambda bi, i: (bi, i, COL_DQ // 512)),
            pl.BlockSpec((None, tm, 512), lambda bi, i: (bi, i, COL_DK // 512)),
            pl.BlockSpec((None, tm, 512), lambda bi, i: (bi, i, COL_DV // 512)),
            tab, tab, tab,
        ],
        out_specs=out_specs,
        out_shape=out_shape,
        compiler_params=_cparams(("parallel", "parallel")),
        name="diff_prep",
    )(main, main, main, cos, slo, shi)


def _attn_kernel(*refs, n_maps, has_lat, lam_init):
    refs = list(refs)
    qt_ref, kc_ref, vtc_ref = refs[:3]
    pos = 3
    if has_lat:
        kl_ref, vtl_ref = refs[pos:pos + 2]
        pos += 2
    if n_maps == 2:
        lam_ref, gn_ref = refs[pos:pos + 2]
        pos += 2
    o_ref = refs[pos]
    if has_lat:
        m_scr, acc_scr, s_scr, cm_scr, al_scr, p_scr = refs[pos + 1:pos + 7]
    tq = qt_ref.shape[2]
    feat = lax.broadcasted_iota(jnp.int32, (LANES, 1), 0)

    qts = []
    for hh in range(2):
        qt = qt_ref[hh]
        if n_maps == 2:
            zero = jnp.zeros_like(qt)
            qt = jnp.concatenate(
                [jnp.where((feat >= DIFF_DQK * mi) & (feat < DIFF_DQK * (mi + 1)), qt, zero) for mi in range(2)], axis=1)
        qts.append(qt)

    accs = []
    for hh in range(2):
        st = jnp.dot(kc_ref[hh], qts[hh], preferred_element_type=F32)
        m0 = jnp.max(st, axis=0, keepdims=True)
        acc0 = jnp.dot(vtc_ref[hh, 0], jnp.exp2(st - m0).astype(BF16), preferred_element_type=F32)
        accs.append(acc0)
        if has_lat:
            m_scr[hh] = m0
            acc_scr[hh] = acc0

    if has_lat:
        n = vtl_ref.shape[1]
        tk = vtl_ref.shape[3]

        def scores(j, slot):
            off = pl.multiple_of(j * tk, tk)
            for hh in range(2):
                st = jnp.dot(kl_ref[hh, pl.ds(off, tk), :], qts[hh], preferred_element_type=F32)
                s_scr[hh, slot] = st
                cm_scr[hh, slot] = jnp.max(st, axis=0, keepdims=True)

        def numer(slot):
            for hh in range(2):
                m_old = m_scr[hh]
                m_new = jnp.maximum(m_old, cm_scr[hh, slot])
                al_scr[hh, slot] = jnp.exp2(m_old - m_new)
                p_scr[hh, slot] = jnp.exp2(s_scr[hh, slot] - m_new).astype(BF16)
                m_scr[hh] = m_new

        def values(j, slot):
            for hh in range(2):
                acc_scr[hh] = (al_scr[hh, slot] * acc_scr[hh]
                               + jnp.dot(vtl_ref[hh, j], p_scr[hh, slot], preferred_element_type=F32))

        first = n % 2
        if first:
            scores(0, 0)
            numer(0)
            values(0, 0)
        if n > first:
            for hh in range(2):
                p_scr[hh, 1] = jnp.zeros(p_scr.shape[2:], BF16)
                al_scr[hh, 1] = jnp.ones(al_scr.shape[2:], F32)
            scores(first, 0)

            def body(t, carry):
                j = first + 2 * t
                scores(j + 1, 1)
                numer(0)
                values(jnp.maximum(j - 1, first), 1)
                scores(jnp.minimum(j + 2, n - 1), 0)
                numer(1)
                values(j, 0)
                return carry

            lax.fori_loop(0, (n - first) // 2, body, 0)
            values(n - 1, 1)
        accs = [acc_scr[0], acc_scr[1]]

    outs = []
    for hh in range(2):
        acc = accs[hh]
        ot = acc[0:64, :] / acc[64:65, :]
        if n_maps == 2:
            oh = ot[:, :tq] - lam_ref[:, 0:1] * ot[:, tq:]
            ms = jnp.mean(oh * oh, axis=0, keepdims=True)
            ot = oh * lax.rsqrt(ms + NORM_EPS)
        outs.append(ot)
    o = jnp.concatenate(outs, axis=0).T
    if n_maps == 2:
        o = o * gn_ref[...] * (1.0 - lam_init)
    o_ref[...] = o.astype(BF16)


def _attention(qt, kc, vtc, kl, vtl, extra, *, n_maps, lam_init):
    b, nh, _, l = qt.shape
    lc = kc.shape[2]
    tq = min(ATT_R // n_maps, l)
    r = n_maps * tq
    has_lat = kl is not None
    scratch = []
    in_specs = [
        pl.BlockSpec((None, 2, LANES, tq), lambda bi, hp, i: (bi, hp, 0, i)),
        pl.BlockSpec((None, 2, lc, LANES), lambda bi, hp, i: (bi, hp, 0, 0)),
        pl.BlockSpec((None, 2) + vtc.shape[2:], lambda bi, hp, i: (bi, hp, 0, 0, 0)),
    ]
    args = [qt, kc, vtc]
    if has_lat:
        in_specs += [
            pl.BlockSpec((None, 2, kl.shape[2], LANES), lambda bi, hp, i: (bi, hp, 0, 0)),
            pl.BlockSpec((None, 2) + vtl.shape[2:], lambda bi, hp, i: (bi, hp, 0, 0, 0)),
        ]
        args += [kl, vtl]
        tk = vtl.shape[4]
        scratch = [
            pltpu.VMEM((2, 1, r), F32), pltpu.VMEM((2, VT_ROWS, r), F32),
            pltpu.VMEM((2, 2, tk, r), F32), pltpu.VMEM((2, 2, 1, r), F32), pltpu.VMEM((2, 2, 1, r), F32),
            pltpu.VMEM((2, 2, tk, r), BF16),
        ]
    if n_maps == 2:
        in_specs += [pl.BlockSpec((1, LANES), lambda bi, hp, i: (0, 0))] * 2
        args += list(extra)
    return pl.pallas_call(
        functools.partial(_attn_kernel, n_maps=n_maps, has_lat=has_lat, lam_init=lam_init),
        grid=(b, nh // 2, l // tq),
        in_specs=in_specs,
        out_specs=pl.BlockSpec((None, tq, LANES), lambda bi, hp, i: (bi, i, hp)),
        out_shape=jax.ShapeDtypeStruct((b, l, nh * 64), BF16),
        scratch_shapes=scratch,
        compiler_params=_cparams(("parallel", "parallel", "arbitrary")),
        name="attn_mla" if n_maps == 1 else "attn_diff",
    )(*args)


def _outproj_kernel(x_ref, a_ref, b_ref, m_ref, d_ref, w_ref, mod_ref, o_ref):
    mix = jnp.concatenate([a_ref[...], b_ref[...], m_ref[...], d_ref[...]], axis=-1)
    y = jnp.dot(mix, w_ref[...], preferred_element_type=F32)
    o_ref[...] = x_ref[...] + mod_ref[2:3, :] * y


def _outproj(x, a, bb, m, dd, w, mod):
    b, l, d = x.shape
    tm = min(512, l)
    slab = pl.BlockSpec((None, tm, 256), lambda bi, i: (bi, i, 0))
    return pl.pallas_call(
        _outproj_kernel,
        grid=(b, l // tm),
        in_specs=[
            pl.BlockSpec((None, tm, d), lambda bi, i: (bi, i, 0)),
            slab, slab, slab, slab,
            pl.BlockSpec(w.shape, lambda bi, i: (0, 0)),
            pl.BlockSpec((None, 6, d), lambda bi, i: (bi, 0, 0)),
        ],
        out_specs=pl.BlockSpec((None, tm, d), lambda bi, i: (bi, i, 0)),
        out_shape=jax.ShapeDtypeStruct((b, l, d), F32),
        compiler_params=_cparams(("parallel", "parallel")),
        name="outproj",
    )(x, a, bb, m, dd, w, mod)


def _top_rows(s, n):
    rows = []
    for r in range(n):
        mx = jnp.max(s, axis=0, keepdims=True)
        rows.append(mx)
        if r + 1 < n:
            s = jnp.where(s == mx, -jnp.inf, s)
    return rows


def _oddeven_merge(lo, hi, r):
    step = r * 2
    if step < hi - lo:
        yield from _oddeven_merge(lo, hi, step)
        yield from _oddeven_merge(lo + r, hi, step)
        yield from [(i, i + r) for i in range(lo + r, hi - r, step)]
    else:
        yield (lo, lo + r)


def _oddeven_sort(lo, hi):
    if hi - lo >= 1:
        mid = lo + (hi - lo) // 2
        yield from _oddeven_sort(lo, mid)
        yield from _oddeven_sort(mid + 1, hi)
        yield from _oddeven_merge(lo, hi, 1)


SORT16 = list(_oddeven_sort(0, PEER_TOPK - 1))


def _colmax8(x):
    for sh in (1, 2, 4):
        x = jnp.maximum(x, pltpu.roll(x, sh, 0))
    return x


def _top16_rows(s):
    ls = [s[8 * r:8 * r + 8, :] for r in range(PEER_TOPK)]
    for i, j in SORT16:
        ls[i], ls[j] = jnp.maximum(ls[i], ls[j]), jnp.minimum(ls[i], ls[j])
    rows = []
    for r in range(PEER_TOPK):
        m = _colmax8(ls[0])
        rows.append(m[0:1, :])
        keep = PEER_TOPK - r - 1
        hit = ls[0] == m
        ls = [jnp.where(hit, ls[d + 1], ls[d]) for d in range(keep)]
    return rows


def _peer_route_kernel(x_ref, g_ref, mod_ref, wq_ref, keys_ref, xnt_ref, e1_ref, th_ref, s2_ref, e2_ref, xn_scr):
    h = pl.program_id(2)

    @pl.when(h == 0)
    def _():
        xn = _modnorm(x_ref[...], g_ref[...], mod_ref, 3, 4)
        xn_scr[...] = xn.astype(BF16)
        xnt_ref[...] = xn.T.astype(BF16)

    q = jnp.dot(xn_scr[...], wq_ref[...], preferred_element_type=F32).astype(BF16)
    half = PEER_DQ // 2
    s1 = _nt(keys_ref[0], q[:, :half])
    s2 = _nt(keys_ref[1], q[:, half:])
    top1 = _top16_rows(s1)
    top2 = _top16_rows(s2)
    pad_rows = [jnp.full_like(top1[0], -jnp.inf)] * (-len(STAIR) % 8)
    cand = jnp.concatenate([top1[a] + top2[b] for a, b in STAIR] + pad_rows, axis=0)
    thr = _top_rows(cand, PEER_TOPK)[-1]
    m1, m2 = top1[0], top2[0]
    zsum = jnp.sum(jnp.where(cand >= thr, jnp.exp(cand - (m1 + m2)), 0.0), axis=0, keepdims=True)
    e1_ref[...] = jnp.exp(s1 - m1) / zsum
    th_ref[...] = thr - s1
    s2_ref[...] = s2
    e2_ref[...] = jnp.exp(s2 - m2)


def _peer_route(x, g, mod, wq, keys):
    b, l, d = x.shape
    tm = min(256, l)
    nt = l // tm
    n = b * l
    hb = pl.BlockSpec((None, PEER_NKEYS, tm), lambda bi, i, h: (h, 0, bi * nt + i))
    return pl.pallas_call(
        _peer_route_kernel,
        grid=(b, nt, PEER_HEADS),
        in_specs=[
            pl.BlockSpec((None, tm, d), lambda bi, i, h: (bi, i, 0)),
            pl.BlockSpec((1, d), lambda bi, i, h: (0, 0)),
            pl.BlockSpec((None, 6, d), lambda bi, i, h: (bi, 0, 0)),
            pl.BlockSpec((d, PEER_DQ), lambda bi, i, h: (0, h)),
            pl.BlockSpec((None, 2, PEER_NKEYS, PEER_DQ // 2), lambda bi, i, h: (h, 0, 0, 0)),
        ],
        out_specs=[pl.BlockSpec((d, tm), lambda bi, i, h: (0, bi * nt + i)), hb, hb, hb, hb],
        out_shape=[jax.ShapeDtypeStruct((d, n), BF16)] + [jax.ShapeDtypeStruct((PEER_HEADS, PEER_NKEYS, n), F32)] * 4,
        scratch_shapes=[pltpu.VMEM((tm, d), BF16)],
        compiler_params=_cparams(("parallel", "parallel", "arbitrary")),
        name="peer_route",
    )(x, g.reshape(1, d), mod, wq, keys)


GELU_C = math.sqrt(2.0 / math.pi)
PEER_ROWS = 8
PEER_VMEM_LIMIT = 58 * 1024 * 1024
GATE_RB = 32
GATE_IG = 4
SCORE_ROWS, SCORE_COLS = 512, 256
ACCUM_DEPTH = 256


def _gelu_tanh(x):
    k = -2.0 * GELU_C * LOG2E
    return x / (1.0 + jnp.exp2(x * (k + (k * 0.044715) * (x * x))))


def _peer_pipe_kernel(*refs, final, n_pairs):
    refs = list(refs)
    x_ref, mod_ref, xnt_ref, e1_ref, th_ref, s2_ref, e2_ref, u_ref, vt_ref = refs[:9]
    pos = 9
    if final:
        fg_ref = refs[pos]
        pos += 1
    o_ref, acc_scr = refs[pos:pos + 2]
    a_scr = (refs[pos + 2:pos + 4], refs[pos + 4:pos + 6])
    wg_scr = (refs[pos + 6:pos + 8], refs[pos + 8:pos + 10])
    be1_scr = refs[pos + 10:pos + 12]
    bth_scr = refs[pos + 12:pos + 14]
    g = pl.program_id(2)
    eb = PEER_ROWS * PEER_NKEYS
    tm = s2_ref.shape[2]

    def score_tasks(p):
        tasks = []
        for k in range(2):
            for mh in range(0, eb, SCORE_ROWS):
                for nh in range(0, tm, SCORE_COLS):
                    def task(k=k, mh=mh, nh=nh):
                        a_scr[p][k][mh:mh + SCORE_ROWS, nh:nh + SCORE_COLS] = jnp.dot(
                            u_ref[k * eb + mh:k * eb + mh + SCORE_ROWS, :], xnt_ref[:, nh:nh + SCORE_COLS],
                            preferred_element_type=F32)
                    tasks.append(task)
        return tasks

    def accum_tasks(p):
        tasks = []
        for k in range(2):
            for kc in range(0, eb, ACCUM_DEPTH):
                def task(k=k, kc=kc):
                    acc_scr[...] += jnp.dot(vt_ref[:, k * eb + kc:k * eb + kc + ACCUM_DEPTH],
                                            wg_scr[1 - p][k][kc:kc + ACCUM_DEPTH, :], preferred_element_type=F32)
                tasks.append(task)
        return tasks

    def run_all(tasks):
        for t in tasks:
            t()

    def gate_one(a_ref, wg_ref, k, side):
        zero8 = jnp.zeros((8, tm), F32)
        for h in range(PEER_HEADS):
            for ii in range(PEER_ROWS):
                be1_scr[k][h * PEER_ROWS + ii] = e1_ref[h, pl.ds(k * PEER_ROWS + ii, 1), :] + zero8
                bth_scr[k][h * PEER_ROWS + ii] = th_ref[h, pl.ds(k * PEER_ROWS + ii, 1), :] + zero8
        for c0 in range(0, tm, LANES):
            cs = slice(c0, c0 + LANES)
            for r0 in range(0, PEER_NKEYS, GATE_RB):
                for t in side.pop(0):
                    t()
                for ig in range(0, PEER_ROWS, GATE_IG):
                    ws = [None] * GATE_IG
                    for h in range(PEER_HEADS):
                        s2 = s2_ref[h, r0:r0 + GATE_RB, cs].reshape(GATE_RB // 8, 8, LANES)
                        e2 = e2_ref[h, r0:r0 + GATE_RB, cs].reshape(GATE_RB // 8, 8, LANES)
                        for j in range(GATE_IG):
                            row = h * PEER_ROWS + ig + j
                            contrib = jnp.where(s2 >= bth_scr[k][row, :, cs][None], e2 * be1_scr[k][row, :, cs][None], 0.0)
                            ws[j] = contrib if ws[j] is None else ws[j] + contrib
                    for j in range(GATE_IG):
                        rs = slice((ig + j) * PEER_NKEYS + r0, (ig + j) * PEER_NKEYS + r0 + GATE_RB)
                        wj = ws[j].reshape(GATE_RB, LANES)
                        wg_ref[rs, cs] = (wj * _gelu_tanh(a_ref[rs, cs])).astype(BF16)

    def gates(p, tasks):
        points = 2 * (tm // LANES) * (PEER_NKEYS // GATE_RB)
        side = [[] for _ in range(points)]
        for n, t in enumerate(tasks):
            side[n * points // len(tasks)].append(t)
        for k in range(2):
            gate_one(a_scr[1 - p][k], wg_scr[p][k], k, side)

    def mix(xs, ys):
        out = []
        for i in range(max(len(xs), len(ys))):
            out += xs[i:i + 1] + ys[i:i + 1]
        return out

    @pl.when(g == 0)
    def _():
        acc_scr[...] = jnp.zeros_like(acc_scr)
        run_all(score_tasks(0))

    @pl.when(g == 1)
    def _():
        gates(1, score_tasks(1))

    for p in range(2):
        @pl.when((g >= 2) & (g < n_pairs) & (g % 2 == p))
        def _(p=p):
            gates(p, mix(score_tasks(p), accum_tasks(p)))

    @pl.when(g == n_pairs)
    def _():
        gates(n_pairs % 2, accum_tasks(n_pairs % 2))

    @pl.when(g == n_pairs + 1)
    def _():
        run_all(accum_tasks((n_pairs + 1) % 2))
        y = x_ref[...] + mod_ref[5:6, :] * acc_scr[...].T
        if final:
            ms = jnp.mean(y * y, axis=-1, keepdims=True)
            y = y * lax.rsqrt(ms + NORM_EPS) * fg_ref[...]
        o_ref[...] = y


def _peer_pipe(x, mod, xnt, e1, th, s2, e2, u, vt, final_g):
    b, l, d = x.shape
    tm = min(512, l)
    nt = l // tm
    eb = PEER_ROWS * PEER_NKEYS
    n_pairs = u.shape[0] // (2 * eb)
    assert n_pairs >= 2
    final = final_g is not None
    tok = lambda bi, i: bi * nt + i
    clamp = lambda v: jnp.clip(v, 0, n_pairs - 1)
    rows = pl.BlockSpec((PEER_HEADS, 2 * PEER_ROWS, tm), lambda bi, i, g: (0, clamp(g - 1), tok(bi, i)))
    keyblk = pl.BlockSpec((PEER_HEADS, PEER_NKEYS, tm), lambda bi, i, g: (0, 0, tok(bi, i)))
    in_specs = [
        pl.BlockSpec((None, tm, d), lambda bi, i, g: (bi, i, 0)),
        pl.BlockSpec((None, 6, d), lambda bi, i, g: (bi, 0, 0)),
        pl.BlockSpec((d, tm), lambda bi, i, g: (0, tok(bi, i))),
        rows, rows, keyblk, keyblk,
        pl.BlockSpec((2 * eb, d), lambda bi, i, g: (clamp(g), 0)),
        pl.BlockSpec((d, 2 * eb), lambda bi, i, g: (0, clamp(g - 2))),
    ]
    args = [x, mod, xnt, e1, th, s2, e2, u, vt]
    if final:
        in_specs.append(pl.BlockSpec((1, d), lambda bi, i, g: (0, 0)))
        args.append(final_g.reshape(1, d))
    bcast = pltpu.VMEM((PEER_HEADS * PEER_ROWS, 8, tm), F32)
    return pl.pallas_call(
        functools.partial(_peer_pipe_kernel, final=final, n_pairs=n_pairs),
        grid=(b, nt, n_pairs + 2),
        in_specs=in_specs,
        out_specs=pl.BlockSpec((None, tm, d), lambda bi, i, g: (bi, i, 0)),
        out_shape=jax.ShapeDtypeStruct((b, l, d), F32),
        scratch_shapes=([pltpu.VMEM((d, tm), F32)] + [pltpu.VMEM((eb, tm), F32)] * 4 + [pltpu.VMEM((eb, tm), BF16)] * 4
                        + [bcast] * 4),
        compiler_params=_cparams(("parallel", "parallel", "arbitrary"), PEER_VMEM_LIMIT),
        name="peer_experts",
    )(*args)


def _pad_cols(a, width):
    return jnp.pad(a, ((0, 0), (0, width - a.shape[1])))


def _layout_w_in(w):
    d = w.shape[0]
    z = lambda n: jnp.zeros((d, n), w.dtype)
    gla, hg, mla, dif = 0, 800, 2080, 2432
    parts = [w[:, gla:gla + 768], w[:, hg:hg + 768],
             w[:, mla:mla + 192], z(64), w[:, mla + 192:mla + 320], z(64), w[:, mla + 320:mla + 352], z(32)]
    for base in (dif, dif + 256, dif + 512):
        for h in range(DIFF_HEADS):
            parts += [w[:, base + 64 * h:base + 64 * (h + 1)], z(64)]
    parts += [w[:, gla + 768:gla + 800], z(224), w[:, hg + 768:hg + 1280]]
    out = jnp.concatenate(parts, axis=1)
    assert out.shape[1] == MAIN_COLS + GATE_COLS
    return out.astype(BF16)


def _layout_mla(w_uq, w_ukv):
    dq = MLA_NOPE + MLA_ROPE
    wq = jnp.concatenate([_pad_cols(w_uq[:, dq * h:dq * (h + 1)], LANES) for h in range(MLA_HEADS)], axis=1)
    wq = jnp.pad(wq, ((0, 256 - MLA_Q_RANK), (0, 0)))
    per = MLA_NOPE + MLA_DV
    wk = jnp.concatenate([_pad_cols(w_ukv[:, per * h:per * h + MLA_NOPE], LANES) for h in range(MLA_HEADS)], axis=1)
    wv = jnp.concatenate([_pad_cols(w_ukv[:, per * h + MLA_NOPE:per * (h + 1)], LANES) for h in range(MLA_HEADS)], axis=1)
    return wq.astype(BF16), jnp.concatenate([wk, wv], axis=1).astype(BF16)


def kernel(x, c, ctx, c_ctx, ada_w, ada_b, norm_mix_g, norm_ffn_g, w_in, w_out, gla_gate_w, gla_gate_b, gla_norm_g, hgrn_lb_raw, hgrn_norm_g, mla_q_norm_g, mla_kv_norm_g, mla_w_uq, mla_w_ukv, diff_lambda, diff_norm_g, peer_wq, peer_keys, peer_u, peer_v, final_norm_g):
    b, l, d = x.shape
    lc = ctx.shape[1]
    depth = ada_w.shape[0]
    assert lc <= ATT_TK, "context keys are consumed as one attention step"

    rows = -(-(b + 1) // 8) * 8
    cpad = jnp.zeros((rows, d), F32).at[:b].set(c).at[b].set(c_ctx)
    mod_all = _adaln(cpad, ada_w, ada_b)

    lb_sm = jax.nn.softmax(hgrn_lb_raw.astype(F32), axis=0)
    lb_all = jnp.cumsum(lb_sm, axis=0) - lb_sm[0]

    lat_mla_tab = _rope_tables(l, 64, 32, True)
    ctx_mla_tab = _rope_tables(lc, 64, 32, False)
    lat_dif_tab = _rope_tables(l, 0, 64, True)
    ctx_dif_tab = _rope_tables(lc, 0, 64, False)

    xc = ctx
    for li in range(depth):
        need_ctx = li < depth - 1
        lam_init = 0.8 - 0.6 * math.exp(-0.3 * li)
        mod_l = mod_all[li, :b].reshape(b, 6, d)
        mod_c = jnp.broadcast_to(mod_all[li, b].reshape(1, 6, d), (b, 6, d))

        w_perm = _layout_w_in(w_in[li])
        w_o = w_out[li].astype(BF16)
        wg = jnp.zeros((2, 256, 128), F32)
        wg = wg.at[0, 0:GLA_GATE_RANK].set(gla_gate_w[li, 0]).at[1, GLA_GATE_RANK:2 * GLA_GATE_RANK].set(gla_gate_w[li, 1])
        wg = wg.astype(BF16)
        bg = gla_gate_b[li].reshape(2, 1, 128)
        gla_gn = jnp.tile(gla_norm_g[li], GLA_HEADS).reshape(1, 256)
        hg_gn = jnp.tile(hgrn_norm_g[li], HG_HEADS).reshape(1, 256)
        lb = lb_all[li]
        hg_p = jnp.zeros((8, 256), F32).at[0].set(jnp.log(jnp.maximum(lb, LB_FLOOR))).at[1].set(jnp.log1p(-lb)).at[2].set(1.0 - lb)
        hg_dummy = jnp.zeros((8, 128), F32)
        gq = jnp.pad(mla_q_norm_g[li], (0, 256 - MLA_Q_RANK)).reshape(1, 256)
        gkv = mla_kv_norm_g[li].reshape(1, MLA_KV_RANK)
        wq_mla, wkv_mla = _layout_mla(mla_w_uq[li], mla_w_ukv[li])
        lq1, lk1, lq2, lk2 = diff_lambda[li, 0], diff_lambda[li, 1], diff_lambda[li, 2], diff_lambda[li, 3]
        lam = (jnp.exp(jnp.sum(lq1 * lk1).astype(F32)) - jnp.exp(jnp.sum(lq2 * lk2).astype(F32)) + lam_init)
        lam_row = jnp.full((1, LANES), lam, F32)
        dif_gn = jnp.tile(diff_norm_g[li], 2).reshape(1, LANES)
        wq_peer = peer_wq[li].astype(BF16)
        keys = peer_keys[li].astype(BF16)
        u_bf = peer_u[li].astype(BF16)
        vt_bf = peer_v[li].astype(BF16).T

        main_c, gate_c = _inproj(xc, norm_mix_g[li], mod_c, w_perm)
        main_l, gate_l = _inproj(x, norm_mix_g[li], mod_l, w_perm)

        mixes_c, mixes_l = [], []
        for mixer, pa_f, pb_f, pa_b, pb_b, gn, dk in (
            ("gla", wg[0], bg[0], wg[1], bg[1], gla_gn, 128),
            ("hg", hg_p, hg_dummy, hg_p, hg_dummy, hg_gn, 256),
        ):
            zero = jnp.zeros((b, 256, dk), F32)
            o_cf, s_f = _scan(main_c, gate_c, pa_f, pb_f, gn, zero, None, mixer=mixer, reverse=False, chunk=64)
            mix_c, s_b = _scan(main_c, gate_c, pa_b, pb_b, gn, zero, o_cf, mixer=mixer, reverse=True, chunk=64)
            o_lf, _ = _scan(main_l, gate_l, pa_f, pb_f, gn, s_f, None, mixer=mixer, reverse=False, chunk=64)
            mix_l, _ = _scan(main_l, gate_l, pa_b, pb_b, gn, s_b, o_lf, mixer=mixer, reverse=True, chunk=64)
            mixes_c.append(mix_c)
            mixes_l.append(mix_l)

        qm_c, km_c, vm_c = _mla_prep(main_c, gq, gkv, wq_mla, wkv_mla, ctx_mla_tab)
        qm_l, km_l, vm_l = _mla_prep(main_l, gq, gkv, wq_mla, wkv_mla, lat_mla_tab)
        mla_l = _attention(qm_l, km_c, vm_c, km_l, vm_l, None, n_maps=1, lam_init=lam_init)
        qd_c, kd_c, vd_c = _diff_prep(main_c, ctx_dif_tab)
        qd_l, kd_l, vd_l = _diff_prep(main_l, lat_dif_tab)
        dif_l = _attention(qd_l, kd_c, vd_c, kd_l, vd_l, (lam_row, dif_gn), n_maps=2, lam_init=lam_init)

        x = _outproj(x, mixes_l[0], mixes_l[1], mla_l, dif_l, w_o, mod_l)
        rt = _peer_route(x, norm_ffn_g[li], mod_l, wq_peer, keys)
        x = _peer_pipe(x, mod_l, *rt, u_bf, vt_bf, None if need_ctx else final_norm_g)

        if need_ctx:
            mla_c = _attention(qm_c, km_c, vm_c, None, None, None, n_maps=1, lam_init=lam_init)
            dif_c = _attention(qd_c, kd_c, vd_c, None, None, (lam_row, dif_gn), n_maps=2, lam_init=lam_init)
            xc = _outproj(xc, mixes_c[0], mixes_c[1], mla_c, dif_c, w_o, mod_c)
            rtc = _peer_route(xc, norm_ffn_g[li], mod_c, wq_peer, keys)
            xc = _peer_pipe(xc, mod_c, *rtc, u_bf, vt_bf, None)
    return x
```

```python
import functools
import math

import jax
import jax.numpy as jnp
import numpy as np
from jax import lax
from jax.experimental import pallas as pl
from jax.experimental.pallas import tpu as pltpu

F32 = jnp.float32
BF16 = jnp.bfloat16

NORM_EPS = 1e-6
ROPE_BASE = 10000.0
GRID_W = 64
LB_FLOOR = 1e-30

GLA_HEADS, GLA_DK, GLA_DV, GLA_GATE_RANK, GLA_GATE_NORM = 4, 32, 64, 16, 16.0
HG_HEADS, HG_DIM = 4, 64
MLA_HEADS, MLA_Q_RANK, MLA_KV_RANK, MLA_NOPE, MLA_ROPE, MLA_DV = 4, 192, 128, 64, 32, 64
DIFF_HEADS, DIFF_DQK, DIFF_DV = 4, 32, 64
PEER_HEADS, PEER_NKEYS, PEER_DQ, PEER_TOPK = 8, 128, 256, 16

LANES = 128
VMEM_LIMIT = 48 * 1024 * 1024
EXP_CAP = 80.0
NEG_BIG = -1e30
ATT_TK = 512
LOG2E = 1.4426950408889634
ATT_R = 512
ATT_STEPS_PER_TRIP = 4
VT_ROWS = 80

COL_GLA, COL_HG, COL_MLA, COL_DQ, COL_DK, COL_DV, MAIN_COLS = 0, 768, 1536, 2048, 2560, 3072, 3584
GATE_COLS = 768

STAIR = [(a, b) for a in range(PEER_TOPK) for b in range(PEER_TOPK) if (a + 1) * (b + 1) <= PEER_TOPK]


def _cparams(sem, vmem=VMEM_LIMIT):
    return pltpu.CompilerParams(dimension_semantics=sem, vmem_limit_bytes=vmem)


def _sigmoid(x):
    return 1.0 / (1.0 + jnp.exp(-x))


def _logsig(x):
    return jnp.minimum(x, 0.0) - jnp.log(1.0 + jnp.exp(-jnp.abs(x)))


def _nt(a, b):
    return lax.dot_general(a, b, (((1,), (1,)), ((), ())), preferred_element_type=F32)


def _tn(a, b):
    return lax.dot_general(a, b, (((0,), (0,)), ((), ())), preferred_element_type=F32)


def _adaln_kernel(c_ref, w_ref, b_ref, o_ref):
    c = c_ref[...]
    s = c * _sigmoid(c)
    o_ref[...] = jnp.dot(s.astype(BF16), w_ref[...].astype(BF16), preferred_element_type=F32) + b_ref[...]


def _adaln(cpad, ada_w, ada_b):
    depth, d, n6 = ada_w.shape
    rows = cpad.shape[0]
    tn = 512
    return pl.pallas_call(
        _adaln_kernel,
        grid=(depth, n6 // tn),
        in_specs=[
            pl.BlockSpec((rows, d), lambda l, j: (0, 0)),
            pl.BlockSpec((None, d, tn), lambda l, j: (l, 0, j)),
            pl.BlockSpec((None, 1, tn), lambda l, j: (l, 0, j)),
        ],
        out_specs=pl.BlockSpec((None, rows, tn), lambda l, j: (l, 0, j)),
        out_shape=jax.ShapeDtypeStruct((depth, rows, n6), F32),
        compiler_params=_cparams(("parallel", "parallel")),
        name="adaln",
    )(cpad, ada_w, ada_b.reshape(depth, 1, n6))


def _modnorm(x, g, mod_ref, shift_row, scale_row):
    ms = jnp.mean(x * x, axis=-1, keepdims=True)
    h = x * lax.rsqrt(ms + NORM_EPS) * g
    return h * (1.0 + mod_ref[scale_row:scale_row + 1, :]) + mod_ref[shift_row:shift_row + 1, :]


def _inproj_kernel(x_ref, g_ref, mod_ref, w_ref, main_ref, gate_ref):
    h = _modnorm(x_ref[...], g_ref[...], mod_ref, 0, 1)
    y = jnp.dot(h.astype(BF16), w_ref[...], preferred_element_type=F32)
    main_ref[...] = y[:, :MAIN_COLS].astype(BF16)
    gate_ref[...] = y[:, MAIN_COLS:]


def _inproj(x, g, mod, w_perm):
    b, l, d = x.shape
    tm = min(256, l)
    nc = w_perm.shape[1]
    return pl.pallas_call(
        _inproj_kernel,
        grid=(b, l // tm),
        in_specs=[
            pl.BlockSpec((None, tm, d), lambda bi, i: (bi, i, 0)),
            pl.BlockSpec((1, d), lambda bi, i: (0, 0)),
            pl.BlockSpec((None, 6, d), lambda bi, i: (bi, 0, 0)),
            pl.BlockSpec((d, nc), lambda bi, i: (0, 0)),
        ],
        out_specs=[
            pl.BlockSpec((None, tm, MAIN_COLS), lambda bi, i: (bi, i, 0)),
            pl.BlockSpec((None, tm, GATE_COLS), lambda bi, i: (bi, i, 0)),
        ],
        out_shape=[
            jax.ShapeDtypeStruct((b, l, MAIN_COLS), BF16),
            jax.ShapeDtypeStruct((b, l, GATE_COLS), F32),
        ],
        compiler_params=_cparams(("parallel", "parallel")),
        name="inproj",
    )(x, g.reshape(1, d), mod, w_perm)


def _scan_kernel(*refs, mixer, reverse, finish, chunk):
    if finish:
        main_ref, gate_ref, pa_ref, pb_ref, gn_ref, s0_ref, oprev_ref, o_ref, sfin_ref, st_scr = refs
    else:
        main_ref, gate_ref, pa_ref, pb_ref, gn_ref, s0_ref, o_ref, sfin_ref, st_scr = refs
        oprev_ref = None
    i = pl.program_id(1)

    @pl.when(i == 0)
    def _():
        st_scr[...] = s0_ref[...]

    main = main_ref[...]
    t = main.shape[0]
    c = chunk
    nc = t // c
    heads = 4
    dv = 256
    v = main[:, 256:512]
    gcol = main[:, 512:768].astype(F32)
    if mixer == "gla":
        dk = 128
        q = main[:, 0:128].astype(F32) * (GLA_DK ** -0.5)
        k = main[:, 128:256].astype(F32)
        z = jnp.dot(gate_ref[...].astype(BF16), pa_ref[...], preferred_element_type=F32) + pb_ref[...]
        g = _logsig(z) * (1.0 / GLA_GATE_NORM)
    else:
        dk = 256
        qq = main[:, 0:256].astype(F32)
        q = qq * _sigmoid(qq)
        z = gate_ref[...]
        la = pa_ref[0:1, :]
        lb = pa_ref[1:2, :] + _logsig(z)
        g = jnp.maximum(la, lb) + jnp.log(1.0 + jnp.exp(-jnp.abs(la - lb)))
        k = pa_ref[2:3, :] * _sigmoid(-z)
    hk = dk // heads
    hv = dv // heads

    rt = lax.broadcasted_iota(jnp.int32, (t, t), 0)
    ct = lax.broadcasted_iota(jnp.int32, (t, t), 1)
    same = (rt // c) == (ct // c)
    tri = jnp.where(same & ((ct >= rt) if reverse else (ct <= rt)), 1.0, 0.0).astype(BF16)
    g1 = g.astype(BF16)
    r1 = g - g1.astype(F32)
    g2 = r1.astype(BF16)
    g3 = (r1 - g2.astype(F32)).astype(BF16)
    bcum = (jnp.dot(tri, g1, preferred_element_type=F32) + jnp.dot(tri, g2, preferred_element_type=F32)
            + jnp.dot(tri, g3, preferred_element_type=F32))

    b3 = bcum.reshape(nc, c, dk)
    q3 = q.reshape(nc, c, dk)
    k3 = k.reshape(nc, c, dk)
    if reverse:
        bmid = b3[:, c // 2:c // 2 + 1, :]
        bend = b3[:, 0:1, :]
    else:
        bmid = b3[:, c // 2 - 1:c // 2, :]
        bend = b3[:, c - 1:c, :]
    eb = b3 - bmid
    qa = (q3 * jnp.exp(jnp.minimum(eb, EXP_CAP))).reshape(t, dk).astype(BF16)
    ka = (k3 * jnp.exp(jnp.minimum(-eb, EXP_CAP))).reshape(t, dk).astype(BF16)
    qi = (q3 * jnp.exp(b3)).reshape(t, dk).astype(BF16)
    kb = (k3 * jnp.exp(bend - b3)).reshape(t, dk).astype(BF16)
    dec = jnp.exp(bend)

    lane_k = lax.broadcasted_iota(jnp.int32, (1, dk), 1) // hk
    lane_v = lax.broadcasted_iota(jnp.int32, (1, dv), 1) // hv
    rr = lax.broadcasted_iota(jnp.int32, (c, heads * c), 0)
    cc = lax.broadcasted_iota(jnp.int32, (c, heads * c), 1) % c
    causal = (cc >= rr) if reverse else (cc <= rr)
    bd = (lax.broadcasted_iota(jnp.int32, (dv, dk), 0) // hv) == (lax.broadcasted_iota(jnp.int32, (dv, dk), 1) // hk)

    o_intra = []
    upd = []
    for ci in range(nc):
        sl = slice(ci * c, (ci + 1) * c)
        ka_c = ka[sl]
        v_c = v[sl]
        kst = jnp.concatenate([jnp.where(lane_k == h, ka_c, 0) for h in range(heads)], axis=0)
        vst = jnp.concatenate([jnp.where(lane_v == h, v_c, 0) for h in range(heads)], axis=0)
        sw = _nt(qa[sl], kst)
        p = jnp.where(causal, sw, 0.0).astype(BF16)
        o_intra.append(jnp.dot(p, vst, preferred_element_type=F32))
        upd.append(jnp.where(bd, _tn(v_c, kb[sl]), 0.0))

    outs = [None] * nc
    order = range(nc - 1, -1, -1) if reverse else range(nc)
    for ci in order:
        sl = slice(ci * c, (ci + 1) * c)
        st = st_scr[...]
        outs[ci] = o_intra[ci] + _nt(qi[sl], st.astype(BF16))
        st_scr[...] = st * dec[ci] + upd[ci]
    o = jnp.concatenate(outs, axis=0)

    @pl.when(i == pl.num_programs(1) - 1)
    def _():
        sfin_ref[...] = st_scr[...]

    if not finish:
        o_ref[...] = o
    else:
        ot = o + oprev_ref[...]
        sq = ot * ot
        jv = ((lax.broadcasted_iota(jnp.int32, (dv, dv), 0) // hv)
              == (lax.broadcasted_iota(jnp.int32, (dv, dv), 1) // hv))
        jm = jnp.where(jv, 1.0, 0.0).astype(BF16)
        s1 = sq.astype(BF16)
        s2 = (sq - s1.astype(F32)).astype(BF16)
        ms = (jnp.dot(s1, jm, preferred_element_type=F32) + jnp.dot(s2, jm, preferred_element_type=F32)) * (1.0 / hv)
        y = ot * lax.rsqrt(ms + NORM_EPS) * gn_ref[...]
        o_ref[...] = (y * (gcol * _sigmoid(gcol))).astype(BF16)


def _scan(main, gate, pa, pb, gn, s0, oprev, *, mixer, reverse, chunk):
    b, l, _ = main.shape
    t = min(256, l)
    nb = l // t
    finish = oprev is not None
    dk = 128 if mixer == "gla" else 256
    dv = 256
    main_blk = 0 if mixer == "gla" else 1
    gate_blk = 0 if mixer == "gla" else (2 if reverse else 1)

    def tok(bi, i):
        return (nb - 1 - i) if reverse else i

    in_specs = [
        pl.BlockSpec((None, t, 768), lambda bi, i: (bi, tok(bi, i), main_blk)),
        pl.BlockSpec((None, t, 256), lambda bi, i: (bi, tok(bi, i), gate_blk)),
        pl.BlockSpec(pa.shape, lambda bi, i: (0, 0)),
        pl.BlockSpec(pb.shape, lambda bi, i: (0, 0)),
        pl.BlockSpec((1, dv), lambda bi, i: (0, 0)),
        pl.BlockSpec((None, dv, dk), lambda bi, i: (bi, 0, 0)),
    ]
    args = [main, gate, pa, pb, gn, s0]
    if finish:
        in_specs.append(pl.BlockSpec((None, t, dv), lambda bi, i: (bi, tok(bi, i), 0)))
        args.append(oprev)
    return pl.pallas_call(
        functools.partial(_scan_kernel, mixer=mixer, reverse=reverse, finish=finish, chunk=chunk),
        grid=(b, nb),
        in_specs=in_specs,
        out_specs=[
            pl.BlockSpec((None, t, dv), lambda bi, i: (bi, tok(bi, i), 0)),
            pl.BlockSpec((None, dv, dk), lambda bi, i: (bi, 0, 0)),
        ],
        out_shape=[
            jax.ShapeDtypeStruct((b, l, dv), BF16 if finish else F32),
            jax.ShapeDtypeStruct((b, dv, dk), F32),
        ],
        scratch_shapes=[pltpu.VMEM((dv, dk), F32)],
        compiler_params=_cparams(("parallel", "arbitrary")),
        name=f"scan_{mixer}_{'bwd' if reverse else 'fwd'}",
    )(*args)


def _rope(x, cos, sin_lo, sin_hi):
    return x * cos + pltpu.roll(x, LANES - 8, 1) * sin_lo + pltpu.roll(x, 8, 1) * sin_hi


def _rope_tables(l, first_lane, n_lanes, rotate):
    lane = np.arange(LANES)
    d = (lane - first_lane) % 32
    active = (lane >= first_lane) & (lane < first_lane + n_lanes)
    freqs = ROPE_BASE ** (-(d % 8).astype(np.float32) / 8.0)
    tt = jnp.arange(l, dtype=jnp.int32)
    rows = (tt // GRID_W).astype(F32)[:, None]
    cols = (tt % GRID_W).astype(F32)[:, None]
    pos = jnp.where(jnp.asarray(d < 16)[None, :], rows, cols)
    ang = pos * jnp.asarray(freqs, F32)[None, :]
    act = jnp.asarray(active)[None, :] & rotate
    lo = jnp.asarray((d % 16) < 8)[None, :]
    cos = jnp.where(act, jnp.cos(ang), 1.0)
    sin = jnp.where(act, jnp.sin(ang), 0.0)
    return cos, jnp.where(lo, -sin, 0.0), jnp.where(lo, 0.0, sin)


def _vt_block(vgrp):
    lane = lax.broadcasted_iota(jnp.int32, (1, LANES), 1)
    return jnp.where(lane == 64, 1.0, vgrp).T[:VT_ROWS].astype(BF16)


def _mla_prep_kernel(slab_ref, gq_ref, gkv_ref, wq_ref, wkv_ref, cos_ref, slo_ref, shi_ref, qt_ref, k_ref, vt_ref):
    slab = slab_ref[...]
    cos, slo, shi = cos_ref[...], slo_ref[...], shi_ref[...]
    cq = slab[:, 0:256].astype(F32)
    qn = cq * lax.rsqrt(jnp.sum(cq * cq, axis=-1, keepdims=True) * (1.0 / MLA_Q_RANK) + NORM_EPS) * gq_ref[...]
    qall = jnp.dot(qn.astype(BF16), wq_ref[...], preferred_element_type=F32)
    ckv = slab[:, 256:384].astype(F32)
    kvn = ckv * lax.rsqrt(jnp.mean(ckv * ckv, axis=-1, keepdims=True) + NORM_EPS) * gkv_ref[...]
    kvall = jnp.dot(kvn.astype(BF16), wkv_ref[...], preferred_element_type=F32)
    kr = _rope(slab[:, 384:512].astype(F32), cos, slo, shi)
    scale = (MLA_NOPE + MLA_ROPE) ** -0.5 * LOG2E
    for h in range(MLA_HEADS):
        qh = _rope(qall[:, h * LANES:(h + 1) * LANES], cos, slo, shi)
        qt_ref[h] = (qh * scale).T.astype(BF16)
        k_ref[h] = (kvall[:, h * LANES:(h + 1) * LANES] + kr).astype(BF16)
        vt_ref[h] = _vt_block(kvall[:, (MLA_HEADS + h) * LANES:(MLA_HEADS + h + 1) * LANES])


def _attn_prep_specs(b, l, tm, heads):
    qt = pl.BlockSpec((None, heads, LANES, tm), lambda bi, i: (bi, 0, 0, i))
    kk = pl.BlockSpec((None, heads, tm, LANES), lambda bi, i: (bi, 0, i, 0))
    vt = pl.BlockSpec((None, heads, None, VT_ROWS, tm), lambda bi, i: (bi, 0, i, 0, 0))
    shapes = [
        jax.ShapeDtypeStruct((b, heads, LANES, l), BF16),
        jax.ShapeDtypeStruct((b, heads, l, LANES), BF16),
        jax.ShapeDtypeStruct((b, heads, l // tm, VT_ROWS, tm), BF16),
    ]
    return [qt, kk, vt], shapes


def _mla_prep(main, gq, gkv, wq, wkv, tables):
    b, l, _ = main.shape
    tm = min(ATT_TK, l)
    cos, slo, shi = tables
    tab = pl.BlockSpec((tm, LANES), lambda bi, i: (i, 0))
    out_specs, out_shape = _attn_prep_specs(b, l, tm, MLA_HEADS)
    return pl.pallas_call(
        _mla_prep_kernel,
        grid=(b, l // tm),
        in_specs=[
            pl.BlockSpec((None, tm, 512), lambda bi, i: (bi, i, COL_MLA // 512)),
            pl.BlockSpec(gq.shape, lambda bi, i: (0, 0)),
            pl.BlockSpec(gkv.shape, lambda bi, i: (0, 0)),
            pl.BlockSpec(wq.shape, lambda bi, i: (0, 0)),
            pl.BlockSpec(wkv.shape, lambda bi, i: (0, 0)),
            tab, tab, tab,
        ],
        out_specs=out_specs,
        out_shape=out_shape,
        compiler_params=_cparams(("parallel", "parallel")),
        name="mla_prep",
    )(main, gq, gkv, wq, wkv, cos, slo, shi)


def _diff_prep_kernel(q_in, k_in, v_in, cos_ref, slo_ref, shi_ref, qt_ref, k_ref, vt_ref):
    cos, slo, shi = cos_ref[...], slo_ref[...], shi_ref[...]
    qs = q_in[...]
    ks = k_in[...]
    vs = v_in[...]
    scale = DIFF_DQK ** -0.5 * LOG2E
    for h in range(DIFF_HEADS):
        sl = slice(h * LANES, (h + 1) * LANES)
        qt_ref[h] = (_rope(qs[:, sl].astype(F32), cos, slo, shi) * scale).T.astype(BF16)
        k_ref[h] = _rope(ks[:, sl].astype(F32), cos, slo, shi).astype(BF16)
        vt_ref[h] = _vt_block(vs[:, sl].astype(F32))


def _diff_prep(main, tables):
    b, l, _ = main.shape
    tm = min(ATT_TK, l)
    cos, slo, shi = tables
    tab = pl.BlockSpec((tm, LANES), lambda bi, i: (i, 0))
    out_specs, out_shape = _attn_prep_specs(b, l, tm, DIFF_HEADS)
    return pl.pallas_call(
        _diff_prep_kernel,
        grid=(b, l // tm),
        in_specs=[
            pl.BlockSpec((None, tm, 512), lambda bi, i: (bi, i, COL_DQ // 512)),
            pl.BlockSpec((None, tm, 512), lambda bi, i: (bi, i, COL_DK // 512)),
            pl.BlockSpec((None, tm, 512), lambda bi, i: (bi, i, COL_DV // 512)),
            tab, tab, tab,
        ],
        out_specs=out_specs,
        out_shape=out_shape,
        compiler_params=_cparams(("parallel", "parallel")),
        name="diff_prep",
    )(main, main, main, cos, slo, shi)


def _attn_kernel(*refs, n_maps, has_lat, lam_init):
    refs = list(refs)
    qt_ref, kc_ref, vtc_ref = refs[:3]
    pos = 3
    if has_lat:
        kl_ref, vtl_ref = refs[pos:pos + 2]
        pos += 2
    if n_maps == 2:
        lam_ref, gn_ref = refs[pos:pos + 2]
        pos += 2
    o_ref = refs[pos]
    if has_lat:
        m_scr, acc_scr, s_scr, cm_scr, al_scr, p_scr = refs[pos + 1:pos + 7]
    tq = qt_ref.shape[2]
    feat = lax.broadcasted_iota(jnp.int32, (LANES, 1), 0)

    qts = []
    for hh in range(2):
        qt = qt_ref[hh]
        if n_maps == 2:
            zero = jnp.zeros_like(qt)
            qt = jnp.concatenate(
                [jnp.where((feat >= DIFF_DQK * mi) & (feat < DIFF_DQK * (mi + 1)), qt, zero) for mi in range(2)], axis=1)
        qts.append(qt)

    accs = []
    for hh in range(2):
        st = jnp.dot(kc_ref[hh], qts[hh], preferred_element_type=F32)
        m0 = jnp.max(st, axis=0, keepdims=True)
        acc0 = jnp.dot(vtc_ref[hh, 0], jnp.exp2(st - m0).astype(BF16), preferred_element_type=F32)
        accs.append(acc0)
        if has_lat:
            m_scr[hh] = m0
            acc_scr[hh] = acc0

    if has_lat:
        n = vtl_ref.shape[1]
        tk = vtl_ref.shape[3]

        def scores(j, slot):
            off = pl.multiple_of(j * tk, tk)
            for hh in range(2):
                st = jnp.dot(kl_ref[hh, pl.ds(off, tk), :], qts[hh], preferred_element_type=F32)
                s_scr[hh, slot] = st
                cm_scr[hh, slot] = jnp.max(st, axis=0, keepdims=True)

        def numer(slot):
            for hh in range(2):
                m_old = m_scr[hh]
                m_new = jnp.maximum(m_old, cm_scr[hh, slot])
                al_scr[hh, slot] = jnp.exp2(m_old - m_new)
                p_scr[hh, slot] = jnp.exp2(s_scr[hh, slot] - m_new).astype(BF16)
                m_scr[hh] = m_new

        def values(j, slot):
            for hh in range(2):
                acc_scr[hh] = (al_scr[hh, slot] * acc_scr[hh]
                               + jnp.dot(vtl_ref[hh, j], p_scr[hh, slot], preferred_element_type=F32))

        first = n % 2
        if first:
            scores(0, 0)
            numer(0)
            values(0, 0)
        if n > first:
            for hh in range(2):
                p_scr[hh, 1] = jnp.zeros(p_scr.shape[2:], BF16)
                al_scr[hh, 1] = jnp.ones(al_scr.shape[2:], F32)
            scores(first, 0)

            per_trip = ATT_STEPS_PER_TRIP if (n - first) % ATT_STEPS_PER_TRIP == 0 else 2

            def body(t, carry):
                for u in range(per_trip):
                    c = first + per_trip * t + u
                    slot = u % 2
                    scores(jnp.minimum(c + 1, n - 1), 1 - slot)
                    numer(slot)
                    values(jnp.maximum(c - 1, first), 1 - slot)
                return carry

            lax.fori_loop(0, (n - first) // per_trip, body, 0)
            values(n - 1, 1)
        accs = [acc_scr[0], acc_scr[1]]

    outs = []
    for hh in range(2):
        acc = accs[hh]
        ot = acc[0:64, :] / acc[64:65, :]
        if n_maps == 2:
            oh = ot[:, :tq] - lam_ref[:, 0:1] * ot[:, tq:]
            ms = jnp.mean(oh * oh, axis=0, keepdims=True)
            ot = oh * lax.rsqrt(ms + NORM_EPS)
        outs.append(ot)
    o = jnp.concatenate(outs, axis=0).T
    if n_maps == 2:
        o = o * gn_ref[...] * (1.0 - lam_init)
    o_ref[...] = o.astype(BF16)


def _attention(qt, kc, vtc, kl, vtl, extra, *, n_maps, lam_init):
    b, nh, _, l = qt.shape
    lc = kc.shape[2]
    tq = min(ATT_R // n_maps, l)
    r = n_maps * tq
    has_lat = kl is not None
    scratch = []
    in_specs = [
        pl.BlockSpec((None, 2, LANES, tq), lambda bi, hp, i: (bi, hp, 0, i)),
        pl.BlockSpec((None, 2, lc, LANES), lambda bi, hp, i: (bi, hp, 0, 0)),
        pl.BlockSpec((None, 2) + vtc.shape[2:], lambda bi, hp, i: (bi, hp, 0, 0, 0)),
    ]
    args = [qt, kc, vtc]
    if has_lat:
        in_specs += [
            pl.BlockSpec((None, 2, kl.shape[2], LANES), lambda bi, hp, i: (bi, hp, 0, 0)),
            pl.BlockSpec((None, 2) + vtl.shape[2:], lambda bi, hp, i: (bi, hp, 0, 0, 0)),
        ]
        args += [kl, vtl]
        tk = vtl.shape[4]
        scratch = [
            pltpu.VMEM((2, 1, r), F32), pltpu.VMEM((2, VT_ROWS, r), F32),
            pltpu.VMEM((2, 2, tk, r), F32), pltpu.VMEM((2, 2, 1, r), F32), pltpu.VMEM((2, 2, 1, r), F32),
            pltpu.VMEM((2, 2, tk, r), BF16),
        ]
    if n_maps == 2:
        in_specs += [pl.BlockSpec((1, LANES), lambda bi, hp, i: (0, 0))] * 2
        args += list(extra)
    return pl.pallas_call(
        functools.partial(_attn_kernel, n_maps=n_maps, has_lat=has_lat, lam_init=lam_init),
        grid=(b, nh // 2, l // tq),
        in_specs=in_specs,
        out_specs=pl.BlockSpec((None, tq, LANES), lambda bi, hp, i: (bi, i, hp)),
        out_shape=jax.ShapeDtypeStruct((b, l, nh * 64), BF16),
        scratch_shapes=scratch,
        compiler_params=_cparams(("parallel", "parallel", "arbitrary")),
        name="attn_mla" if n_maps == 1 else "attn_diff",
    )(*args)


def _outproj_kernel(x_ref, a_ref, b_ref, m_ref, d_ref, w_ref, mod_ref, o_ref):
    mix = jnp.concatenate([a_ref[...], b_ref[...], m_ref[...], d_ref[...]], axis=-1)
    y = jnp.dot(mix, w_ref[...], preferred_element_type=F32)
    o_ref[...] = x_ref[...] + mod_ref[2:3, :] * y


def _outproj(x, a, bb, m, dd, w, mod):
    b, l, d = x.shape
    tm = min(512, l)
    slab = pl.BlockSpec((None, tm, 256), lambda bi, i: (bi, i, 0))
    return pl.pallas_call(
        _outproj_kernel,
        grid=(b, l // tm),
        in_specs=[
            pl.BlockSpec((None, tm, d), lambda bi, i: (bi, i, 0)),
            slab, slab, slab, slab,
            pl.BlockSpec(w.shape, lambda bi, i: (0, 0)),
            pl.BlockSpec((None, 6, d), lambda bi, i: (bi, 0, 0)),
        ],
        out_specs=pl.BlockSpec((None, tm, d), lambda bi, i: (bi, i, 0)),
        out_shape=jax.ShapeDtypeStruct((b, l, d), F32),
        compiler_params=_cparams(("parallel", "parallel")),
        name="outproj",
    )(x, a, bb, m, dd, w, mod)


def _top_rows(s, n):
    rows = []
    for r in range(n):
        mx = jnp.max(s, axis=0, keepdims=True)
        rows.append(mx)
        if r + 1 < n:
            s = jnp.where(s == mx, -jnp.inf, s)
    return rows


def _oddeven_merge(lo, hi, r):
    step = r * 2
    if step < hi - lo:
        yield from _oddeven_merge(lo, hi, step)
        yield from _oddeven_merge(lo + r, hi, step)
        yield from [(i, i + r) for i in range(lo + r, hi - r, step)]
    else:
        yield (lo, lo + r)


def _oddeven_sort(lo, hi):
    if hi - lo >= 1:
        mid = lo + (hi - lo) // 2
        yield from _oddeven_sort(lo, mid)
        yield from _oddeven_sort(mid + 1, hi)
        yield from _oddeven_merge(lo, hi, 1)


SORT16 = list(_oddeven_sort(0, PEER_TOPK - 1))


def _colmax8(x):
    for sh in (1, 2, 4):
        x = jnp.maximum(x, pltpu.roll(x, sh, 0))
    return x


def _top16_rows(s):
    ls = [s[8 * r:8 * r + 8, :] for r in range(PEER_TOPK)]
    for i, j in SORT16:
        ls[i], ls[j] = jnp.maximum(ls[i], ls[j]), jnp.minimum(ls[i], ls[j])
    rows = []
    for r in range(PEER_TOPK):
        m = _colmax8(ls[0])
        rows.append(m[0:1, :])
        keep = PEER_TOPK - r - 1
        hit = ls[0] == m
        ls = [jnp.where(hit, ls[d + 1], ls[d]) for d in range(keep)]
    return rows


def _peer_route_kernel(x_ref, g_ref, mod_ref, wq_ref, keys_ref, xnt_ref, e1_ref, th_ref, s2_ref, e2_ref, xn_scr):
    h = pl.program_id(2)

    @pl.when(h == 0)
    def _():
        xn = _modnorm(x_ref[...], g_ref[...], mod_ref, 3, 4)
        xn_scr[...] = xn.astype(BF16)
        xnt_ref[...] = xn.T.astype(BF16)

    q = jnp.dot(xn_scr[...], wq_ref[...], preferred_element_type=F32).astype(BF16)
    half = PEER_DQ // 2
    s1 = _nt(keys_ref[0], q[:, :half])
    s2 = _nt(keys_ref[1], q[:, half:])
    top1 = _top16_rows(s1)
    top2 = _top16_rows(s2)
    pad_rows = [jnp.full_like(top1[0], -jnp.inf)] * (-len(STAIR) % 8)
    cand = jnp.concatenate([top1[a] + top2[b] for a, b in STAIR] + pad_rows, axis=0)
    thr = _top_rows(cand, PEER_TOPK)[-1]
    m1, m2 = top1[0], top2[0]
    zsum = jnp.sum(jnp.where(cand >= thr, jnp.exp(cand - (m1 + m2)), 0.0), axis=0, keepdims=True)
    e1_ref[...] = jnp.exp(s1 - m1) / zsum
    th_ref[...] = thr - s1
    s2_ref[...] = s2
    e2_ref[...] = jnp.exp(s2 - m2)


def _peer_route(x, g, mod, wq, keys):
    b, l, d = x.shape
    tm = min(256, l)
    nt = l // tm
    n = b * l
    hb = pl.BlockSpec((None, PEER_NKEYS, tm), lambda bi, i, h: (h, 0, bi * nt + i))
    return pl.pallas_call(
        _peer_route_kernel,
        grid=(b, nt, PEER_HEADS),
        in_specs=[
            pl.BlockSpec((None, tm, d), lambda bi, i, h: (bi, i, 0)),
            pl.BlockSpec((1, d), lambda bi, i, h: (0, 0)),
            pl.BlockSpec((None, 6, d), lambda bi, i, h: (bi, 0, 0)),
            pl.BlockSpec((d, PEER_DQ), lambda bi, i, h: (0, h)),
            pl.BlockSpec((None, 2, PEER_NKEYS, PEER_DQ // 2), lambda bi, i, h: (h, 0, 0, 0)),
        ],
        out_specs=[pl.BlockSpec((d, tm), lambda bi, i, h: (0, bi * nt + i)), hb, hb, hb, hb],
        out_shape=[jax.ShapeDtypeStruct((d, n), BF16)] + [jax.ShapeDtypeStruct((PEER_HEADS, PEER_NKEYS, n), F32)] * 4,
        scratch_shapes=[pltpu.VMEM((tm, d), BF16)],
        compiler_params=_cparams(("parallel", "parallel", "arbitrary")),
        name="peer_route",
    )(x, g.reshape(1, d), mod, wq, keys)


GELU_C = math.sqrt(2.0 / math.pi)
PEER_ROWS = 8
PEER_VMEM_LIMIT = 58 * 1024 * 1024
GATE_RB = 32
GATE_IG = 4


def _gelu_tanh(x):
    k = -2.0 * GELU_C * LOG2E
    return x / (1.0 + jnp.exp2(x * (k + (k * 0.044715) * (x * x))))


def _peer_pipe_kernel(*refs, final, n_pairs):
    refs = list(refs)
    x_ref, mod_ref, xnt_ref, e1_ref, th_ref, s2_ref, e2_ref, u_ref, vt_ref = refs[:9]
    pos = 9
    if final:
        fg_ref = refs[pos]
        pos += 1
    o_ref, acc_scr = refs[pos:pos + 2]
    a_scr = (refs[pos + 2:pos + 4], refs[pos + 4:pos + 6])
    wg_scr = (refs[pos + 6:pos + 8], refs[pos + 8:pos + 10])
    be1_scr = refs[pos + 10:pos + 12]
    bth_scr = refs[pos + 12:pos + 14]
    g = pl.program_id(2)
    eb = PEER_ROWS * PEER_NKEYS
    tm = s2_ref.shape[2]

    def scores(p):
        for k in range(2):
            a_scr[p][k][...] = jnp.dot(u_ref[k * eb:(k + 1) * eb, :], xnt_ref[...], preferred_element_type=F32)

    def gate_one(a_ref, wg_ref, k):
        zero8 = jnp.zeros((8, tm), F32)
        for h in range(PEER_HEADS):
            for ii in range(PEER_ROWS):
                be1_scr[k][h * PEER_ROWS + ii] = e1_ref[h, pl.ds(k * PEER_ROWS + ii, 1), :] + zero8
                bth_scr[k][h * PEER_ROWS + ii] = th_ref[h, pl.ds(k * PEER_ROWS + ii, 1), :] + zero8
        for c0 in range(0, tm, LANES):
            cs = slice(c0, c0 + LANES)
            for r0 in range(0, PEER_NKEYS, GATE_RB):
                for ig in range(0, PEER_ROWS, GATE_IG):
                    ws = [None] * GATE_IG
                    for h in range(PEER_HEADS):
                        s2 = s2_ref[h, r0:r0 + GATE_RB, cs].reshape(GATE_RB // 8, 8, LANES)
                        e2 = e2_ref[h, r0:r0 + GATE_RB, cs].reshape(GATE_RB // 8, 8, LANES)
                        for j in range(GATE_IG):
                            row = h * PEER_ROWS + ig + j
                            contrib = jnp.where(s2 >= bth_scr[k][row, :, cs][None], e2 * be1_scr[k][row, :, cs][None], 0.0)
                            ws[j] = contrib if ws[j] is None else ws[j] + contrib
                    for j in range(GATE_IG):
                        rs = slice((ig + j) * PEER_NKEYS + r0, (ig + j) * PEER_NKEYS + r0 + GATE_RB)
                        wj = ws[j].reshape(GATE_RB, LANES)
                        wg_ref[rs, cs] = (wj * _gelu_tanh(a_ref[rs, cs])).astype(BF16)

    def gates(p):
        for k in range(2):
            gate_one(a_scr[1 - p][k], wg_scr[p][k], k)

    def accums(p):
        for k in range(2):
            acc_scr[...] += jnp.dot(vt_ref[:, k * eb:(k + 1) * eb], wg_scr[1 - p][k][...], preferred_element_type=F32)

    @pl.when(g == 0)
    def _():
        acc_scr[...] = jnp.zeros_like(acc_scr)
        scores(0)

    @pl.when(g == 1)
    def _():
        scores(1)
        gates(1)

    for p in range(2):
        @pl.when((g >= 2) & (g < n_pairs) & (g % 2 == p))
        def _(p=p):
            scores(p)
            gates(p)
            accums(p)

    @pl.when(g == n_pairs)
    def _():
        gates(n_pairs % 2)
        accums(n_pairs % 2)

    @pl.when(g == n_pairs + 1)
    def _():
        accums((n_pairs + 1) % 2)
        y = x_ref[...] + mod_ref[5:6, :] * acc_scr[...].T
        if final:
            ms = jnp.mean(y * y, axis=-1, keepdims=True)
            y = y * lax.rsqrt(ms + NORM_EPS) * fg_ref[...]
        o_ref[...] = y


def _peer_pipe(x, mod, xnt, e1, th, s2, e2, u, vt, final_g):
    b, l, d = x.shape
    tm = min(512, l)
    nt = l // tm
    eb = PEER_ROWS * PEER_NKEYS
    n_pairs = u.shape[0] // (2 * eb)
    assert n_pairs >= 2
    final = final_g is not None
    tok = lambda bi, i: bi * nt + i
    clamp = lambda v: jnp.clip(v, 0, n_pairs - 1)
    rows = pl.BlockSpec((PEER_HEADS, 2 * PEER_ROWS, tm), lambda bi, i, g: (0, clamp(g - 1), tok(bi, i)))
    keyblk = pl.BlockSpec((PEER_HEADS, PEER_NKEYS, tm), lambda bi, i, g: (0, 0, tok(bi, i)))
    in_specs = [
        pl.BlockSpec((None, tm, d), lambda bi, i, g: (bi, i, 0)),
        pl.BlockSpec((None, 6, d), lambda bi, i, g: (bi, 0, 0)),
        pl.BlockSpec((d, tm), lambda bi, i, g: (0, tok(bi, i))),
        rows, rows, keyblk, keyblk,
        pl.BlockSpec((2 * eb, d), lambda bi, i, g: (clamp(g), 0)),
        pl.BlockSpec((d, 2 * eb), lambda bi, i, g: (0, clamp(g - 2))),
    ]
    args = [x, mod, xnt, e1, th, s2, e2, u, vt]
    if final:
        in_specs.append(pl.BlockSpec((1, d), lambda bi, i, g: (0, 0)))
        args.append(final_g.reshape(1, d))
    bcast = pltpu.VMEM((PEER_HEADS * PEER_ROWS, 8, tm), F32)
    return pl.pallas_call(
        functools.partial(_peer_pipe_kernel, final=final, n_pairs=n_pairs),
        grid=(b, nt, n_pairs + 2),
        in_specs=in_specs,
        out_specs=pl.BlockSpec((None, tm, d), lambda bi, i, g: (bi, i, 0)),
        out_shape=jax.ShapeDtypeStruct((b, l, d), F32),
        scratch_shapes=([pltpu.VMEM((d, tm), F32)] + [pltpu.VMEM((eb, tm), F32)] * 4 + [pltpu.VMEM((eb, tm), BF16)] * 4
                        + [bcast] * 4),
        compiler_params=_cparams(("parallel", "parallel", "arbitrary"), PEER_VMEM_LIMIT),
        name="peer_experts",
    )(*args)


def _pad_cols(a, width):
    return jnp.pad(a, ((0, 0), (0, width - a.shape[1])))


def _layout_w_in(w):
    d = w.shape[0]
    z = lambda n: jnp.zeros((d, n), w.dtype)
    gla, hg, mla, dif = 0, 800, 2080, 2432
    parts = [w[:, gla:gla + 768], w[:, hg:hg + 768],
             w[:, mla:mla + 192], z(64), w[:, mla + 192:mla + 320], z(64), w[:, mla + 320:mla + 352], z(32)]
    for base in (dif, dif + 256, dif + 512):
        for h in range(DIFF_HEADS):
            parts += [w[:, base + 64 * h:base + 64 * (h + 1)], z(64)]
    parts += [w[:, gla + 768:gla + 800], z(224), w[:, hg + 768:hg + 1280]]
    out = jnp.concatenate(parts, axis=1)
    assert out.shape[1] == MAIN_COLS + GATE_COLS
    return out.astype(BF16)


def _layout_mla(w_uq, w_ukv):
    dq = MLA_NOPE + MLA_ROPE
    wq = jnp.concatenate([_pad_cols(w_uq[:, dq * h:dq * (h + 1)], LANES) for h in range(MLA_HEADS)], axis=1)
    wq = jnp.pad(wq, ((0, 256 - MLA_Q_RANK), (0, 0)))
    per = MLA_NOPE + MLA_DV
    wk = jnp.concatenate([_pad_cols(w_ukv[:, per * h:per * h + MLA_NOPE], LANES) for h in range(MLA_HEADS)], axis=1)
    wv = jnp.concatenate([_pad_cols(w_ukv[:, per * h + MLA_NOPE:per * (h + 1)], LANES) for h in range(MLA_HEADS)], axis=1)
    return wq.astype(BF16), jnp.concatenate([wk, wv], axis=1).astype(BF16)


def kernel(x, c, ctx, c_ctx, ada_w, ada_b, norm_mix_g, norm_ffn_g, w_in, w_out, gla_gate_w, gla_gate_b, gla_norm_g, hgrn_lb_raw, hgrn_norm_g, mla_q_norm_g, mla_kv_norm_g, mla_w_uq, mla_w_ukv, diff_lambda, diff_norm_g, peer_wq, peer_keys, peer_u, peer_v, final_norm_g):
    b, l, d = x.shape
    lc = ctx.shape[1]
    depth = ada_w.shape[0]
    assert lc <= ATT_TK, "context keys are consumed as one attention step"

    rows = -(-(b + 1) // 8) * 8
    cpad = jnp.zeros((rows, d), F32).at[:b].set(c).at[b].set(c_ctx)
    mod_all = _adaln(cpad, ada_w, ada_b)

    lb_sm = jax.nn.softmax(hgrn_lb_raw.astype(F32), axis=0)
    lb_all = jnp.cumsum(lb_sm, axis=0) - lb_sm[0]

    lat_mla_tab = _rope_tables(l, 64, 32, True)
    ctx_mla_tab = _rope_tables(lc, 64, 32, False)
    lat_dif_tab = _rope_tables(l, 0, 64, True)
    ctx_dif_tab = _rope_tables(lc, 0, 64, False)

    xc = ctx
    for li in range(depth):
        need_ctx = li < depth - 1
        lam_init = 0.8 - 0.6 * math.exp(-0.3 * li)
        mod_l = mod_all[li, :b].reshape(b, 6, d)
        mod_c = jnp.broadcast_to(mod_all[li, b].reshape(1, 6, d), (b, 6, d))

        w_perm = _layout_w_in(w_in[li])
        w_o = w_out[li].astype(BF16)
        wg = jnp.zeros((2, 256, 128), F32)
        wg = wg.at[0, 0:GLA_GATE_RANK].set(gla_gate_w[li, 0]).at[1, GLA_GATE_RANK:2 * GLA_GATE_RANK].set(gla_gate_w[li, 1])
        wg = wg.astype(BF16)
        bg = gla_gate_b[li].reshape(2, 1, 128)
        gla_gn = jnp.tile(gla_norm_g[li], GLA_HEADS).reshape(1, 256)
        hg_gn = jnp.tile(hgrn_norm_g[li], HG_HEADS).reshape(1, 256)
        lb = lb_all[li]
        hg_p = jnp.zeros((8, 256), F32).at[0].set(jnp.log(jnp.maximum(lb, LB_FLOOR))).at[1].set(jnp.log1p(-lb)).at[2].set(1.0 - lb)
        hg_dummy = jnp.zeros((8, 128), F32)
        gq = jnp.pad(mla_q_norm_g[li], (0, 256 - MLA_Q_RANK)).reshape(1, 256)
        gkv = mla_kv_norm_g[li].reshape(1, MLA_KV_RANK)
        wq_mla, wkv_mla = _layout_mla(mla_w_uq[li], mla_w_ukv[li])
        lq1, lk1, lq2, lk2 = diff_lambda[li, 0], diff_lambda[li, 1], diff_lambda[li, 2], diff_lambda[li, 3]
        lam = (jnp.exp(jnp.sum(lq1 * lk1).astype(F32)) - jnp.exp(jnp.sum(lq2 * lk2).astype(F32)) + lam_init)
        lam_row = jnp.full((1, LANES), lam, F32)
        dif_gn = jnp.tile(diff_norm_g[li], 2).reshape(1, LANES)
        wq_peer = peer_wq[li].astype(BF16)
        keys = peer_keys[li].astype(BF16)
        u_bf = peer_u[li].astype(BF16)
        vt_bf = peer_v[li].astype(BF16).T

        main_c, gate_c = _inproj(xc, norm_mix_g[li], mod_c, w_perm)
        main_l, gate_l = _inproj(x, norm_mix_g[li], mod_l, w_perm)

        mixes_c, mixes_l = [], []
        for mixer, pa_f, pb_f, pa_b, pb_b, gn, dk in (
            ("gla", wg[0], bg[0], wg[1], bg[1], gla_gn, 128),
            ("hg", hg_p, hg_dummy, hg_p, hg_dummy, hg_gn, 256),
        ):
            zero = jnp.zeros((b, 256, dk), F32)
            o_cf, s_f = _scan(main_c, gate_c, pa_f, pb_f, gn, zero, None, mixer=mixer, reverse=False, chunk=64)
            mix_c, s_b = _scan(main_c, gate_c, pa_b, pb_b, gn, zero, o_cf, mixer=mixer, reverse=True, chunk=64)
            o_lf, _ = _scan(main_l, gate_l, pa_f, pb_f, gn, s_f, None, mixer=mixer, reverse=False, chunk=64)
            mix_l, _ = _scan(main_l, gate_l, pa_b, pb_b, gn, s_b, o_lf, mixer=mixer, reverse=True, chunk=64)
            mixes_c.append(mix_c)
            mixes_l.append(mix_l)

        qm_c, km_c, vm_c = _mla_prep(main_c, gq, gkv, wq_mla, wkv_mla, ctx_mla_tab)
        qm_l, km_l, vm_l = _mla_prep(main_l, gq, gkv, wq_mla, wkv_mla, lat_mla_tab)
        mla_l = _attention(qm_l, km_c, vm_c, km_l, vm_l, None, n_maps=1, lam_init=lam_init)
        qd_c, kd_c, vd_c = _diff_prep(main_c, ctx_dif_tab)
        qd_l, kd_l, vd_l = _diff_prep(main_l, lat_dif_tab)
        dif_l = _attention(qd_l, kd_c, vd_c, kd_l, vd_l, (lam_row, dif_gn), n_maps=2, lam_init=lam_init)

        x = _outproj(x, mixes_l[0], mixes_l[1], mla_l, dif_l, w_o, mod_l)
        rt = _peer_route(x, norm_ffn_g[li], mod_l, wq_peer, keys)
        x = _peer_pipe(x, mod_l, *rt, u_bf, vt_bf, None if need_ctx else final_norm_g)

        if need_ctx:
            mla_c = _attention(qm_c, km_c, vm_c, None, None, None, n_maps=1, lam_init=lam_init)
            dif_c = _attention(qd_c, kd_c, vd_c, None, None, (lam_row, dif_gn), n_maps=2, lam_init=lam_init)
            xc = _outproj(xc, mixes_c[0], mixes_c[1], mla_c, dif_c, w_o, mod_c)
            rtc = _peer_route(xc, norm_ffn_g[li], mod_c, wq_peer, keys)
            xc = _peer_pipe(xc, mod_c, *rtc, u_bf, vt_bf, None)
    return x
```

```python
import functools
import math

import jax
import jax.numpy as jnp
import numpy as np
from jax import lax
from jax.experimental import pallas as pl
from jax.experimental.pallas import tpu as pltpu

F32 = jnp.float32
BF16 = jnp.bfloat16

NORM_EPS = 1e-6
ROPE_BASE = 10000.0
GRID_W = 64
LB_FLOOR = 1e-30

GLA_HEADS, GLA_DK, GLA_DV, GLA_GATE_RANK, GLA_GATE_NORM = 4, 32, 64, 16, 16.0
HG_HEADS, HG_DIM = 4, 64
MLA_HEADS, MLA_Q_RANK, MLA_KV_RANK, MLA_NOPE, MLA_ROPE, MLA_DV = 4, 192, 128, 64, 32, 64
DIFF_HEADS, DIFF_DQK, DIFF_DV = 4, 32, 64
PEER_HEADS, PEER_NKEYS, PEER_DQ, PEER_TOPK = 8, 128, 256, 16

LANES = 128
VMEM_LIMIT = 48 * 1024 * 1024
EXP_CAP = 80.0
NEG_BIG = -1e30
ATT_TK = 512
LOG2E = 1.4426950408889634
ATT_R = 512
ATT_STEPS_PER_TRIP = 2
VT_ROWS = 80

COL_GLA, COL_HG, COL_MLA, COL_DQ, COL_DK, COL_DV, MAIN_COLS = 0, 768, 1536, 2048, 2560, 3072, 3584
GATE_COLS = 768

STAIR = [(a, b) for a in range(PEER_TOPK) for b in range(PEER_TOPK) if (a + 1) * (b + 1) <= PEER_TOPK]


def _cparams(sem, vmem=VMEM_LIMIT):
    return pltpu.CompilerParams(dimension_semantics=sem, vmem_limit_bytes=vmem)


def _sigmoid(x):
    return 1.0 / (1.0 + jnp.exp(-x))


def _logsig(x):
    return jnp.minimum(x, 0.0) - jnp.log(1.0 + jnp.exp(-jnp.abs(x)))


def _nt(a, b):
    return lax.dot_general(a, b, (((1,), (1,)), ((), ())), preferred_element_type=F32)


def _tn(a, b):
    return lax.dot_general(a, b, (((0,), (0,)), ((), ())), preferred_element_type=F32)


def _adaln_kernel(c_ref, w_ref, b_ref, o_ref):
    c = c_ref[...]
    s = c * _sigmoid(c)
    o_ref[...] = jnp.dot(s.astype(BF16), w_ref[...].astype(BF16), preferred_element_type=F32) + b_ref[...]


def _adaln(cpad, ada_w, ada_b):
    depth, d, n6 = ada_w.shape
    rows = cpad.shape[0]
    tn = 512
    return pl.pallas_call(
        _adaln_kernel,
        grid=(depth, n6 // tn),
        in_specs=[
            pl.BlockSpec((rows, d), lambda l, j: (0, 0)),
            pl.BlockSpec((None, d, tn), lambda l, j: (l, 0, j)),
            pl.BlockSpec((None, 1, tn), lambda l, j: (l, 0, j)),
        ],
        out_specs=pl.BlockSpec((None, rows, tn), lambda l, j: (l, 0, j)),
        out_shape=jax.ShapeDtypeStruct((depth, rows, n6), F32),
        compiler_params=_cparams(("parallel", "parallel")),
        name="adaln",
    )(cpad, ada_w, ada_b.reshape(depth, 1, n6))


def _modnorm(x, g, mod_ref, shift_row, scale_row):
    ms = jnp.mean(x * x, axis=-1, keepdims=True)
    h = x * lax.rsqrt(ms + NORM_EPS) * g
    return h * (1.0 + mod_ref[scale_row:scale_row + 1, :]) + mod_ref[shift_row:shift_row + 1, :]


def _inproj_kernel(x_ref, g_ref, mod_ref, w_ref, main_ref, gate_ref):
    h = _modnorm(x_ref[...], g_ref[...], mod_ref, 0, 1)
    y = jnp.dot(h.astype(BF16), w_ref[...], preferred_element_type=F32)
    main_ref[...] = y[:, :MAIN_COLS].astype(BF16)
    gate_ref[...] = y[:, MAIN_COLS:]


def _inproj(x, g, mod, w_perm):
    b, l, d = x.shape
    tm = min(256, l)
    nc = w_perm.shape[1]
    return pl.pallas_call(
        _inproj_kernel,
        grid=(b, l // tm),
        in_specs=[
            pl.BlockSpec((None, tm, d), lambda bi, i: (bi, i, 0)),
            pl.BlockSpec((1, d), lambda bi, i: (0, 0)),
            pl.BlockSpec((None, 6, d), lambda bi, i: (bi, 0, 0)),
            pl.BlockSpec((d, nc), lambda bi, i: (0, 0)),
        ],
        out_specs=[
            pl.BlockSpec((None, tm, MAIN_COLS), lambda bi, i: (bi, i, 0)),
            pl.BlockSpec((None, tm, GATE_COLS), lambda bi, i: (bi, i, 0)),
        ],
        out_shape=[
            jax.ShapeDtypeStruct((b, l, MAIN_COLS), BF16),
            jax.ShapeDtypeStruct((b, l, GATE_COLS), F32),
        ],
        compiler_params=_cparams(("parallel", "parallel")),
        name="inproj",
    )(x, g.reshape(1, d), mod, w_perm)


def _scan_kernel(*refs, mixer, reverse, finish, chunk):
    if finish:
        main_ref, gate_ref, pa_ref, pb_ref, gn_ref, s0_ref, oprev_ref, o_ref, sfin_ref, st_scr = refs
    else:
        main_ref, gate_ref, pa_ref, pb_ref, gn_ref, s0_ref, o_ref, sfin_ref, st_scr = refs
        oprev_ref = None
    i = pl.program_id(1)

    @pl.when(i == 0)
    def _():
        st_scr[...] = s0_ref[...]

    main = main_ref[...]
    t = main.shape[0]
    c = chunk
    nc = t // c
    heads = 4
    dv = 256
    v = main[:, 256:512]
    gcol = main[:, 512:768].astype(F32)
    if mixer == "gla":
        dk = 128
        q = main[:, 0:128].astype(F32) * (GLA_DK ** -0.5)
        k = main[:, 128:256].astype(F32)
        z = jnp.dot(gate_ref[...].astype(BF16), pa_ref[...], preferred_element_type=F32) + pb_ref[...]
        g = _logsig(z) * (1.0 / GLA_GATE_NORM)
    else:
        dk = 256
        qq = main[:, 0:256].astype(F32)
        q = qq * _sigmoid(qq)
        z = gate_ref[...]
        la = pa_ref[0:1, :]
        lb = pa_ref[1:2, :] + _logsig(z)
        g = jnp.maximum(la, lb) + jnp.log(1.0 + jnp.exp(-jnp.abs(la - lb)))
        k = pa_ref[2:3, :] * _sigmoid(-z)
    hk = dk // heads
    hv = dv // heads

    rt = lax.broadcasted_iota(jnp.int32, (t, t), 0)
    ct = lax.broadcasted_iota(jnp.int32, (t, t), 1)
    same = (rt // c) == (ct // c)
    tri = jnp.where(same & ((ct >= rt) if reverse else (ct <= rt)), 1.0, 0.0).astype(BF16)
    g1 = g.astype(BF16)
    r1 = g - g1.astype(F32)
    g2 = r1.astype(BF16)
    g3 = (r1 - g2.astype(F32)).astype(BF16)
    bcum = (jnp.dot(tri, g1, preferred_element_type=F32) + jnp.dot(tri, g2, preferred_element_type=F32)
            + jnp.dot(tri, g3, preferred_element_type=F32))

    b3 = bcum.reshape(nc, c, dk)
    q3 = q.reshape(nc, c, dk)
    k3 = k.reshape(nc, c, dk)
    if reverse:
        bmid = b3[:, c // 2:c // 2 + 1, :]
        bend = b3[:, 0:1, :]
    else:
        bmid = b3[:, c // 2 - 1:c // 2, :]
        bend = b3[:, c - 1:c, :]
    eb = b3 - bmid
    qa = (q3 * jnp.exp(jnp.minimum(eb, EXP_CAP))).reshape(t, dk).astype(BF16)
    ka = (k3 * jnp.exp(jnp.minimum(-eb, EXP_CAP))).reshape(t, dk).astype(BF16)
    qi = (q3 * jnp.exp(b3)).reshape(t, dk).astype(BF16)
    kb = (k3 * jnp.exp(bend - b3)).reshape(t, dk).astype(BF16)
    dec = jnp.exp(bend)

    lane_k = lax.broadcasted_iota(jnp.int32, (1, dk), 1) // hk
    lane_v = lax.broadcasted_iota(jnp.int32, (1, dv), 1) // hv
    rr = lax.broadcasted_iota(jnp.int32, (c, heads * c), 0)
    cc = lax.broadcasted_iota(jnp.int32, (c, heads * c), 1) % c
    causal = (cc >= rr) if reverse else (cc <= rr)
    bd = (lax.broadcasted_iota(jnp.int32, (dv, dk), 0) // hv) == (lax.broadcasted_iota(jnp.int32, (dv, dk), 1) // hk)

    o_intra = []
    upd = []
    for ci in range(nc):
        sl = slice(ci * c, (ci + 1) * c)
        ka_c = ka[sl]
        v_c = v[sl]
        kst = jnp.concatenate([jnp.where(lane_k == h, ka_c, 0) for h in range(heads)], axis=0)
        vst = jnp.concatenate([jnp.where(lane_v == h, v_c, 0) for h in range(heads)], axis=0)
        sw = _nt(qa[sl], kst)
        p = jnp.where(causal, sw, 0.0).astype(BF16)
        o_intra.append(jnp.dot(p, vst, preferred_element_type=F32))
        upd.append(jnp.where(bd, _tn(v_c, kb[sl]), 0.0))

    outs = [None] * nc
    order = range(nc - 1, -1, -1) if reverse else range(nc)
    for ci in order:
        sl = slice(ci * c, (ci + 1) * c)
        st = st_scr[...]
        outs[ci] = o_intra[ci] + _nt(qi[sl], st.astype(BF16))
        st_scr[...] = st * dec[ci] + upd[ci]
    o = jnp.concatenate(outs, axis=0)

    @pl.when(i == pl.num_programs(1) - 1)
    def _():
        sfin_ref[...] = st_scr[...]

    if not finish:
        o_ref[...] = o
    else:
        ot = o + oprev_ref[...]
        sq = ot * ot
        jv = ((lax.broadcasted_iota(jnp.int32, (dv, dv), 0) // hv)
              == (lax.broadcasted_iota(jnp.int32, (dv, dv), 1) // hv))
        jm = jnp.where(jv, 1.0, 0.0).astype(BF16)
        s1 = sq.astype(BF16)
        s2 = (sq - s1.astype(F32)).astype(BF16)
        ms = (jnp.dot(s1, jm, preferred_element_type=F32) + jnp.dot(s2, jm, preferred_element_type=F32)) * (1.0 / hv)
        y = ot * lax.rsqrt(ms + NORM_EPS) * gn_ref[...]
        o_ref[...] = (y * (gcol * _sigmoid(gcol))).astype(BF16)


def _scan(main, gate, pa, pb, gn, s0, oprev, *, mixer, reverse, chunk):
    b, l, _ = main.shape
    t = min(256, l)
    nb = l // t
    finish = oprev is not None
    dk = 128 if mixer == "gla" else 256
    dv = 256
    main_blk = 0 if mixer == "gla" else 1
    gate_blk = 0 if mixer == "gla" else (2 if reverse else 1)

    def tok(bi, i):
        return (nb - 1 - i) if reverse else i

    in_specs = [
        pl.BlockSpec((None, t, 768), lambda bi, i: (bi, tok(bi, i), main_blk)),
        pl.BlockSpec((None, t, 256), lambda bi, i: (bi, tok(bi, i), gate_blk)),
        pl.BlockSpec(pa.shape, lambda bi, i: (0, 0)),
        pl.BlockSpec(pb.shape, lambda bi, i: (0, 0)),
        pl.BlockSpec((1, dv), lambda bi, i: (0, 0)),
        pl.BlockSpec((None, dv, dk), lambda bi, i: (bi, 0, 0)),
    ]
    args = [main, gate, pa, pb, gn, s0]
    if finish:
        in_specs.append(pl.BlockSpec((None, t, dv), lambda bi, i: (bi, tok(bi, i), 0)))
        args.append(oprev)
    return pl.pallas_call(
        functools.partial(_scan_kernel, mixer=mixer, reverse=reverse, finish=finish, chunk=chunk),
        grid=(b, nb),
        in_specs=in_specs,
        out_specs=[
            pl.BlockSpec((None, t, dv), lambda bi, i: (bi, tok(bi, i), 0)),
            pl.BlockSpec((None, dv, dk), lambda bi, i: (bi, 0, 0)),
        ],
        out_shape=[
            jax.ShapeDtypeStruct((b, l, dv), BF16 if finish else F32),
            jax.ShapeDtypeStruct((b, dv, dk), F32),
        ],
        scratch_shapes=[pltpu.VMEM((dv, dk), F32)],
        compiler_params=_cparams(("parallel", "arbitrary")),
        name=f"scan_{mixer}_{'bwd' if reverse else 'fwd'}",
    )(*args)


def _rope(x, cos, sin_lo, sin_hi):
    return x * cos + pltpu.roll(x, LANES - 8, 1) * sin_lo + pltpu.roll(x, 8, 1) * sin_hi


def _rope_tables(l, first_lane, n_lanes, rotate):
    lane = np.arange(LANES)
    d = (lane - first_lane) % 32
    active = (lane >= first_lane) & (lane < first_lane + n_lanes)
    freqs = ROPE_BASE ** (-(d % 8).astype(np.float32) / 8.0)
    tt = jnp.arange(l, dtype=jnp.int32)
    rows = (tt // GRID_W).astype(F32)[:, None]
    cols = (tt % GRID_W).astype(F32)[:, None]
    pos = jnp.where(jnp.asarray(d < 16)[None, :], rows, cols)
    ang = pos * jnp.asarray(freqs, F32)[None, :]
    act = jnp.asarray(active)[None, :] & rotate
    lo = jnp.asarray((d % 16) < 8)[None, :]
    cos = jnp.where(act, jnp.cos(ang), 1.0)
    sin = jnp.where(act, jnp.sin(ang), 0.0)
    return cos, jnp.where(lo, -sin, 0.0), jnp.where(lo, 0.0, sin)


def _vt_block(vgrp):
    lane = lax.broadcasted_iota(jnp.int32, (1, LANES), 1)
    return jnp.where(lane == 64, 1.0, vgrp).T[:VT_ROWS].astype(BF16)


def _mla_prep_kernel(slab_ref, gq_ref, gkv_ref, wq_ref, wkv_ref, cos_ref, slo_ref, shi_ref, qt_ref, k_ref, vt_ref):
    slab = slab_ref[...]
    cos, slo, shi = cos_ref[...], slo_ref[...], shi_ref[...]
    cq = slab[:, 0:256].astype(F32)
    qn = cq * lax.rsqrt(jnp.sum(cq * cq, axis=-1, keepdims=True) * (1.0 / MLA_Q_RANK) + NORM_EPS) * gq_ref[...]
    qall = jnp.dot(qn.astype(BF16), wq_ref[...], preferred_element_type=F32)
    ckv = slab[:, 256:384].astype(F32)
    kvn = ckv * lax.rsqrt(jnp.mean(ckv * ckv, axis=-1, keepdims=True) + NORM_EPS) * gkv_ref[...]
    kvall = jnp.dot(kvn.astype(BF16), wkv_ref[...], preferred_element_type=F32)
    kr = _rope(slab[:, 384:512].astype(F32), cos, slo, shi)
    scale = (MLA_NOPE + MLA_ROPE) ** -0.5 * LOG2E
    for h in range(MLA_HEADS):
        qh = _rope(qall[:, h * LANES:(h + 1) * LANES], cos, slo, shi)
        qt_ref[h] = (qh * scale).T.astype(BF16)
        k_ref[h] = (kvall[:, h * LANES:(h + 1) * LANES] + kr).astype(BF16)
        vt_ref[h] = _vt_block(kvall[:, (MLA_HEADS + h) * LANES:(MLA_HEADS + h + 1) * LANES])


def _attn_prep_specs(b, l, tm, heads):
    qt = pl.BlockSpec((None, heads, LANES, tm), lambda bi, i: (bi, 0, 0, i))
    kk = pl.BlockSpec((None, heads, tm, LANES), lambda bi, i: (bi, 0, i, 0))
    vt = pl.BlockSpec((None, heads, None, VT_ROWS, tm), lambda bi, i: (bi, 0, i, 0, 0))
    shapes = [
        jax.ShapeDtypeStruct((b, heads, LANES, l), BF16),
        jax.ShapeDtypeStruct((b, heads, l, LANES), BF16),
        jax.ShapeDtypeStruct((b, heads, l // tm, VT_ROWS, tm), BF16),
    ]
    return [qt, kk, vt], shapes


def _mla_prep(main, gq, gkv, wq, wkv, tables):
    b, l, _ = main.shape
    tm = min(ATT_TK, l)
    cos, slo, shi = tables
    tab = pl.BlockSpec((tm, LANES), lambda bi, i: (i, 0))
    out_specs, out_shape = _attn_prep_specs(b, l, tm, MLA_HEADS)
    return pl.pallas_call(
        _mla_prep_kernel,
        grid=(b, l // tm),
        in_specs=[
            pl.BlockSpec((None, tm, 512), lambda bi, i: (bi, i, COL_MLA // 512)),
            pl.BlockSpec(gq.shape, lambda bi, i: (0, 0)),
            pl.BlockSpec(gkv.shape, lambda bi, i: (0, 0)),
            pl.BlockSpec(wq.shape, lambda bi, i: (0, 0)),
            pl.BlockSpec(wkv.shape, lambda bi, i: (0, 0)),
            tab, tab, tab,
        ],
        out_specs=out_specs,
        out_shape=out_shape,
        compiler_params=_cparams(("parallel", "parallel")),
        name="mla_prep",
    )(main, gq, gkv, wq, wkv, cos, slo, shi)


def _diff_prep_kernel(q_in, k_in, v_in, cos_ref, slo_ref, shi_ref, qt_ref, k_ref, vt_ref):
    cos, slo, shi = cos_ref[...], slo_ref[...], shi_ref[...]
    qs = q_in[...]
    ks = k_in[...]
    vs = v_in[...]
    scale = DIFF_DQK ** -0.5 * LOG2E
    for h in range(DIFF_HEADS):
        sl = slice(h * LANES, (h + 1) * LANES)
        qt_ref[h] = (_rope(qs[:, sl].astype(F32), cos, slo, shi) * scale).T.astype(BF16)
        k_ref[h] = _rope(ks[:, sl].astype(F32), cos, slo, shi).astype(BF16)
        vt_ref[h] = _vt_block(vs[:, sl].astype(F32))


def _diff_prep(main, tables):
    b, l, _ = main.shape
    tm = min(ATT_TK, l)
    cos, slo, shi = tables
    tab = pl.BlockSpec((tm, LANES), lambda bi, i: (i, 0))
    out_specs, out_shape = _attn_prep_specs(b, l, tm, DIFF_HEADS)
    return pl.pallas_call(
        _diff_prep_kernel,
        grid=(b, l // tm),
        in_specs=[
            pl.BlockSpec((None, tm, 512), lambda bi, i: (bi, i, COL_DQ // 512)),
            pl.BlockSpec((None, tm, 512), lambda bi, i: (bi, i, COL_DK // 512)),
            pl.BlockSpec((None, tm, 512), lambda bi, i: (bi, i, COL_DV // 512)),
            tab, tab, tab,
        ],
        out_specs=out_specs,
        out_shape=out_shape,
        compiler_params=_cparams(("parallel", "parallel")),
        name="diff_prep",
    )(main, main, main, cos, slo, shi)


def _attn_kernel(*refs, n_maps, has_lat, lam_init):
    refs = list(refs)
    qt_ref, kc_ref, vtc_ref = refs[:3]
    pos = 3
    if has_lat:
        kl_ref, vtl_ref = refs[pos:pos + 2]
        pos += 2
    if n_maps == 2:
        lam_ref, gn_ref = refs[pos:pos + 2]
        pos += 2
    o_ref = refs[pos]
    if has_lat:
        m_scr, acc_scr, s_scr, cm_scr, al_scr, p_scr = refs[pos + 1:pos + 7]
    tq = qt_ref.shape[2]
    feat = lax.broadcasted_iota(jnp.int32, (LANES, 1), 0)

    qts = []
    for hh in range(2):
        qt = qt_ref[hh]
        if n_maps == 2:
            zero = jnp.zeros_like(qt)
            qt = jnp.concatenate(
                [jnp.where((feat >= DIFF_DQK * mi) & (feat < DIFF_DQK * (mi + 1)), qt, zero) for mi in range(2)], axis=1)
        qts.append(qt)

    accs = []
    for hh in range(2):
        st = jnp.dot(kc_ref[hh], qts[hh], preferred_element_type=F32)
        m0 = jnp.max(st, axis=0, keepdims=True)
        acc0 = jnp.dot(vtc_ref[hh, 0], jnp.exp2(st - m0).astype(BF16), preferred_element_type=F32)
        accs.append(acc0)
        if has_lat:
            m_scr[hh] = m0
            acc_scr[hh] = acc0

    if has_lat:
        n = vtl_ref.shape[1]
        tk = vtl_ref.shape[3]

        def scores(j, slot):
            off = pl.multiple_of(j * tk, tk)
            for hh in range(2):
                st = jnp.dot(kl_ref[hh, pl.ds(off, tk), :], qts[hh], preferred_element_type=F32)
                s_scr[hh, slot] = st
                cm_scr[hh, slot] = jnp.max(st, axis=0, keepdims=True)

        def numer(slot):
            for hh in range(2):
                m_old = m_scr[hh]
                m_new = jnp.maximum(m_old, cm_scr[hh, slot])
                al_scr[hh, slot] = jnp.exp2(m_old - m_new)
                p_scr[hh, slot] = jnp.exp2(s_scr[hh, slot] - m_new).astype(BF16)
                m_scr[hh] = m_new

        def values(j, slot):
            for hh in range(2):
                acc_scr[hh] = (al_scr[hh, slot] * acc_scr[hh]
                               + jnp.dot(vtl_ref[hh, j], p_scr[hh, slot], preferred_element_type=F32))

        first = n % 2
        if first:
            scores(0, 0)
            numer(0)
            values(0, 0)
        if n > first:
            for hh in range(2):
                p_scr[hh, 1] = jnp.zeros(p_scr.shape[2:], BF16)
                al_scr[hh, 1] = jnp.ones(al_scr.shape[2:], F32)
            scores(first, 0)

            per_trip = ATT_STEPS_PER_TRIP if (n - first) % ATT_STEPS_PER_TRIP == 0 else 2

            def body(t, carry):
                for u in range(per_trip):
                    c = first + per_trip * t + u
                    slot = u % 2
                    scores(jnp.minimum(c + 1, n - 1), 1 - slot)
                    numer(slot)
                    values(jnp.maximum(c - 1, first), 1 - slot)
                return carry

            lax.fori_loop(0, (n - first) // per_trip, body, 0)
            values(n - 1, 1)
        accs = [acc_scr[0], acc_scr[1]]

    outs = []
    for hh in range(2):
        acc = accs[hh]
        ot = acc[0:64, :] / acc[64:65, :]
        if n_maps == 2:
            oh = ot[:, :tq] - lam_ref[:, 0:1] * ot[:, tq:]
            ms = jnp.mean(oh * oh, axis=0, keepdims=True)
            ot = oh * lax.rsqrt(ms + NORM_EPS)
        outs.append(ot)
    o = jnp.concatenate(outs, axis=0).T
    if n_maps == 2:
        o = o * gn_ref[...] * (1.0 - lam_init)
    o_ref[...] = o.astype(BF16)


def _attention(qt, kc, vtc, kl, vtl, extra, *, n_maps, lam_init):
    b, nh, _, l = qt.shape
    lc = kc.shape[2]
    tq = min(ATT_R // n_maps, l)
    r = n_maps * tq
    has_lat = kl is not None
    scratch = []
    in_specs = [
        pl.BlockSpec((None, 2, LANES, tq), lambda bi, hp, i: (bi, hp, 0, i)),
        pl.BlockSpec((None, 2, lc, LANES), lambda bi, hp, i: (bi, hp, 0, 0)),
        pl.BlockSpec((None, 2) + vtc.shape[2:], lambda bi, hp, i: (bi, hp, 0, 0, 0)),
    ]
    args = [qt, kc, vtc]
    if has_lat:
        in_specs += [
            pl.BlockSpec((None, 2, kl.shape[2], LANES), lambda bi, hp, i: (bi, hp, 0, 0)),
            pl.BlockSpec((None, 2) + vtl.shape[2:], lambda bi, hp, i: (bi, hp, 0, 0, 0)),
        ]
        args += [kl, vtl]
        tk = vtl.shape[4]
        scratch = [
            pltpu.VMEM((2, 1, r), F32), pltpu.VMEM((2, VT_ROWS, r), F32),
            pltpu.VMEM((2, 2, tk, r), F32), pltpu.VMEM((2, 2, 1, r), F32), pltpu.VMEM((2, 2, 1, r), F32),
            pltpu.VMEM((2, 2, tk, r), BF16),
        ]
    if n_maps == 2:
        in_specs += [pl.BlockSpec((1, LANES), lambda bi, hp, i: (0, 0))] * 2
        args += list(extra)
    return pl.pallas_call(
        functools.partial(_attn_kernel, n_maps=n_maps, has_lat=has_lat, lam_init=lam_init),
        grid=(b, nh // 2, l // tq),
        in_specs=in_specs,
        out_specs=pl.BlockSpec((None, tq, LANES), lambda bi, hp, i: (bi, i, hp)),
        out_shape=jax.ShapeDtypeStruct((b, l, nh * 64), BF16),
        scratch_shapes=scratch,
        compiler_params=_cparams(("parallel", "parallel", "arbitrary")),
        name="attn_mla" if n_maps == 1 else "attn_diff",
    )(*args)


def _outproj_kernel(x_ref, a_ref, b_ref, m_ref, d_ref, w_ref, mod_ref, o_ref):
    mix = jnp.concatenate([a_ref[...], b_ref[...], m_ref[...], d_ref[...]], axis=-1)
    y = jnp.dot(mix, w_ref[...], preferred_element_type=F32)
    o_ref[...] = x_ref[...] + mod_ref[2:3, :] * y


def _outproj(x, a, bb, m, dd, w, mod):
    b, l, d = x.shape
    tm = min(512, l)
    slab = pl.BlockSpec((None, tm, 256), lambda bi, i: (bi, i, 0))
    return pl.pallas_call(
        _outproj_kernel,
        grid=(b, l // tm),
        in_specs=[
            pl.BlockSpec((None, tm, d), lambda bi, i: (bi, i, 0)),
            slab, slab, slab, slab,
            pl.BlockSpec(w.shape, lambda bi, i: (0, 0)),
            pl.BlockSpec((None, 6, d), lambda bi, i: (bi, 0, 0)),
        ],
        out_specs=pl.BlockSpec((None, tm, d), lambda bi, i: (bi, i, 0)),
        out_shape=jax.ShapeDtypeStruct((b, l, d), F32),
        compiler_params=_cparams(("parallel", "parallel")),
        name="outproj",
    )(x, a, bb, m, dd, w, mod)


def _top_rows(s, n):
    rows = []
    for r in range(n):
        mx = jnp.max(s, axis=0, keepdims=True)
        rows.append(mx)
        if r + 1 < n:
            s = jnp.where(s == mx, -jnp.inf, s)
    return rows


def _oddeven_merge(lo, hi, r):
    step = r * 2
    if step < hi - lo:
        yield from _oddeven_merge(lo, hi, step)
        yield from _oddeven_merge(lo + r, hi, step)
        yield from [(i, i + r) for i in range(lo + r, hi - r, step)]
    else:
        yield (lo, lo + r)


def _oddeven_sort(lo, hi):
    if hi - lo >= 1:
        mid = lo + (hi - lo) // 2
        yield from _oddeven_sort(lo, mid)
        yield from _oddeven_sort(mid + 1, hi)
        yield from _oddeven_merge(lo, hi, 1)


SORT16 = list(_oddeven_sort(0, PEER_TOPK - 1))


def _colmax8(x):
    for sh in (1, 2, 4):
        x = jnp.maximum(x, pltpu.roll(x, sh, 0))
    return x


def _top16_rows(s):
    ls = [s[8 * r:8 * r + 8, :] for r in range(PEER_TOPK)]
    for i, j in SORT16:
        ls[i], ls[j] = jnp.maximum(ls[i], ls[j]), jnp.minimum(ls[i], ls[j])
    rows = []
    for r in range(PEER_TOPK):
        m = _colmax8(ls[0])
        rows.append(m[0:1, :])
        keep = PEER_TOPK - r - 1
        hit = ls[0] == m
        ls = [jnp.where(hit, ls[d + 1], ls[d]) for d in range(keep)]
    return rows


def _peer_route_kernel(x_ref, g_ref, mod_ref, wq_ref, keys_ref, xnt_ref, e1_ref, th_ref, s2_ref, e2_ref, xn_scr):
    h = pl.program_id(2)

    @pl.when(h == 0)
    def _():
        xn = _modnorm(x_ref[...], g_ref[...], mod_ref, 3, 4)
        xn_scr[...] = xn.astype(BF16)
        xnt_ref[...] = xn.T.astype(BF16)

    q = jnp.dot(xn_scr[...], wq_ref[...], preferred_element_type=F32).astype(BF16)
    half = PEER_DQ // 2
    s1 = _nt(keys_ref[0], q[:, :half])
    s2 = _nt(keys_ref[1], q[:, half:])
    top1 = _top16_rows(s1)
    top2 = _top16_rows(s2)
    pad_rows = [jnp.full_like(top1[0], -jnp.inf)] * (-len(STAIR) % 8)
    cand = jnp.concatenate([top1[a] + top2[b] for a, b in STAIR] + pad_rows, axis=0)
    thr = _top_rows(cand, PEER_TOPK)[-1]
    m1, m2 = top1[0], top2[0]
    zsum = jnp.sum(jnp.where(cand >= thr, jnp.exp(cand - (m1 + m2)), 0.0), axis=0, keepdims=True)
    e1_ref[...] = jnp.exp(s1 - m1) / zsum
    th_ref[...] = thr - s1
    s2_ref[...] = s2
    e2_ref[...] = jnp.exp(s2 - m2)


def _peer_route(x, g, mod, wq, keys):
    b, l, d = x.shape
    tm = min(256, l)
    nt = l // tm
    n = b * l
    hb = pl.BlockSpec((None, PEER_NKEYS, tm), lambda bi, i, h: (h, 0, bi * nt + i))
    return pl.pallas_call(
        _peer_route_kernel,
        grid=(b, nt, PEER_HEADS),
        in_specs=[
            pl.BlockSpec((None, tm, d), lambda bi, i, h: (bi, i, 0)),
            pl.BlockSpec((1, d), lambda bi, i, h: (0, 0)),
            pl.BlockSpec((None, 6, d), lambda bi, i, h: (bi, 0, 0)),
            pl.BlockSpec((d, PEER_DQ), lambda bi, i, h: (0, h)),
            pl.BlockSpec((None, 2, PEER_NKEYS, PEER_DQ // 2), lambda bi, i, h: (h, 0, 0, 0)),
        ],
        out_specs=[pl.BlockSpec((d, tm), lambda bi, i, h: (0, bi * nt + i)), hb, hb, hb, hb],
        out_shape=[jax.ShapeDtypeStruct((d, n), BF16)] + [jax.ShapeDtypeStruct((PEER_HEADS, PEER_NKEYS, n), F32)] * 4,
        scratch_shapes=[pltpu.VMEM((tm, d), BF16)],
        compiler_params=_cparams(("parallel", "parallel", "arbitrary")),
        name="peer_route",
    )(x, g.reshape(1, d), mod, wq, keys)


GELU_C = math.sqrt(2.0 / math.pi)
PEER_ROWS = 8
PEER_VMEM_LIMIT = 58 * 1024 * 1024
GATE_RB = 32
GATE_IG = 4


def _gelu_tanh(x):
    k = -2.0 * GELU_C * LOG2E
    return x / (1.0 + jnp.exp2(x * (k + (k * 0.044715) * (x * x))))


def _peer_pipe_kernel(*refs, final, n_pairs):
    refs = list(refs)
    x_ref, mod_ref, xnt_ref, e1_ref, th_ref, s2_ref, e2_ref, u_ref, vt_ref = refs[:9]
    pos = 9
    if final:
        fg_ref = refs[pos]
        pos += 1
    o_ref, acc_scr = refs[pos:pos + 2]
    a_scr = (refs[pos + 2:pos + 4], refs[pos + 4:pos + 6])
    wg_scr = (refs[pos + 6:pos + 8], refs[pos + 8:pos + 10])
    be1_scr = refs[pos + 10:pos + 12]
    bth_scr = refs[pos + 12:pos + 14]
    g = pl.program_id(2)
    eb = PEER_ROWS * PEER_NKEYS
    tm = s2_ref.shape[2]

    def scores(p):
        for k in range(2):
            a_scr[p][k][...] = jnp.dot(u_ref[k * eb:(k + 1) * eb, :], xnt_ref[...], preferred_element_type=F32)

    def gate_one(a_ref, wg_ref, k):
        zero8 = jnp.zeros((8, tm), F32)
        for h in range(PEER_HEADS):
            for ii in range(PEER_ROWS):
                be1_scr[k][h * PEER_ROWS + ii] = e1_ref[h, pl.ds(k * PEER_ROWS + ii, 1), :] + zero8
                bth_scr[k][h * PEER_ROWS + ii] = th_ref[h, pl.ds(k * PEER_ROWS + ii, 1), :] + zero8
        for c0 in range(0, tm, LANES):
            cs = slice(c0, c0 + LANES)
            for r0 in range(0, PEER_NKEYS, GATE_RB):
                for ig in range(0, PEER_ROWS, GATE_IG):
                    ws = [None] * GATE_IG
                    for h in range(PEER_HEADS):
                        s2 = s2_ref[h, r0:r0 + GATE_RB, cs].reshape(GATE_RB // 8, 8, LANES)
                        e2 = e2_ref[h, r0:r0 + GATE_RB, cs].reshape(GATE_RB // 8, 8, LANES)
                        for j in range(GATE_IG):
                            row = h * PEER_ROWS + ig + j
                            contrib = jnp.where(s2 >= bth_scr[k][row, :, cs][None], e2 * be1_scr[k][row, :, cs][None], 0.0)
                            ws[j] = contrib if ws[j] is None else ws[j] + contrib
                    for j in range(GATE_IG):
                        rs = slice((ig + j) * PEER_NKEYS + r0, (ig + j) * PEER_NKEYS + r0 + GATE_RB)
                        wj = ws[j].reshape(GATE_RB, LANES)
                        wg_ref[rs, cs] = (wj * _gelu_tanh(a_ref[rs, cs])).astype(BF16)

    def gates(p):
        for k in range(2):
            gate_one(a_scr[1 - p][k], wg_scr[p][k], k)

    def accums(p):
        for k in range(2):
            acc_scr[...] += jnp.dot(vt_ref[:, k * eb:(k + 1) * eb], wg_scr[1 - p][k][...], preferred_element_type=F32)

    @pl.when(g == 0)
    def _():
        acc_scr[...] = jnp.zeros_like(acc_scr)
        scores(0)

    @pl.when(g == 1)
    def _():
        scores(1)
        gates(1)

    for p in range(2):
        @pl.when((g >= 2) & (g < n_pairs) & (g % 2 == p))
        def _(p=p):
            scores(p)
            gates(p)
            accums(p)

    @pl.when(g == n_pairs)
    def _():
        gates(n_pairs % 2)
        accums(n_pairs % 2)

    @pl.when(g == n_pairs + 1)
    def _():
        accums((n_pairs + 1) % 2)
        y = x_ref[...] + mod_ref[5:6, :] * acc_scr[...].T
        if final:
            ms = jnp.mean(y * y, axis=-1, keepdims=True)
            y = y * lax.rsqrt(ms + NORM_EPS) * fg_ref[...]
        o_ref[...] = y


def _peer_pipe(x, mod, xnt, e1, th, s2, e2, u, vt, final_g):
    b, l, d = x.shape
    tm = min(512, l)
    nt = l // tm
    eb = PEER_ROWS * PEER_NKEYS
    n_pairs = u.shape[0] // (2 * eb)
    assert n_pairs >= 2
    final = final_g is not None
    tok = lambda bi, i: bi * nt + i
    clamp = lambda v: jnp.clip(v, 0, n_pairs - 1)
    rows = pl.BlockSpec((PEER_HEADS, 2 * PEER_ROWS, tm), lambda bi, i, g: (0, clamp(g - 1), tok(bi, i)))
    keyblk = pl.BlockSpec((PEER_HEADS, PEER_NKEYS, tm), lambda bi, i, g: (0, 0, tok(bi, i)))
    in_specs = [
        pl.BlockSpec((None, tm, d), lambda bi, i, g: (bi, i, 0)),
        pl.BlockSpec((None, 6, d), lambda bi, i, g: (bi, 0, 0)),
        pl.BlockSpec((d, tm), lambda bi, i, g: (0, tok(bi, i))),
        rows, rows, keyblk, keyblk,
        pl.BlockSpec((2 * eb, d), lambda bi, i, g: (clamp(g), 0)),
        pl.BlockSpec((d, 2 * eb), lambda bi, i, g: (0, clamp(g - 2))),
    ]
    args = [x, mod, xnt, e1, th, s2, e2, u, vt]
    if final:
        in_specs.append(pl.BlockSpec((1, d), lambda bi, i, g: (0, 0)))
        args.append(final_g.reshape(1, d))
    bcast = pltpu.VMEM((PEER_HEADS * PEER_ROWS, 8, tm), F32)
    return pl.pallas_call(
        functools.partial(_peer_pipe_kernel, final=final, n_pairs=n_pairs),
        grid=(b, nt, n_pairs + 2),
        in_specs=in_specs,
        out_specs=pl.BlockSpec((None, tm, d), lambda bi, i, g: (bi, i, 0)),
        out_shape=jax.ShapeDtypeStruct((b, l, d), F32),
        scratch_shapes=([pltpu.VMEM((d, tm), F32)] + [pltpu.VMEM((eb, tm), F32)] * 4 + [pltpu.VMEM((eb, tm), BF16)] * 4
                        + [bcast] * 4),
        compiler_params=_cparams(("parallel", "parallel", "arbitrary"), PEER_VMEM_LIMIT),
        name="peer_experts",
    )(*args)


def _pad_cols(a, width):
    return jnp.pad(a, ((0, 0), (0, width - a.shape[1])))


def _layout_w_in(w):
    d = w.shape[0]
    z = lambda n: jnp.zeros((d, n), w.dtype)
    gla, hg, mla, dif = 0, 800, 2080, 2432
    parts = [w[:, gla:gla + 768], w[:, hg:hg + 768],
             w[:, mla:mla + 192], z(64), w[:, mla + 192:mla + 320], z(64), w[:, mla + 320:mla + 352], z(32)]
    for base in (dif, dif + 256, dif + 512):
        for h in range(DIFF_HEADS):
            parts += [w[:, base + 64 * h:base + 64 * (h + 1)], z(64)]
    parts += [w[:, gla + 768:gla + 800], z(224), w[:, hg + 768:hg + 1280]]
    out = jnp.concatenate(parts, axis=1)
    assert out.shape[1] == MAIN_COLS + GATE_COLS
    return out.astype(BF16)


def _layout_mla(w_uq, w_ukv):
    dq = MLA_NOPE + MLA_ROPE
    wq = jnp.concatenate([_pad_cols(w_uq[:, dq * h:dq * (h + 1)], LANES) for h in range(MLA_HEADS)], axis=1)
    wq = jnp.pad(wq, ((0, 256 - MLA_Q_RANK), (0, 0)))
    per = MLA_NOPE + MLA_DV
    wk = jnp.concatenate([_pad_cols(w_ukv[:, per * h:per * h + MLA_NOPE], LANES) for h in range(MLA_HEADS)], axis=1)
    wv = jnp.concatenate([_pad_cols(w_ukv[:, per * h + MLA_NOPE:per * (h + 1)], LANES) for h in range(MLA_HEADS)], axis=1)
    return wq.astype(BF16), jnp.concatenate([wk, wv], axis=1).astype(BF16)


def kernel(x, c, ctx, c_ctx, ada_w, ada_b, norm_mix_g, norm_ffn_g, w_in, w_out, gla_gate_w, gla_gate_b, gla_norm_g, hgrn_lb_raw, hgrn_norm_g, mla_q_norm_g, mla_kv_norm_g, mla_w_uq, mla_w_ukv, diff_lambda, diff_norm_g, peer_wq, peer_keys, peer_u, peer_v, final_norm_g):
    b, l, d = x.shape
    lc = ctx.shape[1]
    depth = ada_w.shape[0]
    assert lc <= ATT_TK, "context keys are consumed as one attention step"

    rows = -(-(b + 1) // 8) * 8
    cpad = jnp.zeros((rows, d), F32).at[:b].set(c).at[b].set(c_ctx)
    mod_all = _adaln(cpad, ada_w, ada_b)

    lb_sm = jax.nn.softmax(hgrn_lb_raw.astype(F32), axis=0)
    lb_all = jnp.cumsum(lb_sm, axis=0) - lb_sm[0]

    lat_mla_tab = _rope_tables(l, 64, 32, True)
    ctx_mla_tab = _rope_tables(lc, 64, 32, False)
    lat_dif_tab = _rope_tables(l, 0, 64, True)
    ctx_dif_tab = _rope_tables(lc, 0, 64, False)

    xc = ctx
    for li in range(depth):
        need_ctx = li < depth - 1
        lam_init = 0.8 - 0.6 * math.exp(-0.3 * li)
        mod_l = mod_all[li, :b].reshape(b, 6, d)
        mod_c = jnp.broadcast_to(mod_all[li, b].reshape(1, 6, d), (b, 6, d))

        w_perm = _layout_w_in(w_in[li])
        w_o = w_out[li].astype(BF16)
        wg = jnp.zeros((2, 256, 128), F32)
        wg = wg.at[0, 0:GLA_GATE_RANK].set(gla_gate_w[li, 0]).at[1, GLA_GATE_RANK:2 * GLA_GATE_RANK].set(gla_gate_w[li, 1])
        wg = wg.astype(BF16)
        bg = gla_gate_b[li].reshape(2, 1, 128)
        gla_gn = jnp.tile(gla_norm_g[li], GLA_HEADS).reshape(1, 256)
        hg_gn = jnp.tile(hgrn_norm_g[li], HG_HEADS).reshape(1, 256)
        lb = lb_all[li]
        hg_p = jnp.zeros((8, 256), F32).at[0].set(jnp.log(jnp.maximum(lb, LB_FLOOR))).at[1].set(jnp.log1p(-lb)).at[2].set(1.0 - lb)
        hg_dummy = jnp.zeros((8, 128), F32)
        gq = jnp.pad(mla_q_norm_g[li], (0, 256 - MLA_Q_RANK)).reshape(1, 256)
        gkv = mla_kv_norm_g[li].reshape(1, MLA_KV_RANK)
        wq_mla, wkv_mla = _layout_mla(mla_w_uq[li], mla_w_ukv[li])
        lq1, lk1, lq2, lk2 = diff_lambda[li, 0], diff_lambda[li, 1], diff_lambda[li, 2], diff_lambda[li, 3]
        lam = (jnp.exp(jnp.sum(lq1 * lk1).astype(F32)) - jnp.exp(jnp.sum(lq2 * lk2).astype(F32)) + lam_init)
        lam_row = jnp.full((1, LANES), lam, F32)
        dif_gn = jnp.tile(diff_norm_g[li], 2).reshape(1, LANES)
        wq_peer = peer_wq[li].astype(BF16)
        keys = peer_keys[li].astype(BF16)
        u_bf = peer_u[li].astype(BF16)
        vt_bf = peer_v[li].astype(BF16).T

        main_c, gate_c = _inproj(xc, norm_mix_g[li], mod_c, w_perm)
        main_l, gate_l = _inproj(x, norm_mix_g[li], mod_l, w_perm)

        mixes_c, mixes_l = [], []
        for mixer, pa_f, pb_f, pa_b, pb_b, gn, dk in (
            ("gla", wg[0], bg[0], wg[1], bg[1], gla_gn, 128),
            ("hg", hg_p, hg_dummy, hg_p, hg_dummy, hg_gn, 256),
        ):
            zero = jnp.zeros((b, 256, dk), F32)
            o_cf, s_f = _scan(main_c, gate_c, pa_f, pb_f, gn, zero, None, mixer=mixer, reverse=False, chunk=64)
            mix_c, s_b = _scan(main_c, gate_c, pa_b, pb_b, gn, zero, o_cf, mixer=mixer, reverse=True, chunk=64)
            o_lf, _ = _scan(main_l, gate_l, pa_f, pb_f, gn, s_f, None, mixer=mixer, reverse=False, chunk=64)
            mix_l, _ = _scan(main_l, gate_l, pa_b, pb_b, gn, s_b, o_lf, mixer=mixer, reverse=True, chunk=64)
            mixes_c.append(mix_c)
            mixes_l.append(mix_l)

        qm_c, km_c, vm_c = _mla_prep(main_c, gq, gkv, wq_mla, wkv_mla, ctx_mla_tab)
        qm_l, km_l, vm_l = _mla_prep(main_l, gq, gkv, wq_mla, wkv_mla, lat_mla_tab)
        mla_l = _attention(qm_l, km_c, vm_c, km_l, vm_l, None, n_maps=1, lam_init=lam_init)
        qd_c, kd_c, vd_c = _diff_prep(main_c, ctx_dif_tab)
        qd_l, kd_l, vd_l = _diff_prep(main_l, lat_dif_tab)
        dif_l = _attention(qd_l, kd_c, vd_c, kd_l, vd_l, (lam_row, dif_gn), n_maps=2, lam_init=lam_init)

        x = _outproj(x, mixes_l[0], mixes_l[1], mla_l, dif_l, w_o, mod_l)
        rt = _peer_route(x, norm_ffn_g[li], mod_l, wq_peer, keys)
        x = _peer_pipe(x, mod_l, *rt, u_bf, vt_bf, None if need_ctx else final_norm_g)

        if need_ctx:
            mla_c = _attention(qm_c, km_c, vm_c, None, None, None, n_maps=1, lam_init=lam_init)
            dif_c = _attention(qd_c, kd_c, vd_c, None, None, (lam_row, dif_gn), n_maps=2, lam_init=lam_init)
            xc = _outproj(xc, mixes_c[0], mixes_c[1], mla_c, dif_c, w_o, mod_c)
            rtc = _peer_route(xc, norm_ffn_g[li], mod_c, wq_peer, keys)
            xc = _peer_pipe(xc, mod_c, *rtc, u_bf, vt_bf, None)
    return x
```

```python
import functools
import math

import jax
import jax.numpy as jnp
import numpy as np
from jax import lax
from jax.experimental import pallas as pl
from jax.experimental.pallas import tpu as pltpu

F32 = jnp.float32
BF16 = jnp.bfloat16

NORM_EPS = 1e-6
ROPE_BASE = 10000.0
GRID_W = 64
LB_FLOOR = 1e-30

GLA_HEADS, GLA_DK, GLA_DV, GLA_GATE_RANK, GLA_GATE_NORM = 4, 32, 64, 16, 16.0
HG_HEADS, HG_DIM = 4, 64
MLA_HEADS, MLA_Q_RANK, MLA_KV_RANK, MLA_NOPE, MLA_ROPE, MLA_DV = 4, 192, 128, 64, 32, 64
DIFF_HEADS, DIFF_DQK, DIFF_DV = 4, 32, 64
PEER_HEADS, PEER_NKEYS, PEER_DQ, PEER_TOPK = 8, 128, 256, 16

LANES = 128
VMEM_LIMIT = 48 * 1024 * 1024
EXP_CAP = 80.0
NEG_BIG = -1e30
ATT_TK = 512
LOG2E = 1.4426950408889634
ATT_R = 512
ATT_STEPS_PER_TRIP = 2
VT_ROWS = 80

COL_GLA, COL_HG, COL_MLA, COL_DQ, COL_DK, COL_DV, MAIN_COLS = 0, 768, 1536, 2048, 2560, 3072, 3584
GATE_COLS = 768

STAIR = [(a, b) for a in range(PEER_TOPK) for b in range(PEER_TOPK) if (a + 1) * (b + 1) <= PEER_TOPK]


def _cparams(sem, vmem=VMEM_LIMIT):
    return pltpu.CompilerParams(dimension_semantics=sem, vmem_limit_bytes=vmem)


def _sigmoid(x):
    return 1.0 / (1.0 + jnp.exp(-x))


def _logsig(x):
    return jnp.minimum(x, 0.0) - jnp.log(1.0 + jnp.exp(-jnp.abs(x)))


def _nt(a, b):
    return lax.dot_general(a, b, (((1,), (1,)), ((), ())), preferred_element_type=F32)


def _tn(a, b):
    return lax.dot_general(a, b, (((0,), (0,)), ((), ())), preferred_element_type=F32)


def _adaln_kernel(c_ref, w_ref, b_ref, o_ref):
    c = c_ref[...]
    s = c * _sigmoid(c)
    o_ref[...] = jnp.dot(s.astype(BF16), w_ref[...].astype(BF16), preferred_element_type=F32) + b_ref[...]


def _adaln(cpad, ada_w, ada_b):
    depth, d, n6 = ada_w.shape
    rows = cpad.shape[0]
    tn = 512
    return pl.pallas_call(
        _adaln_kernel,
        grid=(depth, n6 // tn),
        in_specs=[
            pl.BlockSpec((rows, d), lambda l, j: (0, 0)),
            pl.BlockSpec((None, d, tn), lambda l, j: (l, 0, j)),
            pl.BlockSpec((None, 1, tn), lambda l, j: (l, 0, j)),
        ],
        out_specs=pl.BlockSpec((None, rows, tn), lambda l, j: (l, 0, j)),
        out_shape=jax.ShapeDtypeStruct((depth, rows, n6), F32),
        compiler_params=_cparams(("parallel", "parallel")),
        name="adaln",
    )(cpad, ada_w, ada_b.reshape(depth, 1, n6))


def _modnorm(x, g, mod_ref, shift_row, scale_row):
    ms = jnp.mean(x * x, axis=-1, keepdims=True)
    h = x * lax.rsqrt(ms + NORM_EPS) * g
    return h * (1.0 + mod_ref[scale_row:scale_row + 1, :]) + mod_ref[shift_row:shift_row + 1, :]


def _inproj_kernel(x_ref, g_ref, mod_ref, w_ref, main_ref, gate_ref):
    h = _modnorm(x_ref[...], g_ref[...], mod_ref, 0, 1)
    y = jnp.dot(h.astype(BF16), w_ref[...], preferred_element_type=F32)
    main_ref[...] = y[:, :MAIN_COLS].astype(BF16)
    gate_ref[...] = y[:, MAIN_COLS:]


def _inproj(x, g, mod, w_perm):
    b, l, d = x.shape
    tm = min(256, l)
    nc = w_perm.shape[1]
    return pl.pallas_call(
        _inproj_kernel,
        grid=(b, l // tm),
        in_specs=[
            pl.BlockSpec((None, tm, d), lambda bi, i: (bi, i, 0)),
            pl.BlockSpec((1, d), lambda bi, i: (0, 0)),
            pl.BlockSpec((None, 6, d), lambda bi, i: (bi, 0, 0)),
            pl.BlockSpec((d, nc), lambda bi, i: (0, 0)),
        ],
        out_specs=[
            pl.BlockSpec((None, tm, MAIN_COLS), lambda bi, i: (bi, i, 0)),
            pl.BlockSpec((None, tm, GATE_COLS), lambda bi, i: (bi, i, 0)),
        ],
        out_shape=[
            jax.ShapeDtypeStruct((b, l, MAIN_COLS), BF16),
            jax.ShapeDtypeStruct((b, l, GATE_COLS), F32),
        ],
        compiler_params=_cparams(("parallel", "parallel")),
        name="inproj",
    )(x, g.reshape(1, d), mod, w_perm)


def _scan_kernel(*refs, mixer, reverse, finish, chunk):
    if finish:
        main_ref, gate_ref, pa_ref, pb_ref, gn_ref, s0_ref, oprev_ref, o_ref, sfin_ref, st_scr = refs
    else:
        main_ref, gate_ref, pa_ref, pb_ref, gn_ref, s0_ref, o_ref, sfin_ref, st_scr = refs
        oprev_ref = None
    i = pl.program_id(1)

    @pl.when(i == 0)
    def _():
        st_scr[...] = s0_ref[...]

    main = main_ref[...]
    t = main.shape[0]
    c = chunk
    nc = t // c
    heads = 4
    dv = 256
    v = main[:, 256:512]
    gcol = main[:, 512:768].astype(F32)
    if mixer == "gla":
        dk = 128
        q = main[:, 0:128].astype(F32) * (GLA_DK ** -0.5)
        k = main[:, 128:256].astype(F32)
        z = jnp.dot(gate_ref[...].astype(BF16), pa_ref[...], preferred_element_type=F32) + pb_ref[...]
        g = _logsig(z) * (1.0 / GLA_GATE_NORM)
    else:
        dk = 256
        qq = main[:, 0:256].astype(F32)
        q = qq * _sigmoid(qq)
        z = gate_ref[...]
        la = pa_ref[0:1, :]
        lb = pa_ref[1:2, :] + _logsig(z)
        g = jnp.maximum(la, lb) + jnp.log(1.0 + jnp.exp(-jnp.abs(la - lb)))
        k = pa_ref[2:3, :] * _sigmoid(-z)
    hk = dk // heads
    hv = dv // heads

    rt = lax.broadcasted_iota(jnp.int32, (t, t), 0)
    ct = lax.broadcasted_iota(jnp.int32, (t, t), 1)
    same = (rt // c) == (ct // c)
    tri = jnp.where(same & ((ct >= rt) if reverse else (ct <= rt)), 1.0, 0.0).astype(BF16)
    g1 = g.astype(BF16)
    r1 = g - g1.astype(F32)
    g2 = r1.astype(BF16)
    g3 = (r1 - g2.astype(F32)).astype(BF16)
    bcum = (jnp.dot(tri, g1, preferred_element_type=F32) + jnp.dot(tri, g2, preferred_element_type=F32)
            + jnp.dot(tri, g3, preferred_element_type=F32))

    b3 = bcum.reshape(nc, c, dk)
    q3 = q.reshape(nc, c, dk)
    k3 = k.reshape(nc, c, dk)
    if reverse:
        bmid = b3[:, c // 2:c // 2 + 1, :]
        bend = b3[:, 0:1, :]
    else:
        bmid = b3[:, c // 2 - 1:c // 2, :]
        bend = b3[:, c - 1:c, :]
    eb = b3 - bmid
    qa = (q3 * jnp.exp(jnp.minimum(eb, EXP_CAP))).reshape(t, dk).astype(BF16)
    ka = (k3 * jnp.exp(jnp.minimum(-eb, EXP_CAP))).reshape(t, dk).astype(BF16)
    qi = (q3 * jnp.exp(b3)).reshape(t, dk).astype(BF16)
    kb = (k3 * jnp.exp(bend - b3)).reshape(t, dk).astype(BF16)
    dec = jnp.exp(bend)

    lane_k = lax.broadcasted_iota(jnp.int32, (1, dk), 1) // hk
    lane_v = lax.broadcasted_iota(jnp.int32, (1, dv), 1) // hv
    rr = lax.broadcasted_iota(jnp.int32, (c, heads * c), 0)
    cc = lax.broadcasted_iota(jnp.int32, (c, heads * c), 1) % c
    causal = (cc >= rr) if reverse else (cc <= rr)
    bd = (lax.broadcasted_iota(jnp.int32, (dv, dk), 0) // hv) == (lax.broadcasted_iota(jnp.int32, (dv, dk), 1) // hk)

    o_intra = []
    upd = []
    for ci in range(nc):
        sl = slice(ci * c, (ci + 1) * c)
        ka_c = ka[sl]
        v_c = v[sl]
        kst = jnp.concatenate([jnp.where(lane_k == h, ka_c, 0) for h in range(heads)], axis=0)
        vst = jnp.concatenate([jnp.where(lane_v == h, v_c, 0) for h in range(heads)], axis=0)
        sw = _nt(qa[sl], kst)
        p = jnp.where(causal, sw, 0.0).astype(BF16)
        o_intra.append(jnp.dot(p, vst, preferred_element_type=F32))
        upd.append(jnp.where(bd, _tn(v_c, kb[sl]), 0.0))

    outs = [None] * nc
    order = range(nc - 1, -1, -1) if reverse else range(nc)
    for ci in order:
        sl = slice(ci * c, (ci + 1) * c)
        st = st_scr[...]
        outs[ci] = o_intra[ci] + _nt(qi[sl], st.astype(BF16))
        st_scr[...] = st * dec[ci] + upd[ci]
    o = jnp.concatenate(outs, axis=0)

    @pl.when(i == pl.num_programs(1) - 1)
    def _():
        sfin_ref[...] = st_scr[...]

    if not finish:
        o_ref[...] = o
    else:
        ot = o + oprev_ref[...]
        sq = ot * ot
        jv = ((lax.broadcasted_iota(jnp.int32, (dv, dv), 0) // hv)
              == (lax.broadcasted_iota(jnp.int32, (dv, dv), 1) // hv))
        jm = jnp.where(jv, 1.0, 0.0).astype(BF16)
        s1 = sq.astype(BF16)
        s2 = (sq - s1.astype(F32)).astype(BF16)
        ms = (jnp.dot(s1, jm, preferred_element_type=F32) + jnp.dot(s2, jm, preferred_element_type=F32)) * (1.0 / hv)
        y = ot * lax.rsqrt(ms + NORM_EPS) * gn_ref[...]
        o_ref[...] = (y * (gcol * _sigmoid(gcol))).astype(BF16)


def _scan(main, gate, pa, pb, gn, s0, oprev, *, mixer, reverse, chunk):
    b, l, _ = main.shape
    t = min(256, l)
    nb = l // t
    finish = oprev is not None
    dk = 128 if mixer == "gla" else 256
    dv = 256
    main_blk = 0 if mixer == "gla" else 1
    gate_blk = 0 if mixer == "gla" else (2 if reverse else 1)

    def tok(bi, i):
        return (nb - 1 - i) if reverse else i

    in_specs = [
        pl.BlockSpec((None, t, 768), lambda bi, i: (bi, tok(bi, i), main_blk)),
        pl.BlockSpec((None, t, 256), lambda bi, i: (bi, tok(bi, i), gate_blk)),
        pl.BlockSpec(pa.shape, lambda bi, i: (0, 0)),
        pl.BlockSpec(pb.shape, lambda bi, i: (0, 0)),
        pl.BlockSpec((1, dv), lambda bi, i: (0, 0)),
        pl.BlockSpec((None, dv, dk), lambda bi, i: (bi, 0, 0)),
    ]
    args = [main, gate, pa, pb, gn, s0]
    if finish:
        in_specs.append(pl.BlockSpec((None, t, dv), lambda bi, i: (bi, tok(bi, i), 0)))
        args.append(oprev)
    return pl.pallas_call(
        functools.partial(_scan_kernel, mixer=mixer, reverse=reverse, finish=finish, chunk=chunk),
        grid=(b, nb),
        in_specs=in_specs,
        out_specs=[
            pl.BlockSpec((None, t, dv), lambda bi, i: (bi, tok(bi, i), 0)),
            pl.BlockSpec((None, dv, dk), lambda bi, i: (bi, 0, 0)),
        ],
        out_shape=[
            jax.ShapeDtypeStruct((b, l, dv), BF16 if finish else F32),
            jax.ShapeDtypeStruct((b, dv, dk), F32),
        ],
        scratch_shapes=[pltpu.VMEM((dv, dk), F32)],
        compiler_params=_cparams(("parallel", "arbitrary")),
        name=f"scan_{mixer}_{'bwd' if reverse else 'fwd'}",
    )(*args)


def _rope(x, cos, sin_lo, sin_hi):
    return x * cos + pltpu.roll(x, LANES - 8, 1) * sin_lo + pltpu.roll(x, 8, 1) * sin_hi


def _rope_tables(l, first_lane, n_lanes, rotate):
    lane = np.arange(LANES)
    d = (lane - first_lane) % 32
    active = (lane >= first_lane) & (lane < first_lane + n_lanes)
    freqs = ROPE_BASE ** (-(d % 8).astype(np.float32) / 8.0)
    tt = jnp.arange(l, dtype=jnp.int32)
    rows = (tt // GRID_W).astype(F32)[:, None]
    cols = (tt % GRID_W).astype(F32)[:, None]
    pos = jnp.where(jnp.asarray(d < 16)[None, :], rows, cols)
    ang = pos * jnp.asarray(freqs, F32)[None, :]
    act = jnp.asarray(active)[None, :] & rotate
    lo = jnp.asarray((d % 16) < 8)[None, :]
    cos = jnp.where(act, jnp.cos(ang), 1.0)
    sin = jnp.where(act, jnp.sin(ang), 0.0)
    return cos, jnp.where(lo, -sin, 0.0), jnp.where(lo, 0.0, sin)


def _vt_block(vgrp):
    lane = lax.broadcasted_iota(jnp.int32, (1, LANES), 1)
    return jnp.where(lane == 64, 1.0, vgrp).T[:VT_ROWS].astype(BF16)


def _mla_prep_kernel(slab_ref, gq_ref, gkv_ref, wq_ref, wkv_ref, cos_ref, slo_ref, shi_ref, qt_ref, k_ref, vt_ref):
    slab = slab_ref[...]
    cos, slo, shi = cos_ref[...], slo_ref[...], shi_ref[...]
    cq = slab[:, 0:256].astype(F32)
    qn = cq * lax.rsqrt(jnp.sum(cq * cq, axis=-1, keepdims=True) * (1.0 / MLA_Q_RANK) + NORM_EPS) * gq_ref[...]
    qall = jnp.dot(qn.astype(BF16), wq_ref[...], preferred_element_type=F32)
    ckv = slab[:, 256:384].astype(F32)
    kvn = ckv * lax.rsqrt(jnp.mean(ckv * ckv, axis=-1, keepdims=True) + NORM_EPS) * gkv_ref[...]
    kvall = jnp.dot(kvn.astype(BF16), wkv_ref[...], preferred_element_type=F32)
    kr = _rope(slab[:, 384:512].astype(F32), cos, slo, shi)
    scale = (MLA_NOPE + MLA_ROPE) ** -0.5 * LOG2E
    for h in range(MLA_HEADS):
        qh = _rope(qall[:, h * LANES:(h + 1) * LANES], cos, slo, shi)
        qt_ref[h] = (qh * scale).T.astype(BF16)
        k_ref[h] = (kvall[:, h * LANES:(h + 1) * LANES] + kr).astype(BF16)
        vt_ref[h] = _vt_block(kvall[:, (MLA_HEADS + h) * LANES:(MLA_HEADS + h + 1) * LANES])


def _attn_prep_specs(b, l, tm, heads):
    qt = pl.BlockSpec((None, heads, LANES, tm), lambda bi, i: (bi, 0, 0, i))
    kk = pl.BlockSpec((None, heads, tm, LANES), lambda bi, i: (bi, 0, i, 0))
    vt = pl.BlockSpec((None, heads, None, VT_ROWS, tm), lambda bi, i: (bi, 0, i, 0, 0))
    shapes = [
        jax.ShapeDtypeStruct((b, heads, LANES, l), BF16),
        jax.ShapeDtypeStruct((b, heads, l, LANES), BF16),
        jax.ShapeDtypeStruct((b, heads, l // tm, VT_ROWS, tm), BF16),
    ]
    return [qt, kk, vt], shapes


def _mla_prep(main, gq, gkv, wq, wkv, tables):
    b, l, _ = main.shape
    tm = min(ATT_TK, l)
    cos, slo, shi = tables
    tab = pl.BlockSpec((tm, LANES), lambda bi, i: (i, 0))
    out_specs, out_shape = _attn_prep_specs(b, l, tm, MLA_HEADS)
    return pl.pallas_call(
        _mla_prep_kernel,
        grid=(b, l // tm),
        in_specs=[
            pl.BlockSpec((None, tm, 512), lambda bi, i: (bi, i, COL_MLA // 512)),
            pl.BlockSpec(gq.shape, lambda bi, i: (0, 0)),
            pl.BlockSpec(gkv.shape, lambda bi, i: (0, 0)),
            pl.BlockSpec(wq.shape, lambda bi, i: (0, 0)),
            pl.BlockSpec(wkv.shape, lambda bi, i: (0, 0)),
            tab, tab, tab,
        ],
        out_specs=out_specs,
        out_shape=out_shape,
        compiler_params=_cparams(("parallel", "parallel")),
        name="mla_prep",
    )(main, gq, gkv, wq, wkv, cos, slo, shi)


def _diff_prep_kernel(q_in, k_in, v_in, cos_ref, slo_ref, shi_ref, qt_ref, k_ref, vt_ref):
    cos, slo, shi = cos_ref[...], slo_ref[...], shi_ref[...]
    qs = q_in[...]
    ks = k_in[...]
    vs = v_in[...]
    scale = DIFF_DQK ** -0.5 * LOG2E
    for h in range(DIFF_HEADS):
        sl = slice(h * LANES, (h + 1) * LANES)
        qt_ref[h] = (_rope(qs[:, sl].astype(F32), cos, slo, shi) * scale).T.astype(BF16)
        k_ref[h] = _rope(ks[:, sl].astype(F32), cos, slo, shi).astype(BF16)
        vt_ref[h] = _vt_block(vs[:, sl].astype(F32))


def _diff_prep(main, tables):
    b, l, _ = main.shape
    tm = min(ATT_TK, l)
    cos, slo, shi = tables
    tab = pl.BlockSpec((tm, LANES), lambda bi, i: (i, 0))
    out_specs, out_shape = _attn_prep_specs(b, l, tm, DIFF_HEADS)
    return pl.pallas_call(
        _diff_prep_kernel,
        grid=(b, l // tm),
        in_specs=[
            pl.BlockSpec((None, tm, 512), lambda bi, i: (bi, i, COL_DQ // 512)),
            pl.BlockSpec((None, tm, 512), lambda bi, i: (bi, i, COL_DK // 512)),
            pl.BlockSpec((None, tm, 512), lambda bi, i: (bi, i, COL_DV // 512)),
            tab, tab, tab,
        ],
        out_specs=out_specs,
        out_shape=out_shape,
        compiler_params=_cparams(("parallel", "parallel")),
        name="diff_prep",
    )(main, main, main, cos, slo, shi)


def _attn_kernel(*refs, n_maps, has_lat, lam_init):
    refs = list(refs)
    qt_ref, kc_ref, vtc_ref = refs[:3]
    pos = 3
    if has_lat:
        kl_ref, vtl_ref = refs[pos:pos + 2]
        pos += 2
    if n_maps == 2:
        lam_ref, gn_ref = refs[pos:pos + 2]
        pos += 2
    o_ref = refs[pos]
    if has_lat:
        m_scr, acc_scr, s_scr, cm_scr, al_scr, p_scr = refs[pos + 1:pos + 7]
    tq = qt_ref.shape[2]
    feat = lax.broadcasted_iota(jnp.int32, (LANES, 1), 0)

    qts = []
    for hh in range(2):
        qt = qt_ref[hh]
        if n_maps == 2:
            zero = jnp.zeros_like(qt)
            qt = jnp.concatenate(
                [jnp.where((feat >= DIFF_DQK * mi) & (feat < DIFF_DQK * (mi + 1)), qt, zero) for mi in range(2)], axis=1)
        qts.append(qt)

    accs = []
    for hh in range(2):
        st = jnp.dot(kc_ref[hh], qts[hh], preferred_element_type=F32)
        m0 = jnp.max(st, axis=0, keepdims=True)
        acc0 = jnp.dot(vtc_ref[hh, 0], jnp.exp2(st - m0).astype(BF16), preferred_element_type=F32)
        accs.append(acc0)
        if has_lat:
            m_scr[hh] = m0
            acc_scr[hh] = acc0

    if has_lat:
        n = vtl_ref.shape[1]
        tk = vtl_ref.shape[3]

        def scores(j, slot):
            off = pl.multiple_of(j * tk, tk)
            for hh in range(2):
                st = jnp.dot(kl_ref[hh, pl.ds(off, tk), :], qts[hh], preferred_element_type=F32)
                s_scr[hh, slot] = st
                cm_scr[hh, slot] = jnp.max(st, axis=0, keepdims=True)

        def numer(slot):
            for hh in range(2):
                m_old = m_scr[hh]
                m_new = jnp.maximum(m_old, cm_scr[hh, slot])
                al_scr[hh, slot] = jnp.exp2(m_old - m_new)
                p_scr[hh, slot] = jnp.exp2(s_scr[hh, slot] - m_new).astype(BF16)
                m_scr[hh] = m_new

        def values(j, slot):
            for hh in range(2):
                acc_scr[hh] = (al_scr[hh, slot] * acc_scr[hh]
                               + jnp.dot(vtl_ref[hh, j], p_scr[hh, slot], preferred_element_type=F32))

        first = n % 2
        if first:
            scores(0, 0)
            numer(0)
            values(0, 0)
        if n > first:
            for hh in range(2):
                p_scr[hh, 1] = jnp.zeros(p_scr.shape[2:], BF16)
                al_scr[hh, 1] = jnp.ones(al_scr.shape[2:], F32)
            scores(first, 0)

            per_trip = ATT_STEPS_PER_TRIP if (n - first) % ATT_STEPS_PER_TRIP == 0 else 2

            def body(t, carry):
                for u in range(per_trip):
                    c = first + per_trip * t + u
                    slot = u % 2
                    numer(slot)
                    values(jnp.maximum(c - 1, first), 1 - slot)
                    scores(jnp.minimum(c + 1, n - 1), 1 - slot)
                return carry

            lax.fori_loop(0, (n - first) // per_trip, body, 0)
            values(n - 1, 1)
        accs = [acc_scr[0], acc_scr[1]]

    outs = []
    for hh in range(2):
        acc = accs[hh]
        ot = acc[0:64, :] / acc[64:65, :]
        if n_maps == 2:
            oh = ot[:, :tq] - lam_ref[:, 0:1] * ot[:, tq:]
            ms = jnp.mean(oh * oh, axis=0, keepdims=True)
            ot = oh * lax.rsqrt(ms + NORM_EPS)
        outs.append(ot)
    o = jnp.concatenate(outs, axis=0).T
    if n_maps == 2:
        o = o * gn_ref[...] * (1.0 - lam_init)
    o_ref[...] = o.astype(BF16)


def _attention(qt, kc, vtc, kl, vtl, extra, *, n_maps, lam_init):
    b, nh, _, l = qt.shape
    lc = kc.shape[2]
    tq = min(ATT_R // n_maps, l)
    r = n_maps * tq
    has_lat = kl is not None
    scratch = []
    in_specs = [
        pl.BlockSpec((None, 2, LANES, tq), lambda bi, hp, i: (bi, hp, 0, i)),
        pl.BlockSpec((None, 2, lc, LANES), lambda bi, hp, i: (bi, hp, 0, 0)),
        pl.BlockSpec((None, 2) + vtc.shape[2:], lambda bi, hp, i: (bi, hp, 0, 0, 0)),
    ]
    args = [qt, kc, vtc]
    if has_lat:
        in_specs += [
            pl.BlockSpec((None, 2, kl.shape[2], LANES), lambda bi, hp, i: (bi, hp, 0, 0)),
            pl.BlockSpec((None, 2) + vtl.shape[2:], lambda bi, hp, i: (bi, hp, 0, 0, 0)),
        ]
        args += [kl, vtl]
        tk = vtl.shape[4]
        scratch = [
            pltpu.VMEM((2, 1, r), F32), pltpu.VMEM((2, VT_ROWS, r), F32),
            pltpu.VMEM((2, 2, tk, r), F32), pltpu.VMEM((2, 2, 1, r), F32), pltpu.VMEM((2, 2, 1, r), F32),
            pltpu.VMEM((2, 2, tk, r), BF16),
        ]
    if n_maps == 2:
        in_specs += [pl.BlockSpec((1, LANES), lambda bi, hp, i: (0, 0))] * 2
        args += list(extra)
    return pl.pallas_call(
        functools.partial(_attn_kernel, n_maps=n_maps, has_lat=has_lat, lam_init=lam_init),
        grid=(b, nh // 2, l // tq),
        in_specs=in_specs,
        out_specs=pl.BlockSpec((None, tq, LANES), lambda bi, hp, i: (bi, i, hp)),
        out_shape=jax.ShapeDtypeStruct((b, l, nh * 64), BF16),
        scratch_shapes=scratch,
        compiler_params=_cparams(("parallel", "parallel", "arbitrary")),
        name="attn_mla" if n_maps == 1 else "attn_diff",
    )(*args)


def _outproj_kernel(x_ref, a_ref, b_ref, m_ref, d_ref, w_ref, mod_ref, o_ref):
    mix = jnp.concatenate([a_ref[...], b_ref[...], m_ref[...], d_ref[...]], axis=-1)
    y = jnp.dot(mix, w_ref[...], preferred_element_type=F32)
    o_ref[...] = x_ref[...] + mod_ref[2:3, :] * y


def _outproj(x, a, bb, m, dd, w, mod):
    b, l, d = x.shape
    tm = min(512, l)
    slab = pl.BlockSpec((None, tm, 256), lambda bi, i: (bi, i, 0))
    return pl.pallas_call(
        _outproj_kernel,
        grid=(b, l // tm),
        in_specs=[
            pl.BlockSpec((None, tm, d), lambda bi, i: (bi, i, 0)),
            slab, slab, slab, slab,
            pl.BlockSpec(w.shape, lambda bi, i: (0, 0)),
            pl.BlockSpec((None, 6, d), lambda bi, i: (bi, 0, 0)),
        ],
        out_specs=pl.BlockSpec((None, tm, d), lambda bi, i: (bi, i, 0)),
        out_shape=jax.ShapeDtypeStruct((b, l, d), F32),
        compiler_params=_cparams(("parallel", "parallel")),
        name="outproj",
    )(x, a, bb, m, dd, w, mod)


def _top_rows(s, n):
    rows = []
    for r in range(n):
        mx = jnp.max(s, axis=0, keepdims=True)
        rows.append(mx)
        if r + 1 < n:
            s = jnp.where(s == mx, -jnp.inf, s)
    return rows


def _oddeven_merge(lo, hi, r):
    step = r * 2
    if step < hi - lo:
        yield from _oddeven_merge(lo, hi, step)
        yield from _oddeven_merge(lo + r, hi, step)
        yield from [(i, i + r) for i in range(lo + r, hi - r, step)]
    else:
        yield (lo, lo + r)


def _oddeven_sort(lo, hi):
    if hi - lo >= 1:
        mid = lo + (hi - lo) // 2
        yield from _oddeven_sort(lo, mid)
        yield from _oddeven_sort(mid + 1, hi)
        yield from _oddeven_merge(lo, hi, 1)


SORT16 = list(_oddeven_sort(0, PEER_TOPK - 1))


def _colmax8(x):
    for sh in (1, 2, 4):
        x = jnp.maximum(x, pltpu.roll(x, sh, 0))
    return x


def _top16_rows(s):
    ls = [s[8 * r:8 * r + 8, :] for r in range(PEER_TOPK)]
    for i, j in SORT16:
        ls[i], ls[j] = jnp.maximum(ls[i], ls[j]), jnp.minimum(ls[i], ls[j])
    rows = []
    for r in range(PEER_TOPK):
        m = _colmax8(ls[0])
        rows.append(m[0:1, :])
        keep = PEER_TOPK - r - 1
        hit = ls[0] == m
        ls = [jnp.where(hit, ls[d + 1], ls[d]) for d in range(keep)]
    return rows


def _peer_route_kernel(x_ref, g_ref, mod_ref, wq_ref, keys_ref, xnt_ref, e1_ref, th_ref, s2_ref, e2_ref, xn_scr):
    h = pl.program_id(2)

    @pl.when(h == 0)
    def _():
        xn = _modnorm(x_ref[...], g_ref[...], mod_ref, 3, 4)
        xn_scr[...] = xn.astype(BF16)
        xnt_ref[...] = xn.T.astype(BF16)

    q = jnp.dot(xn_scr[...], wq_ref[...], preferred_element_type=F32).astype(BF16)
    half = PEER_DQ // 2
    s1 = _nt(keys_ref[0], q[:, :half])
    s2 = _nt(keys_ref[1], q[:, half:])
    top1 = _top16_rows(s1)
    top2 = _top16_rows(s2)
    pad_rows = [jnp.full_like(top1[0], -jnp.inf)] * (-len(STAIR) % 8)
    cand = jnp.concatenate([top1[a] + top2[b] for a, b in STAIR] + pad_rows, axis=0)
    thr = _top_rows(cand, PEER_TOPK)[-1]
    m1, m2 = top1[0], top2[0]
    zsum = jnp.sum(jnp.where(cand >= thr, jnp.exp(cand - (m1 + m2)), 0.0), axis=0, keepdims=True)
    e1_ref[...] = jnp.exp(s1 - m1) / zsum
    th_ref[...] = thr - s1
    s2_ref[...] = s2
    e2_ref[...] = jnp.exp(s2 - m2)


def _peer_route(x, g, mod, wq, keys):
    b, l, d = x.shape
    tm = min(256, l)
    nt = l // tm
    n = b * l
    hb = pl.BlockSpec((None, PEER_NKEYS, tm), lambda bi, i, h: (h, 0, bi * nt + i))
    return pl.pallas_call(
        _peer_route_kernel,
        grid=(b, nt, PEER_HEADS),
        in_specs=[
            pl.BlockSpec((None, tm, d), lambda bi, i, h: (bi, i, 0)),
            pl.BlockSpec((1, d), lambda bi, i, h: (0, 0)),
            pl.BlockSpec((None, 6, d), lambda bi, i, h: (bi, 0, 0)),
            pl.BlockSpec((d, PEER_DQ), lambda bi, i, h: (0, h)),
            pl.BlockSpec((None, 2, PEER_NKEYS, PEER_DQ // 2), lambda bi, i, h: (h, 0, 0, 0)),
        ],
        out_specs=[pl.BlockSpec((d, tm), lambda bi, i, h: (0, bi * nt + i)), hb, hb, hb, hb],
        out_shape=[jax.ShapeDtypeStruct((d, n), BF16)] + [jax.ShapeDtypeStruct((PEER_HEADS, PEER_NKEYS, n), F32)] * 4,
        scratch_shapes=[pltpu.VMEM((tm, d), BF16)],
        compiler_params=_cparams(("parallel", "parallel", "arbitrary")),
        name="peer_route",
    )(x, g.reshape(1, d), mod, wq, keys)


GELU_C = math.sqrt(2.0 / math.pi)
PEER_ROWS = 8
PEER_VMEM_LIMIT = 58 * 1024 * 1024
GATE_RB = 32
GATE_IG = 4


def _gelu_tanh(x):
    k = -2.0 * GELU_C * LOG2E
    return x / (1.0 + jnp.exp2(x * (k + (k * 0.044715) * (x * x))))


def _peer_pipe_kernel(*refs, final, n_pairs):
    refs = list(refs)
    x_ref, mod_ref, xnt_ref, e1_ref, th_ref, s2_ref, e2_ref, u_ref, vt_ref = refs[:9]
    pos = 9
    if final:
        fg_ref = refs[pos]
        pos += 1
    o_ref, acc_scr = refs[pos:pos + 2]
    a_scr = (refs[pos + 2:pos + 4], refs[pos + 4:pos + 6])
    wg_scr = (refs[pos + 6:pos + 8], refs[pos + 8:pos + 10])
    be1_scr = refs[pos + 10:pos + 12]
    bth_scr = refs[pos + 12:pos + 14]
    g = pl.program_id(2)
    eb = PEER_ROWS * PEER_NKEYS
    tm = s2_ref.shape[2]

    def scores(p):
        for k in range(2):
            a_scr[p][k][...] = jnp.dot(u_ref[k * eb:(k + 1) * eb, :], xnt_ref[...], preferred_element_type=F32)

    def gate_one(a_ref, wg_ref, k):
        zero8 = jnp.zeros((8, tm), F32)
        for h in range(PEER_HEADS):
            for ii in range(PEER_ROWS):
                be1_scr[k][h * PEER_ROWS + ii] = e1_ref[h, pl.ds(k * PEER_ROWS + ii, 1), :] + zero8
                bth_scr[k][h * PEER_ROWS + ii] = th_ref[h, pl.ds(k * PEER_ROWS + ii, 1), :] + zero8
        for c0 in range(0, tm, LANES):
            cs = slice(c0, c0 + LANES)
            for r0 in range(0, PEER_NKEYS, GATE_RB):
                for ig in range(0, PEER_ROWS, GATE_IG):
                    ws = [None] * GATE_IG
                    for h in range(PEER_HEADS):
                        s2 = s2_ref[h, r0:r0 + GATE_RB, cs].reshape(GATE_RB // 8, 8, LANES)
                        e2 = e2_ref[h, r0:r0 + GATE_RB, cs].reshape(GATE_RB // 8, 8, LANES)
                        for j in range(GATE_IG):
                            row = h * PEER_ROWS + ig + j
                            contrib = jnp.where(s2 >= bth_scr[k][row, :, cs][None], e2 * be1_scr[k][row, :, cs][None], 0.0)
                            ws[j] = contrib if ws[j] is None else ws[j] + contrib
                    for j in range(GATE_IG):
                        rs = slice((ig + j) * PEER_NKEYS + r0, (ig + j) * PEER_NKEYS + r0 + GATE_RB)
                        wj = ws[j].reshape(GATE_RB, LANES)
                        wg_ref[rs, cs] = (wj * _gelu_tanh(a_ref[rs, cs])).astype(BF16)

    def gates(p):
        for k in range(2):
            gate_one(a_scr[1 - p][k], wg_scr[p][k], k)

    def accums(p):
        for k in range(2):
            acc_scr[...] += jnp.dot(vt_ref[:, k * eb:(k + 1) * eb], wg_scr[1 - p][k][...], preferred_element_type=F32)

    @pl.when(g == 0)
    def _():
        acc_scr[...] = jnp.zeros_like(acc_scr)
        scores(0)

    @pl.when(g == 1)
    def _():
        scores(1)
        gates(1)

    for p in range(2):
        @pl.when((g >= 2) & (g < n_pairs) & (g % 2 == p))
        def _(p=p):
            scores(p)
            gates(p)
            accums(p)

    @pl.when(g == n_pairs)
    def _():
        gates(n_pairs % 2)
        accums(n_pairs % 2)

    @pl.when(g == n_pairs + 1)
    def _():
        accums((n_pairs + 1) % 2)
        y = x_ref[...] + mod_ref[5:6, :] * acc_scr[...].T
        if final:
            ms = jnp.mean(y * y, axis=-1, keepdims=True)
            y = y * lax.rsqrt(ms + NORM_EPS) * fg_ref[...]
        o_ref[...] = y


def _peer_pipe(x, mod, xnt, e1, th, s2, e2, u, vt, final_g):
    b, l, d = x.shape
    tm = min(512, l)
    nt = l // tm
    eb = PEER_ROWS * PEER_NKEYS
    n_pairs = u.shape[0] // (2 * eb)
    assert n_pairs >= 2
    final = final_g is not None
    tok = lambda bi, i: bi * nt + i
    clamp = lambda v: jnp.clip(v, 0, n_pairs - 1)
    rows = pl.BlockSpec((PEER_HEADS, 2 * PEER_ROWS, tm), lambda bi, i, g: (0, clamp(g - 1), tok(bi, i)))
    keyblk = pl.BlockSpec((PEER_HEADS, PEER_NKEYS, tm), lambda bi, i, g: (0, 0, tok(bi, i)))
    in_specs = [
        pl.BlockSpec((None, tm, d), lambda bi, i, g: (bi, i, 0)),
        pl.BlockSpec((None, 6, d), lambda bi, i, g: (bi, 0, 0)),
        pl.BlockSpec((d, tm), lambda bi, i, g: (0, tok(bi, i))),
        rows, rows, keyblk, keyblk,
        pl.BlockSpec((2 * eb, d), lambda bi, i, g: (clamp(g), 0)),
        pl.BlockSpec((d, 2 * eb), lambda bi, i, g: (0, clamp(g - 2))),
    ]
    args = [x, mod, xnt, e1, th, s2, e2, u, vt]
    if final:
        in_specs.append(pl.BlockSpec((1, d), lambda bi, i, g: (0, 0)))
        args.append(final_g.reshape(1, d))
    bcast = pltpu.VMEM((PEER_HEADS * PEER_ROWS, 8, tm), F32)
    return pl.pallas_call(
        functools.partial(_peer_pipe_kernel, final=final, n_pairs=n_pairs),
        grid=(b, nt, n_pairs + 2),
        in_specs=in_specs,
        out_specs=pl.BlockSpec((None, tm, d), lambda bi, i, g: (bi, i, 0)),
        out_shape=jax.ShapeDtypeStruct((b, l, d), F32),
        scratch_shapes=([pltpu.VMEM((d, tm), F32)] + [pltpu.VMEM((eb, tm), F32)] * 4 + [pltpu.VMEM((eb, tm), BF16)] * 4
                        + [bcast] * 4),
        compiler_params=_cparams(("parallel", "parallel", "arbitrary"), PEER_VMEM_LIMIT),
        name="peer_experts",
    )(*args)


def _pad_cols(a, width):
    return jnp.pad(a, ((0, 0), (0, width - a.shape[1])))


def _layout_w_in(w):
    d = w.shape[0]
    z = lambda n: jnp.zeros((d, n), w.dtype)
    gla, hg, mla, dif = 0, 800, 2080, 2432
    parts = [w[:, gla:gla + 768], w[:, hg:hg + 768],
             w[:, mla:mla + 192], z(64), w[:, mla + 192:mla + 320], z(64), w[:, mla + 320:mla + 352], z(32)]
    for base in (dif, dif + 256, dif + 512):
        for h in range(DIFF_HEADS):
            parts += [w[:, base + 64 * h:base + 64 * (h + 1)], z(64)]
    parts += [w[:, gla + 768:gla + 800], z(224), w[:, hg + 768:hg + 1280]]
    out = jnp.concatenate(parts, axis=1)
    assert out.shape[1] == MAIN_COLS + GATE_COLS
    return out.astype(BF16)


def _layout_mla(w_uq, w_ukv):
    dq = MLA_NOPE + MLA_ROPE
    wq = jnp.concatenate([_pad_cols(w_uq[:, dq * h:dq * (h + 1)], LANES) for h in range(MLA_HEADS)], axis=1)
    wq = jnp.pad(wq, ((0, 256 - MLA_Q_RANK), (0, 0)))
    per = MLA_NOPE + MLA_DV
    wk = jnp.concatenate([_pad_cols(w_ukv[:, per * h:per * h + MLA_NOPE], LANES) for h in range(MLA_HEADS)], axis=1)
    wv = jnp.concatenate([_pad_cols(w_ukv[:, per * h + MLA_NOPE:per * (h + 1)], LANES) for h in range(MLA_HEADS)], axis=1)
    return wq.astype(BF16), jnp.concatenate([wk, wv], axis=1).astype(BF16)


def kernel(x, c, ctx, c_ctx, ada_w, ada_b, norm_mix_g, norm_ffn_g, w_in, w_out, gla_gate_w, gla_gate_b, gla_norm_g, hgrn_lb_raw, hgrn_norm_g, mla_q_norm_g, mla_kv_norm_g, mla_w_uq, mla_w_ukv, diff_lambda, diff_norm_g, peer_wq, peer_keys, peer_u, peer_v, final_norm_g):
    b, l, d = x.shape
    lc = ctx.shape[1]
    depth = ada_w.shape[0]
    assert lc <= ATT_TK, "context keys are consumed as one attention step"

    rows = -(-(b + 1) // 8) * 8
    cpad = jnp.zeros((rows, d), F32).at[:b].set(c).at[b].set(c_ctx)
    mod_all = _adaln(cpad, ada_w, ada_b)

    lb_sm = jax.nn.softmax(hgrn_lb_raw.astype(F32), axis=0)
    lb_all = jnp.cumsum(lb_sm, axis=0) - lb_sm[0]

    lat_mla_tab = _rope_tables(l, 64, 32, True)
    ctx_mla_tab = _rope_tables(lc, 64, 32, False)
    lat_dif_tab = _rope_tables(l, 0, 64, True)
    ctx_dif_tab = _rope_tables(lc, 0, 64, False)

    xc = ctx
    for li in range(depth):
        need_ctx = li < depth - 1
        lam_init = 0.8 - 0.6 * math.exp(-0.3 * li)
        mod_l = mod_all[li, :b].reshape(b, 6, d)
        mod_c = jnp.broadcast_to(mod_all[li, b].reshape(1, 6, d), (b, 6, d))

        w_perm = _layout_w_in(w_in[li])
        w_o = w_out[li].astype(BF16)
        wg = jnp.zeros((2, 256, 128), F32)
        wg = wg.at[0, 0:GLA_GATE_RANK].set(gla_gate_w[li, 0]).at[1, GLA_GATE_RANK:2 * GLA_GATE_RANK].set(gla_gate_w[li, 1])
        wg = wg.astype(BF16)
        bg = gla_gate_b[li].reshape(2, 1, 128)
        gla_gn = jnp.tile(gla_norm_g[li], GLA_HEADS).reshape(1, 256)
        hg_gn = jnp.tile(hgrn_norm_g[li], HG_HEADS).reshape(1, 256)
        lb = lb_all[li]
        hg_p = jnp.zeros((8, 256), F32).at[0].set(jnp.log(jnp.maximum(lb, LB_FLOOR))).at[1].set(jnp.log1p(-lb)).at[2].set(1.0 - lb)
        hg_dummy = jnp.zeros((8, 128), F32)
        gq = jnp.pad(mla_q_norm_g[li], (0, 256 - MLA_Q_RANK)).reshape(1, 256)
        gkv = mla_kv_norm_g[li].reshape(1, MLA_KV_RANK)
        wq_mla, wkv_mla = _layout_mla(mla_w_uq[li], mla_w_ukv[li])
        lq1, lk1, lq2, lk2 = diff_lambda[li, 0], diff_lambda[li, 1], diff_lambda[li, 2], diff_lambda[li, 3]
        lam = (jnp.exp(jnp.sum(lq1 * lk1).astype(F32)) - jnp.exp(jnp.sum(lq2 * lk2).astype(F32)) + lam_init)
        lam_row = jnp.full((1, LANES), lam, F32)
        dif_gn = jnp.tile(diff_norm_g[li], 2).reshape(1, LANES)
        wq_peer = peer_wq[li].astype(BF16)
        keys = peer_keys[li].astype(BF16)
        u_bf = peer_u[li].astype(BF16)
        vt_bf = peer_v[li].astype(BF16).T

        main_c, gate_c = _inproj(xc, norm_mix_g[li], mod_c, w_perm)
        main_l, gate_l = _inproj(x, norm_mix_g[li], mod_l, w_perm)

        mixes_c, mixes_l = [], []
        for mixer, pa_f, pb_f, pa_b, pb_b, gn, dk in (
            ("gla", wg[0], bg[0], wg[1], bg[1], gla_gn, 128),
            ("hg", hg_p, hg_dummy, hg_p, hg_dummy, hg_gn, 256),
        ):
            zero = jnp.zeros((b, 256, dk), F32)
            o_cf, s_f = _scan(main_c, gate_c, pa_f, pb_f, gn, zero, None, mixer=mixer, reverse=False, chunk=64)
            mix_c, s_b = _scan(main_c, gate_c, pa_b, pb_b, gn, zero, o_cf, mixer=mixer, reverse=True, chunk=64)
            o_lf, _ = _scan(main_l, gate_l, pa_f, pb_f, gn, s_f, None, mixer=mixer, reverse=False, chunk=64)
            mix_l, _ = _scan(main_l, gate_l, pa_b, pb_b, gn, s_b, o_lf, mixer=mixer, reverse=True, chunk=64)
            mixes_c.append(mix_c)
            mixes_l.append(mix_l)

        qm_c, km_c, vm_c = _mla_prep(main_c, gq, gkv, wq_mla, wkv_mla, ctx_mla_tab)
        qm_l, km_l, vm_l = _mla_prep(main_l, gq, gkv, wq_mla, wkv_mla, lat_mla_tab)
        mla_l = _attention(qm_l, km_c, vm_c, km_l, vm_l, None, n_maps=1, lam_init=lam_init)
        qd_c, kd_c, vd_c = _diff_prep(main_c, ctx_dif_tab)
        qd_l, kd_l, vd_l = _diff_prep(main_l, lat_dif_tab)
        dif_l = _attention(qd_l, kd_c, vd_c, kd_l, vd_l, (lam_row, dif_gn), n_maps=2, lam_init=lam_init)

        x = _outproj(x, mixes_l[0], mixes_l[1], mla_l, dif_l, w_o, mod_l)
        rt = _peer_route(x, norm_ffn_g[li], mod_l, wq_peer, keys)
        x = _peer_pipe(x, mod_l, *rt, u_bf, vt_bf, None if need_ctx else final_norm_g)

        if need_ctx:
            mla_c = _attention(qm_c, km_c, vm_c, None, None, None, n_maps=1, lam_init=lam_init)
            dif_c = _attention(qd_c, kd_c, vd_c, None, None, (lam_row, dif_gn), n_maps=2, lam_init=lam_init)
            xc = _outproj(xc, mixes_c[0], mixes_c[1], mla_c, dif_c, w_o, mod_c)
            rtc = _peer_route(xc, norm_ffn_g[li], mod_c, wq_peer, keys)
            xc = _peer_pipe(xc, mod_c, *rtc, u_bf, vt_bf, None)
    return x
```

```python
import functools
import math

import jax
import jax.numpy as jnp
import numpy as np
from jax import lax
from jax.experimental import pallas as pl
from jax.experimental.pallas import tpu as pltpu

F32 = jnp.float32
BF16 = jnp.bfloat16

NORM_EPS = 1e-6
ROPE_BASE = 10000.0
GRID_W = 64
LB_FLOOR = 1e-30

GLA_HEADS, GLA_DK, GLA_DV, GLA_GATE_RANK, GLA_GATE_NORM = 4, 32, 64, 16, 16.0
HG_HEADS, HG_DIM = 4, 64
MLA_HEADS, MLA_Q_RANK, MLA_KV_RANK, MLA_NOPE, MLA_ROPE, MLA_DV = 4, 192, 128, 64, 32, 64
DIFF_HEADS, DIFF_DQK, DIFF_DV = 4, 32, 64
PEER_HEADS, PEER_NKEYS, PEER_DQ, PEER_TOPK = 8, 128, 256, 16

LANES = 128
VMEM_LIMIT = 48 * 1024 * 1024
EXP_CAP = 80.0
NEG_BIG = -1e30
ATT_TK = 512
LOG2E = 1.4426950408889634
ATT_R = 512
ATT_STEPS_PER_TRIP = 2
VT_ROWS = 80

COL_GLA, COL_HG, COL_MLA, COL_DQ, COL_DK, COL_DV, MAIN_COLS = 0, 768, 1536, 2048, 2560, 3072, 3584
GATE_COLS = 768

STAIR = [(a, b) for a in range(PEER_TOPK) for b in range(PEER_TOPK) if (a + 1) * (b + 1) <= PEER_TOPK]


def _cparams(sem, vmem=VMEM_LIMIT):
    return pltpu.CompilerParams(dimension_semantics=sem, vmem_limit_bytes=vmem)


def _sigmoid(x):
    return 1.0 / (1.0 + jnp.exp(-x))


def _logsig(x):
    return jnp.minimum(x, 0.0) - jnp.log(1.0 + jnp.exp(-jnp.abs(x)))


def _nt(a, b):
    return lax.dot_general(a, b, (((1,), (1,)), ((), ())), preferred_element_type=F32)


def _tn(a, b):
    return lax.dot_general(a, b, (((0,), (0,)), ((), ())), preferred_element_type=F32)


def _adaln_kernel(c_ref, w_ref, b_ref, o_ref):
    c = c_ref[...]
    s = c * _sigmoid(c)
    o_ref[...] = jnp.dot(s.astype(BF16), w_ref[...].astype(BF16), preferred_element_type=F32) + b_ref[...]


def _adaln(cpad, ada_w, ada_b):
    depth, d, n6 = ada_w.shape
    rows = cpad.shape[0]
    tn = 512
    return pl.pallas_call(
        _adaln_kernel,
        grid=(depth, n6 // tn),
        in_specs=[
            pl.BlockSpec((rows, d), lambda l, j: (0, 0)),
            pl.BlockSpec((None, d, tn), lambda l, j: (l, 0, j)),
            pl.BlockSpec((None, 1, tn), lambda l, j: (l, 0, j)),
        ],
        out_specs=pl.BlockSpec((None, rows, tn), lambda l, j: (l, 0, j)),
        out_shape=jax.ShapeDtypeStruct((depth, rows, n6), F32),
        compiler_params=_cparams(("parallel", "parallel")),
        name="adaln",
    )(cpad, ada_w, ada_b.reshape(depth, 1, n6))


def _modnorm(x, g, mod_ref, shift_row, scale_row):
    ms = jnp.mean(x * x, axis=-1, keepdims=True)
    h = x * lax.rsqrt(ms + NORM_EPS) * g
    return h * (1.0 + mod_ref[scale_row:scale_row + 1, :]) + mod_ref[shift_row:shift_row + 1, :]


def _inproj_kernel(x_ref, g_ref, mod_ref, w_ref, main_ref, gate_ref):
    h = _modnorm(x_ref[...], g_ref[...], mod_ref, 0, 1)
    y = jnp.dot(h.astype(BF16), w_ref[...], preferred_element_type=F32)
    main_ref[...] = y[:, :MAIN_COLS].astype(BF16)
    gate_ref[...] = y[:, MAIN_COLS:]


def _inproj(x, g, mod, w_perm):
    b, l, d = x.shape
    tm = min(256, l)
    nc = w_perm.shape[1]
    return pl.pallas_call(
        _inproj_kernel,
        grid=(b, l // tm),
        in_specs=[
            pl.BlockSpec((None, tm, d), lambda bi, i: (bi, i, 0)),
            pl.BlockSpec((1, d), lambda bi, i: (0, 0)),
            pl.BlockSpec((None, 6, d), lambda bi, i: (bi, 0, 0)),
            pl.BlockSpec((d, nc), lambda bi, i: (0, 0)),
        ],
        out_specs=[
            pl.BlockSpec((None, tm, MAIN_COLS), lambda bi, i: (bi, i, 0)),
            pl.BlockSpec((None, tm, GATE_COLS), lambda bi, i: (bi, i, 0)),
        ],
        out_shape=[
            jax.ShapeDtypeStruct((b, l, MAIN_COLS), BF16),
            jax.ShapeDtypeStruct((b, l, GATE_COLS), F32),
        ],
        compiler_params=_cparams(("parallel", "parallel")),
        name="inproj",
    )(x, g.reshape(1, d), mod, w_perm)


def _scan_kernel(*refs, mixer, reverse, finish, chunk):
    if finish:
        main_ref, gate_ref, pa_ref, pb_ref, gn_ref, s0_ref, oprev_ref, o_ref, sfin_ref, st_scr = refs
    else:
        main_ref, gate_ref, pa_ref, pb_ref, gn_ref, s0_ref, o_ref, sfin_ref, st_scr = refs
        oprev_ref = None
    i = pl.program_id(1)

    @pl.when(i == 0)
    def _():
        st_scr[...] = s0_ref[...]

    main = main_ref[...]
    t = main.shape[0]
    c = chunk
    nc = t // c
    heads = 4
    dv = 256
    v = main[:, 256:512]
    gcol = main[:, 512:768].astype(F32)
    if mixer == "gla":
        dk = 128
        q = main[:, 0:128].astype(F32) * (GLA_DK ** -0.5)
        k = main[:, 128:256].astype(F32)
        z = jnp.dot(gate_ref[...].astype(BF16), pa_ref[...], preferred_element_type=F32) + pb_ref[...]
        g = _logsig(z) * (1.0 / GLA_GATE_NORM)
    else:
        dk = 256
        qq = main[:, 0:256].astype(F32)
        q = qq * _sigmoid(qq)
        z = gate_ref[...]
        la = pa_ref[0:1, :]
        lb = pa_ref[1:2, :] + _logsig(z)
        g = jnp.maximum(la, lb) + jnp.log(1.0 + jnp.exp(-jnp.abs(la - lb)))
        k = pa_ref[2:3, :] * _sigmoid(-z)
    hk = dk // heads
    hv = dv // heads

    rt = lax.broadcasted_iota(jnp.int32, (t, t), 0)
    ct = lax.broadcasted_iota(jnp.int32, (t, t), 1)
    same = (rt // c) == (ct // c)
    tri = jnp.where(same & ((ct >= rt) if reverse else (ct <= rt)), 1.0, 0.0).astype(BF16)
    g1 = g.astype(BF16)
    r1 = g - g1.astype(F32)
    g2 = r1.astype(BF16)
    g3 = (r1 - g2.astype(F32)).astype(BF16)
    bcum = (jnp.dot(tri, g1, preferred_element_type=F32) + jnp.dot(tri, g2, preferred_element_type=F32)
            + jnp.dot(tri, g3, preferred_element_type=F32))

    b3 = bcum.reshape(nc, c, dk)
    q3 = q.reshape(nc, c, dk)
    k3 = k.reshape(nc, c, dk)
    if reverse:
        bmid = b3[:, c // 2:c // 2 + 1, :]
        bend = b3[:, 0:1, :]
    else:
        bmid = b3[:, c // 2 - 1:c // 2, :]
        bend = b3[:, c - 1:c, :]
    eb = b3 - bmid
    qa = (q3 * jnp.exp(jnp.minimum(eb, EXP_CAP))).reshape(t, dk).astype(BF16)
    ka = (k3 * jnp.exp(jnp.minimum(-eb, EXP_CAP))).reshape(t, dk).astype(BF16)
    qi = (q3 * jnp.exp(b3)).reshape(t, dk).astype(BF16)
    kb = (k3 * jnp.exp(bend - b3)).reshape(t, dk).astype(BF16)
    dec = jnp.exp(bend)

    lane_k = lax.broadcasted_iota(jnp.int32, (1, dk), 1) // hk
    lane_v = lax.broadcasted_iota(jnp.int32, (1, dv), 1) // hv
    rr = lax.broadcasted_iota(jnp.int32, (c, heads * c), 0)
    cc = lax.broadcasted_iota(jnp.int32, (c, heads * c), 1) % c
    causal = (cc >= rr) if reverse else (cc <= rr)
    bd = (lax.broadcasted_iota(jnp.int32, (dv, dk), 0) // hv) == (lax.broadcasted_iota(jnp.int32, (dv, dk), 1) // hk)

    o_intra = []
    upd = []
    for ci in range(nc):
        sl = slice(ci * c, (ci + 1) * c)
        ka_c = ka[sl]
        v_c = v[sl]
        kst = jnp.concatenate([jnp.where(lane_k == h, ka_c, 0) for h in range(heads)], axis=0)
        vst = jnp.concatenate([jnp.where(lane_v == h, v_c, 0) for h in range(heads)], axis=0)
        sw = _nt(qa[sl], kst)
        p = jnp.where(causal, sw, 0.0).astype(BF16)
        o_intra.append(jnp.dot(p, vst, preferred_element_type=F32))
        upd.append(jnp.where(bd, _tn(v_c, kb[sl]), 0.0))

    outs = [None] * nc
    order = range(nc - 1, -1, -1) if reverse else range(nc)
    for ci in order:
        sl = slice(ci * c, (ci + 1) * c)
        st = st_scr[...]
        outs[ci] = o_intra[ci] + _nt(qi[sl], st.astype(BF16))
        st_scr[...] = st * dec[ci] + upd[ci]
    o = jnp.concatenate(outs, axis=0)

    @pl.when(i == pl.num_programs(1) - 1)
    def _():
        sfin_ref[...] = st_scr[...]

    if not finish:
        o_ref[...] = o
    else:
        ot = o + oprev_ref[...]
        sq = ot * ot
        jv = ((lax.broadcasted_iota(jnp.int32, (dv, dv), 0) // hv)
              == (lax.broadcasted_iota(jnp.int32, (dv, dv), 1) // hv))
        jm = jnp.where(jv, 1.0, 0.0).astype(BF16)
        s1 = sq.astype(BF16)
        s2 = (sq - s1.astype(F32)).astype(BF16)
        ms = (jnp.dot(s1, jm, preferred_element_type=F32) + jnp.dot(s2, jm, preferred_element_type=F32)) * (1.0 / hv)
        y = ot * lax.rsqrt(ms + NORM_EPS) * gn_ref[...]
        o_ref[...] = (y * (gcol * _sigmoid(gcol))).astype(BF16)


def _scan(main, gate, pa, pb, gn, s0, oprev, *, mixer, reverse, chunk):
    b, l, _ = main.shape
    t = min(256, l)
    nb = l // t
    finish = oprev is not None
    dk = 128 if mixer == "gla" else 256
    dv = 256
    main_blk = 0 if mixer == "gla" else 1
    gate_blk = 0 if mixer == "gla" else (2 if reverse else 1)

    def tok(bi, i):
        return (nb - 1 - i) if reverse else i

    in_specs = [
        pl.BlockSpec((None, t, 768), lambda bi, i: (bi, tok(bi, i), main_blk)),
        pl.BlockSpec((None, t, 256), lambda bi, i: (bi, tok(bi, i), gate_blk)),
        pl.BlockSpec(pa.shape, lambda bi, i: (0, 0)),
        pl.BlockSpec(pb.shape, lambda bi, i: (0, 0)),
        pl.BlockSpec((1, dv), lambda bi, i: (0, 0)),
        pl.BlockSpec((None, dv, dk), lambda bi, i: (bi, 0, 0)),
    ]
    args = [main, gate, pa, pb, gn, s0]
    if finish:
        in_specs.append(pl.BlockSpec((None, t, dv), lambda bi, i: (bi, tok(bi, i), 0)))
        args.append(oprev)
    return pl.pallas_call(
        functools.partial(_scan_kernel, mixer=mixer, reverse=reverse, finish=finish, chunk=chunk),
        grid=(b, nb),
        in_specs=in_specs,
        out_specs=[
            pl.BlockSpec((None, t, dv), lambda bi, i: (bi, tok(bi, i), 0)),
            pl.BlockSpec((None, dv, dk), lambda bi, i: (bi, 0, 0)),
        ],
        out_shape=[
            jax.ShapeDtypeStruct((b, l, dv), BF16 if finish else F32),
            jax.ShapeDtypeStruct((b, dv, dk), F32),
        ],
        scratch_shapes=[pltpu.VMEM((dv, dk), F32)],
        compiler_params=_cparams(("parallel", "arbitrary")),
        name=f"scan_{mixer}_{'bwd' if reverse else 'fwd'}",
    )(*args)


def _rope(x, cos, sin_lo, sin_hi):
    return x * cos + pltpu.roll(x, LANES - 8, 1) * sin_lo + pltpu.roll(x, 8, 1) * sin_hi


def _rope_tables(l, first_lane, n_lanes, rotate):
    lane = np.arange(LANES)
    d = (lane - first_lane) % 32
    active = (lane >= first_lane) & (lane < first_lane + n_lanes)
    freqs = ROPE_BASE ** (-(d % 8).astype(np.float32) / 8.0)
    tt = jnp.arange(l, dtype=jnp.int32)
    rows = (tt // GRID_W).astype(F32)[:, None]
    cols = (tt % GRID_W).astype(F32)[:, None]
    pos = jnp.where(jnp.asarray(d < 16)[None, :], rows, cols)
    ang = pos * jnp.asarray(freqs, F32)[None, :]
    act = jnp.asarray(active)[None, :] & rotate
    lo = jnp.asarray((d % 16) < 8)[None, :]
    cos = jnp.where(act, jnp.cos(ang), 1.0)
    sin = jnp.where(act, jnp.sin(ang), 0.0)
    return cos, jnp.where(lo, -sin, 0.0), jnp.where(lo, 0.0, sin)


def _vt_block(vgrp):
    lane = lax.broadcasted_iota(jnp.int32, (1, LANES), 1)
    return jnp.where(lane == 64, 1.0, vgrp).T[:VT_ROWS].astype(BF16)


def _mla_prep_kernel(slab_ref, gq_ref, gkv_ref, wq_ref, wkv_ref, cos_ref, slo_ref, shi_ref, qt_ref, k_ref, vt_ref):
    slab = slab_ref[...]
    cos, slo, shi = cos_ref[...], slo_ref[...], shi_ref[...]
    cq = slab[:, 0:256].astype(F32)
    qn = cq * lax.rsqrt(jnp.sum(cq * cq, axis=-1, keepdims=True) * (1.0 / MLA_Q_RANK) + NORM_EPS) * gq_ref[...]
    qall = jnp.dot(qn.astype(BF16), wq_ref[...], preferred_element_type=F32)
    ckv = slab[:, 256:384].astype(F32)
    kvn = ckv * lax.rsqrt(jnp.mean(ckv * ckv, axis=-1, keepdims=True) + NORM_EPS) * gkv_ref[...]
    kvall = jnp.dot(kvn.astype(BF16), wkv_ref[...], preferred_element_type=F32)
    kr = _rope(slab[:, 384:512].astype(F32), cos, slo, shi)
    scale = (MLA_NOPE + MLA_ROPE) ** -0.5 * LOG2E
    for h in range(MLA_HEADS):
        qh = _rope(qall[:, h * LANES:(h + 1) * LANES], cos, slo, shi)
        qt_ref[h] = (qh * scale).T.astype(BF16)
        k_ref[h] = (kvall[:, h * LANES:(h + 1) * LANES] + kr).astype(BF16)
        vt_ref[h] = _vt_block(kvall[:, (MLA_HEADS + h) * LANES:(MLA_HEADS + h + 1) * LANES])


def _attn_prep_specs(b, l, tm, heads):
    qt = pl.BlockSpec((None, heads, LANES, tm), lambda bi, i: (bi, 0, 0, i))
    kk = pl.BlockSpec((None, heads, tm, LANES), lambda bi, i: (bi, 0, i, 0))
    vt = pl.BlockSpec((None, heads, None, VT_ROWS, tm), lambda bi, i: (bi, 0, i, 0, 0))
    shapes = [
        jax.ShapeDtypeStruct((b, heads, LANES, l), BF16),
        jax.ShapeDtypeStruct((b, heads, l, LANES), BF16),
        jax.ShapeDtypeStruct((b, heads, l // tm, VT_ROWS, tm), BF16),
    ]
    return [qt, kk, vt], shapes


def _mla_prep(main, gq, gkv, wq, wkv, tables):
    b, l, _ = main.shape
    tm = min(ATT_TK, l)
    cos, slo, shi = tables
    tab = pl.BlockSpec((tm, LANES), lambda bi, i: (i, 0))
    out_specs, out_shape = _attn_prep_specs(b, l, tm, MLA_HEADS)
    return pl.pallas_call(
        _mla_prep_kernel,
        grid=(b, l // tm),
        in_specs=[
            pl.BlockSpec((None, tm, 512), lambda bi, i: (bi, i, COL_MLA // 512)),
            pl.BlockSpec(gq.shape, lambda bi, i: (0, 0)),
            pl.BlockSpec(gkv.shape, lambda bi, i: (0, 0)),
            pl.BlockSpec(wq.shape, lambda bi, i: (0, 0)),
            pl.BlockSpec(wkv.shape, lambda bi, i: (0, 0)),
            tab, tab, tab,
        ],
        out_specs=out_specs,
        out_shape=out_shape,
        compiler_params=_cparams(("parallel", "parallel")),
        name="mla_prep",
    )(main, gq, gkv, wq, wkv, cos, slo, shi)


def _diff_prep_kernel(q_in, k_in, v_in, cos_ref, slo_ref, shi_ref, qt_ref, k_ref, vt_ref):
    cos, slo, shi = cos_ref[...], slo_ref[...], shi_ref[...]
    qs = q_in[...]
    ks = k_in[...]
    vs = v_in[...]
    scale = DIFF_DQK ** -0.5 * LOG2E
    for h in range(DIFF_HEADS):
        sl = slice(h * LANES, (h + 1) * LANES)
        qt_ref[h] = (_rope(qs[:, sl].astype(F32), cos, slo, shi) * scale).T.astype(BF16)
        k_ref[h] = _rope(ks[:, sl].astype(F32), cos, slo, shi).astype(BF16)
        vt_ref[h] = _vt_block(vs[:, sl].astype(F32))


def _diff_prep(main, tables):
    b, l, _ = main.shape
    tm = min(ATT_TK, l)
    cos, slo, shi = tables
    tab = pl.BlockSpec((tm, LANES), lambda bi, i: (i, 0))
    out_specs, out_shape = _attn_prep_specs(b, l, tm, DIFF_HEADS)
    return pl.pallas_call(
        _diff_prep_kernel,
        grid=(b, l // tm),
        in_specs=[
            pl.BlockSpec((None, tm, 512), lambda bi, i: (bi, i, COL_DQ // 512)),
            pl.BlockSpec((None, tm, 512), lambda bi, i: (bi, i, COL_DK // 512)),
            pl.BlockSpec((None, tm, 512), lambda bi, i: (bi, i, COL_DV // 512)),
            tab, tab, tab,
        ],
        out_specs=out_specs,
        out_shape=out_shape,
        compiler_params=_cparams(("parallel", "parallel")),
        name="diff_prep",
    )(main, main, main, cos, slo, shi)


def _attn_kernel(*refs, n_maps, has_lat, lam_init):
    refs = list(refs)
    qt_ref, kc_ref, vtc_ref = refs[:3]
    pos = 3
    if has_lat:
        kl_ref, vtl_ref = refs[pos:pos + 2]
        pos += 2
    if n_maps == 2:
        lam_ref, gn_ref = refs[pos:pos + 2]
        pos += 2
    o_ref = refs[pos]
    if has_lat:
        m_scr, acc_scr, s_scr, cm_scr, al_scr, p_scr = refs[pos + 1:pos + 7]
    tq = qt_ref.shape[2]
    feat = lax.broadcasted_iota(jnp.int32, (LANES, 1), 0)

    qts = []
    for hh in range(2):
        qt = qt_ref[hh]
        if n_maps == 2:
            zero = jnp.zeros_like(qt)
            qt = jnp.concatenate(
                [jnp.where((feat >= DIFF_DQK * mi) & (feat < DIFF_DQK * (mi + 1)), qt, zero) for mi in range(2)], axis=1)
        qts.append(qt)

    accs = []
    for hh in range(2):
        st = jnp.dot(kc_ref[hh], qts[hh], preferred_element_type=F32)
        m0 = jnp.max(st, axis=0, keepdims=True)
        acc0 = jnp.dot(vtc_ref[hh, 0], jnp.exp2(st - m0).astype(BF16), preferred_element_type=F32)
        accs.append(acc0)
        if has_lat:
            m_scr[hh] = m0
            acc_scr[hh] = acc0

    if has_lat:
        n = vtl_ref.shape[1]
        tk = vtl_ref.shape[3]

        def scores(j, slot):
            off = pl.multiple_of(j * tk, tk)
            for hh in range(2):
                st = jnp.dot(kl_ref[hh, pl.ds(off, tk), :], qts[hh], preferred_element_type=F32)
                s_scr[hh, slot] = st
                cm_scr[hh, slot] = jnp.max(st, axis=0, keepdims=True)

        def numer(slot):
            for hh in range(2):
                m_old = m_scr[hh]
                m_new = jnp.maximum(m_old, cm_scr[hh, slot])
                al_scr[hh, slot] = jnp.exp2(m_old - m_new)
                p_scr[hh, slot] = jnp.exp2(s_scr[hh, slot] - m_new).astype(BF16)
                m_scr[hh] = m_new

        def values(j, slot):
            for hh in range(2):
                acc_scr[hh] = (al_scr[hh, slot] * acc_scr[hh]
                               + jnp.dot(vtl_ref[hh, j], p_scr[hh, slot], preferred_element_type=F32))

        first = n % 2
        if first:
            scores(0, 0)
            numer(0)
            values(0, 0)
        if n > first:
            for hh in range(2):
                p_scr[hh, 1] = jnp.zeros(p_scr.shape[2:], BF16)
                al_scr[hh, 1] = jnp.ones(al_scr.shape[2:], F32)
            scores(first, 0)

            per_trip = ATT_STEPS_PER_TRIP if (n - first) % ATT_STEPS_PER_TRIP == 0 else 2

            def body(t, carry):
                for u in range(per_trip):
                    c = first + per_trip * t + u
                    slot = u % 2
                    scores(jnp.minimum(c + 1, n - 1), 1 - slot)
                    numer(slot)
                    values(jnp.maximum(c - 1, first), 1 - slot)
                return carry

            lax.fori_loop(0, (n - first) // per_trip, body, 0)
            values(n - 1, 1)
        accs = [acc_scr[0], acc_scr[1]]

    outs = []
    for hh in range(2):
        acc = accs[hh]
        ot = acc[0:64, :] / acc[64:65, :]
        if n_maps == 2:
            oh = ot[:, :tq] - lam_ref[:, 0:1] * ot[:, tq:]
            ms = jnp.mean(oh * oh, axis=0, keepdims=True)
            ot = oh * lax.rsqrt(ms + NORM_EPS)
        outs.append(ot)
    o = jnp.concatenate(outs, axis=0).T
    if n_maps == 2:
        o = o * gn_ref[...] * (1.0 - lam_init)
    o_ref[...] = o.astype(BF16)


def _attention(qt, kc, vtc, kl, vtl, extra, *, n_maps, lam_init):
    b, nh, _, l = qt.shape
    lc = kc.shape[2]
    tq = min(ATT_R // n_maps, l)
    r = n_maps * tq
    has_lat = kl is not None
    scratch = []
    in_specs = [
        pl.BlockSpec((None, 2, LANES, tq), lambda bi, hp, i: (bi, hp, 0, i)),
        pl.BlockSpec((None, 2, lc, LANES), lambda bi, hp, i: (bi, hp, 0, 0)),
        pl.BlockSpec((None, 2) + vtc.shape[2:], lambda bi, hp, i: (bi, hp, 0, 0, 0)),
    ]
    args = [qt, kc, vtc]
    if has_lat:
        in_specs += [
            pl.BlockSpec((None, 2, kl.shape[2], LANES), lambda bi, hp, i: (bi, hp, 0, 0)),
            pl.BlockSpec((None, 2) + vtl.shape[2:], lambda bi, hp, i: (bi, hp, 0, 0, 0)),
        ]
        args += [kl, vtl]
        tk = vtl.shape[4]
        scratch = [
            pltpu.VMEM((2, 1, r), F32), pltpu.VMEM((2, VT_ROWS, r), F32),
            pltpu.VMEM((2, 2, tk, r), F32), pltpu.VMEM((2, 2, 1, r), F32), pltpu.VMEM((2, 2, 1, r), F32),
            pltpu.VMEM((2, 2, tk, r), BF16),
        ]
    if n_maps == 2:
        in_specs += [pl.BlockSpec((1, LANES), lambda bi, hp, i: (0, 0))] * 2
        args += list(extra)
    return pl.pallas_call(
        functools.partial(_attn_kernel, n_maps=n_maps, has_lat=has_lat, lam_init=lam_init),
        grid=(b, nh // 2, l // tq),
        in_specs=in_specs,
        out_specs=pl.BlockSpec((None, tq, LANES), lambda bi, hp, i: (bi, i, hp)),
        out_shape=jax.ShapeDtypeStruct((b, l, nh * 64), BF16),
        scratch_shapes=scratch,
        compiler_params=_cparams(("parallel", "parallel", "arbitrary")),
        name="attn_mla" if n_maps == 1 else "attn_diff",
    )(*args)


def _outproj_kernel(x_ref, a_ref, b_ref, m_ref, d_ref, w_ref, mod_ref, o_ref):
    mix = jnp.concatenate([a_ref[...], b_ref[...], m_ref[...], d_ref[...]], axis=-1)
    y = jnp.dot(mix, w_ref[...], preferred_element_type=F32)
    o_ref[...] = x_ref[...] + mod_ref[2:3, :] * y


def _outproj(x, a, bb, m, dd, w, mod):
    b, l, d = x.shape
    tm = min(512, l)
    slab = pl.BlockSpec((None, tm, 256), lambda bi, i: (bi, i, 0))
    return pl.pallas_call(
        _outproj_kernel,
        grid=(b, l // tm),
        in_specs=[
            pl.BlockSpec((None, tm, d), lambda bi, i: (bi, i, 0)),
            slab, slab, slab, slab,
            pl.BlockSpec(w.shape, lambda bi, i: (0, 0)),
            pl.BlockSpec((None, 6, d), lambda bi, i: (bi, 0, 0)),
        ],
        out_specs=pl.BlockSpec((None, tm, d), lambda bi, i: (bi, i, 0)),
        out_shape=jax.ShapeDtypeStruct((b, l, d), F32),
        compiler_params=_cparams(("parallel", "parallel")),
        name="outproj",
    )(x, a, bb, m, dd, w, mod)


def _top_rows(s, n):
    rows = []
    for r in range(n):
        mx = jnp.max(s, axis=0, keepdims=True)
        rows.append(mx)
        if r + 1 < n:
            s = jnp.where(s == mx, -jnp.inf, s)
    return rows


def _oddeven_merge(lo, hi, r):
    step = r * 2
    if step < hi - lo:
        yield from _oddeven_merge(lo, hi, step)
        yield from _oddeven_merge(lo + r, hi, step)
        yield from [(i, i + r) for i in range(lo + r, hi - r, step)]
    else:
        yield (lo, lo + r)


def _oddeven_sort(lo, hi):
    if hi - lo >= 1:
        mid = lo + (hi - lo) // 2
        yield from _oddeven_sort(lo, mid)
        yield from _oddeven_sort(mid + 1, hi)
        yield from _oddeven_merge(lo, hi, 1)


SORT16 = list(_oddeven_sort(0, PEER_TOPK - 1))


def _colmax8(x):
    for sh in (1, 2, 4):
        x = jnp.maximum(x, pltpu.roll(x, sh, 0))
    return x


def _top16_rows(s):
    if s.shape[1] > LANES:
        parts = [_top16_rows(s[:, c:c + LANES]) for c in range(0, s.shape[1], LANES)]
        return [jnp.concatenate([p[r] for p in parts], axis=1) for r in range(PEER_TOPK)]
    ls = [s[8 * r:8 * r + 8, :] for r in range(PEER_TOPK)]
    for i, j in SORT16:
        ls[i], ls[j] = jnp.maximum(ls[i], ls[j]), jnp.minimum(ls[i], ls[j])
    rows = []
    for r in range(PEER_TOPK):
        m = _colmax8(ls[0])
        rows.append(m[0:1, :])
        keep = PEER_TOPK - r - 1
        hit = ls[0] == m
        ls = [jnp.where(hit, ls[d + 1], ls[d]) for d in range(keep)]
    return rows


def _peer_route_kernel(x_ref, g_ref, mod_ref, wq_ref, keys_ref, xnt_ref, e1_ref, th_ref, s2_ref, e2_ref, xn_scr):
    h = pl.program_id(2)

    @pl.when(h == 0)
    def _():
        xn = _modnorm(x_ref[...], g_ref[...], mod_ref, 3, 4)
        xn_scr[...] = xn.astype(BF16)
        xnt_ref[...] = xn.T.astype(BF16)

    q = jnp.dot(xn_scr[...], wq_ref[...], preferred_element_type=F32).astype(BF16)
    half = PEER_DQ // 2
    s1 = _nt(keys_ref[0], q[:, :half])
    s2 = _nt(keys_ref[1], q[:, half:])
    top1 = _top16_rows(s1)
    top2 = _top16_rows(s2)
    pad_rows = [jnp.full_like(top1[0], -jnp.inf)] * (-len(STAIR) % 8)
    cand = jnp.concatenate([top1[a] + top2[b] for a, b in STAIR] + pad_rows, axis=0)
    thr = _top_rows(cand, PEER_TOPK)[-1]
    m1, m2 = top1[0], top2[0]
    zsum = jnp.sum(jnp.where(cand >= thr, jnp.exp(cand - (m1 + m2)), 0.0), axis=0, keepdims=True)
    e1_ref[...] = jnp.exp(s1 - m1) / zsum
    th_ref[...] = thr - s1
    s2_ref[...] = s2
    e2_ref[...] = jnp.exp(s2 - m2)


def _peer_route(x, g, mod, wq, keys):
    b, l, d = x.shape
    tm = min(256, l)
    nt = l // tm
    n = b * l
    hb = pl.BlockSpec((None, PEER_NKEYS, tm), lambda bi, i, h: (h, 0, bi * nt + i))
    return pl.pallas_call(
        _peer_route_kernel,
        grid=(b, nt, PEER_HEADS),
        in_specs=[
            pl.BlockSpec((None, tm, d), lambda bi, i, h: (bi, i, 0)),
            pl.BlockSpec((1, d), lambda bi, i, h: (0, 0)),
            pl.BlockSpec((None, 6, d), lambda bi, i, h: (bi, 0, 0)),
            pl.BlockSpec((d, PEER_DQ), lambda bi, i, h: (0, h)),
            pl.BlockSpec((None, 2, PEER_NKEYS, PEER_DQ // 2), lambda bi, i, h: (h, 0, 0, 0)),
        ],
        out_specs=[pl.BlockSpec((d, tm), lambda bi, i, h: (0, bi * nt + i)), hb, hb, hb, hb],
        out_shape=[jax.ShapeDtypeStruct((d, n), BF16)] + [jax.ShapeDtypeStruct((PEER_HEADS, PEER_NKEYS, n), F32)] * 4,
        scratch_shapes=[pltpu.VMEM((tm, d), BF16)],
        compiler_params=_cparams(("parallel", "parallel", "arbitrary")),
        name="peer_route",
    )(x, g.reshape(1, d), mod, wq, keys)


GELU_C = math.sqrt(2.0 / math.pi)
PEER_ROWS = 8
PEER_VMEM_LIMIT = 58 * 1024 * 1024
GATE_RB = 32
GATE_IG = 4


def _gelu_tanh(x):
    k = -2.0 * GELU_C * LOG2E
    return x / (1.0 + jnp.exp2(x * (k + (k * 0.044715) * (x * x))))


def _peer_pipe_kernel(*refs, final, n_pairs):
    refs = list(refs)
    x_ref, mod_ref, xnt_ref, e1_ref, th_ref, s2_ref, e2_ref, u_ref, vt_ref = refs[:9]
    pos = 9
    if final:
        fg_ref = refs[pos]
        pos += 1
    o_ref, acc_scr = refs[pos:pos + 2]
    a_scr = (refs[pos + 2:pos + 4], refs[pos + 4:pos + 6])
    wg_scr = (refs[pos + 6:pos + 8], refs[pos + 8:pos + 10])
    be1_scr = refs[pos + 10:pos + 12]
    bth_scr = refs[pos + 12:pos + 14]
    g = pl.program_id(2)
    eb = PEER_ROWS * PEER_NKEYS
    tm = s2_ref.shape[2]

    def scores(p):
        for k in range(2):
            a_scr[p][k][...] = jnp.dot(u_ref[k * eb:(k + 1) * eb, :], xnt_ref[...], preferred_element_type=F32)

    def gate_one(a_ref, wg_ref, k):
        zero8 = jnp.zeros((8, tm), F32)
        for h in range(PEER_HEADS):
            for ii in range(PEER_ROWS):
                be1_scr[k][h * PEER_ROWS + ii] = e1_ref[h, pl.ds(k * PEER_ROWS + ii, 1), :] + zero8
                bth_scr[k][h * PEER_ROWS + ii] = th_ref[h, pl.ds(k * PEER_ROWS + ii, 1), :] + zero8
        for c0 in range(0, tm, LANES):
            cs = slice(c0, c0 + LANES)
            for r0 in range(0, PEER_NKEYS, GATE_RB):
                for ig in range(0, PEER_ROWS, GATE_IG):
                    ws = [None] * GATE_IG
                    for h in range(PEER_HEADS):
                        s2 = s2_ref[h, r0:r0 + GATE_RB, cs].reshape(GATE_RB // 8, 8, LANES)
                        e2 = e2_ref[h, r0:r0 + GATE_RB, cs].reshape(GATE_RB // 8, 8, LANES)
                        for j in range(GATE_IG):
                            row = h * PEER_ROWS + ig + j
                            contrib = jnp.where(s2 >= bth_scr[k][row, :, cs][None], e2 * be1_scr[k][row, :, cs][None], 0.0)
                            ws[j] = contrib if ws[j] is None else ws[j] + contrib
                    for j in range(GATE_IG):
                        rs = slice((ig + j) * PEER_NKEYS + r0, (ig + j) * PEER_NKEYS + r0 + GATE_RB)
                        wj = ws[j].reshape(GATE_RB, LANES)
                        wg_ref[rs, cs] = (wj * _gelu_tanh(a_ref[rs, cs])).astype(BF16)

    def gates(p):
        for k in range(2):
            gate_one(a_scr[1 - p][k], wg_scr[p][k], k)

    def accums(p):
        for k in range(2):
            acc_scr[...] += jnp.dot(vt_ref[:, k * eb:(k + 1) * eb], wg_scr[1 - p][k][...], preferred_element_type=F32)

    @pl.when(g == 0)
    def _():
        acc_scr[...] = jnp.zeros_like(acc_scr)
        scores(0)

    @pl.when(g == 1)
    def _():
        scores(1)
        gates(1)

    for p in range(2):
        @pl.when((g >= 2) & (g < n_pairs) & (g % 2 == p))
        def _(p=p):
            scores(p)
            gates(p)
            accums(p)

    @pl.when(g == n_pairs)
    def _():
        gates(n_pairs % 2)
        accums(n_pairs % 2)

    @pl.when(g == n_pairs + 1)
    def _():
        accums((n_pairs + 1) % 2)
        y = x_ref[...] + mod_ref[5:6, :] * acc_scr[...].T
        if final:
            ms = jnp.mean(y * y, axis=-1, keepdims=True)
            y = y * lax.rsqrt(ms + NORM_EPS) * fg_ref[...]
        o_ref[...] = y


def _peer_pipe(x, mod, xnt, e1, th, s2, e2, u, vt, final_g):
    b, l, d = x.shape
    tm = min(512, l)
    nt = l // tm
    eb = PEER_ROWS * PEER_NKEYS
    n_pairs = u.shape[0] // (2 * eb)
    assert n_pairs >= 2
    final = final_g is not None
    tok = lambda bi, i: bi * nt + i
    clamp = lambda v: jnp.clip(v, 0, n_pairs - 1)
    rows = pl.BlockSpec((PEER_HEADS, 2 * PEER_ROWS, tm), lambda bi, i, g: (0, clamp(g - 1), tok(bi, i)))
    keyblk = pl.BlockSpec((PEER_HEADS, PEER_NKEYS, tm), lambda bi, i, g: (0, 0, tok(bi, i)))
    in_specs = [
        pl.BlockSpec((None, tm, d), lambda bi, i, g: (bi, i, 0)),
        pl.BlockSpec((None, 6, d), lambda bi, i, g: (bi, 0, 0)),
        pl.BlockSpec((d, tm), lambda bi, i, g: (0, tok(bi, i))),
        rows, rows, keyblk, keyblk,
        pl.BlockSpec((2 * eb, d), lambda bi, i, g: (clamp(g), 0)),
        pl.BlockSpec((d, 2 * eb), lambda bi, i, g: (0, clamp(g - 2))),
    ]
    args = [x, mod, xnt, e1, th, s2, e2, u, vt]
    if final:
        in_specs.append(pl.BlockSpec((1, d), lambda bi, i, g: (0, 0)))
        args.append(final_g.reshape(1, d))
    bcast = pltpu.VMEM((PEER_HEADS * PEER_ROWS, 8, tm), F32)
    return pl.pallas_call(
        functools.partial(_peer_pipe_kernel, final=final, n_pairs=n_pairs),
        grid=(b, nt, n_pairs + 2),
        in_specs=in_specs,
        out_specs=pl.BlockSpec((None, tm, d), lambda bi, i, g: (bi, i, 0)),
        out_shape=jax.ShapeDtypeStruct((b, l, d), F32),
        scratch_shapes=([pltpu.VMEM((d, tm), F32)] + [pltpu.VMEM((eb, tm), F32)] * 4 + [pltpu.VMEM((eb, tm), BF16)] * 4
                        + [bcast] * 4),
        compiler_params=_cparams(("parallel", "parallel", "arbitrary"), PEER_VMEM_LIMIT),
        name="peer_experts",
    )(*args)


def _pad_cols(a, width):
    return jnp.pad(a, ((0, 0), (0, width - a.shape[1])))


def _layout_w_in(w):
    d = w.shape[0]
    z = lambda n: jnp.zeros((d, n), w.dtype)
    gla, hg, mla, dif = 0, 800, 2080, 2432
    parts = [w[:, gla:gla + 768], w[:, hg:hg + 768],
             w[:, mla:mla + 192], z(64), w[:, mla + 192:mla + 320], z(64), w[:, mla + 320:mla + 352], z(32)]
    for base in (dif, dif + 256, dif + 512):
        for h in range(DIFF_HEADS):
            parts += [w[:, base + 64 * h:base + 64 * (h + 1)], z(64)]
    parts += [w[:, gla + 768:gla + 800], z(224), w[:, hg + 768:hg + 1280]]
    out = jnp.concatenate(parts, axis=1)
    assert out.shape[1] == MAIN_COLS + GATE_COLS
    return out.astype(BF16)


def _layout_mla(w_uq, w_ukv):
    dq = MLA_NOPE + MLA_ROPE
    wq = jnp.concatenate([_pad_cols(w_uq[:, dq * h:dq * (h + 1)], LANES) for h in range(MLA_HEADS)], axis=1)
    wq = jnp.pad(wq, ((0, 256 - MLA_Q_RANK), (0, 0)))
    per = MLA_NOPE + MLA_DV
    wk = jnp.concatenate([_pad_cols(w_ukv[:, per * h:per * h + MLA_NOPE], LANES) for h in range(MLA_HEADS)], axis=1)
    wv = jnp.concatenate([_pad_cols(w_ukv[:, per * h + MLA_NOPE:per * (h + 1)], LANES) for h in range(MLA_HEADS)], axis=1)
    return wq.astype(BF16), jnp.concatenate([wk, wv], axis=1).astype(BF16)


def kernel(x, c, ctx, c_ctx, ada_w, ada_b, norm_mix_g, norm_ffn_g, w_in, w_out, gla_gate_w, gla_gate_b, gla_norm_g, hgrn_lb_raw, hgrn_norm_g, mla_q_norm_g, mla_kv_norm_g, mla_w_uq, mla_w_ukv, diff_lambda, diff_norm_g, peer_wq, peer_keys, peer_u, peer_v, final_norm_g):
    b, l, d = x.shape
    lc = ctx.shape[1]
    depth = ada_w.shape[0]
    assert lc <= ATT_TK, "context keys are consumed as one attention step"

    rows = -(-(b + 1) // 8) * 8
    cpad = jnp.zeros((rows, d), F32).at[:b].set(c).at[b].set(c_ctx)
    mod_all = _adaln(cpad, ada_w, ada_b)

    lb_sm = jax.nn.softmax(hgrn_lb_raw.astype(F32), axis=0)
    lb_all = jnp.cumsum(lb_sm, axis=0) - lb_sm[0]

    lat_mla_tab = _rope_tables(l, 64, 32, True)
    ctx_mla_tab = _rope_tables(lc, 64, 32, False)
    lat_dif_tab = _rope_tables(l, 0, 64, True)
    ctx_dif_tab = _rope_tables(lc, 0, 64, False)

    xc = ctx
    for li in range(depth):
        need_ctx = li < depth - 1
        lam_init = 0.8 - 0.6 * math.exp(-0.3 * li)
        mod_l = mod_all[li, :b].reshape(b, 6, d)
        mod_c = jnp.broadcast_to(mod_all[li, b].reshape(1, 6, d), (b, 6, d))

        w_perm = _layout_w_in(w_in[li])
        w_o = w_out[li].astype(BF16)
        wg = jnp.zeros((2, 256, 128), F32)
        wg = wg.at[0, 0:GLA_GATE_RANK].set(gla_gate_w[li, 0]).at[1, GLA_GATE_RANK:2 * GLA_GATE_RANK].set(gla_gate_w[li, 1])
        wg = wg.astype(BF16)
        bg = gla_gate_b[li].reshape(2, 1, 128)
        gla_gn = jnp.tile(gla_norm_g[li], GLA_HEADS).reshape(1, 256)
        hg_gn = jnp.tile(hgrn_norm_g[li], HG_HEADS).reshape(1, 256)
        lb = lb_all[li]
        hg_p = jnp.zeros((8, 256), F32).at[0].set(jnp.log(jnp.maximum(lb, LB_FLOOR))).at[1].set(jnp.log1p(-lb)).at[2].set(1.0 - lb)
        hg_dummy = jnp.zeros((8, 128), F32)
        gq = jnp.pad(mla_q_norm_g[li], (0, 256 - MLA_Q_RANK)).reshape(1, 256)
        gkv = mla_kv_norm_g[li].reshape(1, MLA_KV_RANK)
        wq_mla, wkv_mla = _layout_mla(mla_w_uq[li], mla_w_ukv[li])
        lq1, lk1, lq2, lk2 = diff_lambda[li, 0], diff_lambda[li, 1], diff_lambda[li, 2], diff_lambda[li, 3]
        lam = (jnp.exp(jnp.sum(lq1 * lk1).astype(F32)) - jnp.exp(jnp.sum(lq2 * lk2).astype(F32)) + lam_init)
        lam_row = jnp.full((1, LANES), lam, F32)
        dif_gn = jnp.tile(diff_norm_g[li], 2).reshape(1, LANES)
        wq_peer = peer_wq[li].astype(BF16)
        keys = peer_keys[li].astype(BF16)
        u_bf = peer_u[li].astype(BF16)
        vt_bf = peer_v[li].astype(BF16).T

        main_c, gate_c = _inproj(xc, norm_mix_g[li], mod_c, w_perm)
        main_l, gate_l = _inproj(x, norm_mix_g[li], mod_l, w_perm)

        mixes_c, mixes_l = [], []
        for mixer, pa_f, pb_f, pa_b, pb_b, gn, dk in (
            ("gla", wg[0], bg[0], wg[1], bg[1], gla_gn, 128),
            ("hg", hg_p, hg_dummy, hg_p, hg_dummy, hg_gn, 256),
        ):
            zero = jnp.zeros((b, 256, dk), F32)
            o_cf, s_f = _scan(main_c, gate_c, pa_f, pb_f, gn, zero, None, mixer=mixer, reverse=False, chunk=64)
            mix_c, s_b = _scan(main_c, gate_c, pa_b, pb_b, gn, zero, o_cf, mixer=mixer, reverse=True, chunk=64)
            o_lf, _ = _scan(main_l, gate_l, pa_f, pb_f, gn, s_f, None, mixer=mixer, reverse=False, chunk=64)
            mix_l, _ = _scan(main_l, gate_l, pa_b, pb_b, gn, s_b, o_lf, mixer=mixer, reverse=True, chunk=64)
            mixes_c.append(mix_c)
            mixes_l.append(mix_l)

        qm_c, km_c, vm_c = _mla_prep(main_c, gq, gkv, wq_mla, wkv_mla, ctx_mla_tab)
        qm_l, km_l, vm_l = _mla_prep(main_l, gq, gkv, wq_mla, wkv_mla, lat_mla_tab)
        mla_l = _attention(qm_l, km_c, vm_c, km_l, vm_l, None, n_maps=1, lam_init=lam_init)
        qd_c, kd_c, vd_c = _diff_prep(main_c, ctx_dif_tab)
        qd_l, kd_l, vd_l = _diff_prep(main_l, lat_dif_tab)
        dif_l = _attention(qd_l, kd_c, vd_c, kd_l, vd_l, (lam_row, dif_gn), n_maps=2, lam_init=lam_init)

        x = _outproj(x, mixes_l[0], mixes_l[1], mla_l, dif_l, w_o, mod_l)
        rt = _peer_route(x, norm_ffn_g[li], mod_l, wq_peer, keys)
        x = _peer_pipe(x, mod_l, *rt, u_bf, vt_bf, None if need_ctx else final_norm_g)

        if need_ctx:
            mla_c = _attention(qm_c, km_c, vm_c, None, None, None, n_maps=1, lam_init=lam_init)
            dif_c = _attention(qd_c, kd_c, vd_c, None, None, (lam_row, dif_gn), n_maps=2, lam_init=lam_init)
            xc = _outproj(xc, mixes_c[0], mixes_c[1], mla_c, dif_c, w_o, mod_c)
            rtc = _peer_route(xc, norm_ffn_g[li], mod_c, wq_peer, keys)
            xc = _peer_pipe(xc, mod_c, *rtc, u_bf, vt_bf, None)
    return x
```
